```python
import math
import jax, jax.numpy as jnp
from jax import lax
import numpy as np

D_MODEL = 2048
BATCH = 4
SEQ = 4096
DEPTH = 4

CHUNK = 64
MIX_W = D_MODEL // 2
N_BRANCH = 3
LRU_W = MIX_W
LRU_BLOCKS = 16
LRU_BW = LRU_W // LRU_BLOCKS
CONV_W = 4
LRU_C = 8.0
ATT_HEADS = 8
ATT_HD = MIX_W // ATT_HEADS
ATT_LEFT_CHUNKS = 8
ATT_BAND = (ATT_LEFT_CHUNKS + 1) * CHUNK
MAX_REL = 128
N_REL = 2 * MAX_REL + 1
SSM_W = MIX_W
SSM_GROUP = 16
SSM_G = SSM_W // SSM_GROUP
SSM_P = 64
FFN_HIDDEN = -(-8 * D_MODEL // (3 * 256)) * 256
IN_SPLITS = [LRU_W, LRU_W, MIX_W, MIX_W, MIX_W, SSM_W]
IN_W = sum(IN_SPLITS) + N_BRANCH * D_MODEL
NORM_EPS = 1e-6
MASK_VALUE = -1e30

kernel_name = "hybrid_gated_lru_chunkattn_s5_encoder"


def rmsnorm(x, g):
    xf = x.astype(jnp.float32)
    xf = xf * lax.rsqrt(jnp.mean(xf * xf, axis=-1, keepdims=True) + NORM_EPS)
    return xf.astype(x.dtype) * g


def causal_dwconv(x, w, b):
    s = x.shape[1]
    xp = jnp.pad(x, ((0, 0), (CONV_W - 1, 0), (0, 0)))
    y = b
    for k in range(CONV_W):
        y = y + xp[:, k:k + s] * w[k]
    return y


def _lin_op(e1, e2):
    a1, b1 = e1
    a2, b2 = e2
    return a1 * a2, a2 * b1 + b2


def _complex_lin_op(e1, e2):
    ar1, ai1, br1, bi1 = e1
    ar2, ai2, br2, bi2 = e2
    return (ar2 * ar1 - ai2 * ai1,
            ar2 * ai1 + ai2 * ar1,
            ar2 * br1 - ai2 * bi1 + br2,
            ar2 * bi1 + ai2 * br1 + bi2)


def rg_lru_branch(xin, gate_in, conv_w, conv_b, wa, ba, wx, bx, lam):
    b_, s_, _ = xin.shape
    xc = causal_dwconv(xin, conv_w, conv_b).astype(jnp.float32)
    xb = xc.reshape(b_, s_, LRU_BLOCKS, LRU_BW)
    r = jax.nn.sigmoid(jnp.einsum("bsnk,nkj->bsnj", xb, wa.astype(jnp.float32)).reshape(b_, s_, LRU_W) + ba)
    i = jax.nn.sigmoid(jnp.einsum("bsnk,nkj->bsnj", xb, wx.astype(jnp.float32)).reshape(b_, s_, LRU_W) + bx)
    log_a = -LRU_C * r * jax.nn.softplus(-lam.astype(jnp.float32))
    a = jnp.exp(log_a)
    inp = jnp.sqrt(-jnp.expm1(2.0 * log_a)) * (i * xc)
    h = lax.associative_scan(_lin_op, (a, inp), axis=1)[1]
    return (h * jax.nn.gelu(gate_in.astype(jnp.float32))).astype(xin.dtype)


def chunk_attention_branch(q, k, v, rel_bias):
    b_, s_, _ = q.shape
    n_c = s_ // CHUNK
    qc = q.reshape(b_, n_c, CHUNK, ATT_HEADS, ATT_HD)
    pad = ((0, 0), (ATT_LEFT_CHUNKS, 0), (0, 0), (0, 0), (0, 0))
    kp = jnp.pad(k.reshape(b_, n_c, CHUNK, ATT_HEADS, ATT_HD), pad)
    vp = jnp.pad(v.reshape(b_, n_c, CHUNK, ATT_HEADS, ATT_HD), pad)
    band_idx = jnp.arange(n_c)[:, None] + jnp.arange(ATT_LEFT_CHUNKS + 1)[None, :]
    kb = kp[:, band_idx].reshape(b_, n_c, ATT_BAND, ATT_HEADS, ATT_HD)
    vb = vp[:, band_idx].reshape(b_, n_c, ATT_BAND, ATT_HEADS, ATT_HD)
    scores = jnp.einsum("bcqhd,bckhd->bchqk", qc, kb).astype(jnp.float32) * (ATT_HD ** -0.5)
    q_pos = ATT_LEFT_CHUNKS * CHUNK + jnp.arange(CHUNK)
    k_pos = jnp.arange(ATT_BAND)
    rel = jnp.clip(q_pos[:, None] - k_pos[None, :], -MAX_REL, MAX_REL) + MAX_REL
    bias = rel_bias.astype(jnp.float32)[:, rel]
    key_abs = (jnp.arange(n_c)[:, None] - ATT_LEFT_CHUNKS) * CHUNK + k_pos[None, :]
    valid = key_abs >= 0
    scores = jnp.where(valid[None, :, None, None, :], scores + bias[None, None], MASK_VALUE)
    p = jax.nn.softmax(scores, axis=-1).astype(v.dtype)
    o = jnp.einsum("bchqk,bckhd->bcqhd", p, vb)
    return o.reshape(b_, s_, MIX_W)


def s5_branch(u, a_re, a_im, b_re, b_im, c_re, c_im, d, log_step):
    b_, s_, _ = u.shape
    uf = u.astype(jnp.float32)
    ug = uf.reshape(b_, s_, SSM_G, SSM_GROUP)
    a_re = a_re.astype(jnp.float32)
    a_im = a_im.astype(jnp.float32)
    step = jnp.exp(log_step.astype(jnp.float32))[:, None]
    mag = jnp.exp(a_re * step)
    ang = a_im * step
    lb_re = mag * jnp.cos(ang)
    lb_im = mag * jnp.sin(ang)
    den = a_re * a_re + a_im * a_im
    nr = lb_re - 1.0
    coef_re = (nr * a_re + lb_im * a_im) / den
    coef_im = (lb_im * a_re - nr * a_im) / den
    b_re = b_re.astype(jnp.float32)
    b_im = b_im.astype(jnp.float32)
    bb_re = coef_re[..., None] * b_re - coef_im[..., None] * b_im
    bb_im = coef_re[..., None] * b_im + coef_im[..., None] * b_re
    bu_re = jnp.einsum("bsgh,gph->bsgp", ug, bb_re)
    bu_im = jnp.einsum("bsgh,gph->bsgp", ug, bb_im)
    ar = jnp.broadcast_to(lb_re, bu_re.shape)
    ai = jnp.broadcast_to(lb_im, bu_re.shape)
    _, _, xs_re, xs_im = lax.associative_scan(_complex_lin_op, (ar, ai, bu_re, bu_im), axis=1)
    y = (jnp.einsum("bsgp,ghp->bsgh", xs_re, c_re.astype(jnp.float32))
         - jnp.einsum("bsgp,ghp->bsgh", xs_im, c_im.astype(jnp.float32)))
    y = y.reshape(b_, s_, SSM_W) + d.astype(jnp.float32) * uf
    return y.astype(u.dtype)


def setup_inputs(seed: int = 0) -> dict:
    key = jax.random.key(seed)
    ks = jax.random.split(key, 32)
    f32 = jnp.float32

    def nrm(k, shape, scale):
        return jax.random.normal(k, shape, f32) * scale

    u_lam = jax.random.uniform(ks[8], (DEPTH, LRU_W), f32, 0.9, 0.999)
    p_lam = u_lam ** (1.0 / LRU_C)
    lru_lambda = jnp.log(p_lam) - jnp.log1p(-p_lam)
    n_idx = jnp.arange(SSM_P, dtype=f32)
    return {
        "x": nrm(ks[0], (BATCH, SEQ, D_MODEL), 1.0),
        "norm_mix_g": 1.0 + nrm(ks[1], (DEPTH, D_MODEL), 0.01),
        "w_in": nrm(ks[2], (DEPTH, D_MODEL, IN_W), D_MODEL ** -0.5),
        "gate_bias": nrm(ks[3], (DEPTH, N_BRANCH, D_MODEL), 0.01),
        "lru_conv_w": nrm(ks[4], (DEPTH, CONV_W, LRU_W), CONV_W ** -0.5),
        "lru_conv_b": nrm(ks[5], (DEPTH, LRU_W), 0.01),
        "lru_wa": nrm(ks[6], (DEPTH, LRU_BLOCKS, LRU_BW, LRU_BW), LRU_BW ** -0.5),
        "lru_ba": nrm(ks[7], (DEPTH, LRU_W), 0.01),
        "lru_wx": nrm(ks[9], (DEPTH, LRU_BLOCKS, LRU_BW, LRU_BW), LRU_BW ** -0.5),
        "lru_bx": nrm(ks[10], (DEPTH, LRU_W), 0.01),
        "lru_lambda": lru_lambda,
        "attn_rel_bias": nrm(ks[11], (DEPTH, ATT_HEADS, N_REL), 0.1),
        "ssm_a_re": -0.5 + nrm(ks[12], (DEPTH, SSM_G, SSM_P), 0.01),
        "ssm_a_im": math.pi * n_idx + nrm(ks[13], (DEPTH, SSM_G, SSM_P), 0.01),
        "ssm_b_re": nrm(ks[14], (DEPTH, SSM_G, SSM_P, SSM_GROUP), (2 * SSM_GROUP) ** -0.5),
        "ssm_b_im": nrm(ks[15], (DEPTH, SSM_G, SSM_P, SSM_GROUP), (2 * SSM_GROUP) ** -0.5),
        "ssm_c_re": nrm(ks[16], (DEPTH, SSM_G, SSM_GROUP, SSM_P), (2 * SSM_P) ** -0.5),
        "ssm_c_im": nrm(ks[17], (DEPTH, SSM_G, SSM_GROUP, SSM_P), (2 * SSM_P) ** -0.5),
        "ssm_d": nrm(ks[18], (DEPTH, SSM_W), 1.0),
        "ssm_log_step": jax.random.uniform(ks[19], (DEPTH, SSM_G), f32, math.log(1e-3), math.log(1e-1)),
        "ssm_w_glu": nrm(ks[20], (DEPTH, SSM_W, D_MODEL), SSM_W ** -0.5),
        "w_branch": nrm(ks[21], (DEPTH, N_BRANCH, MIX_W, D_MODEL), MIX_W ** -0.5),
        "w_out": nrm(ks[22], (DEPTH, D_MODEL, D_MODEL), D_MODEL ** -0.5),
        "norm_ffn_g": 1.0 + nrm(ks[23], (DEPTH, D_MODEL), 0.01),
        "w_ffn_gate": nrm(ks[24], (DEPTH, D_MODEL, FFN_HIDDEN), D_MODEL ** -0.5),
        "w_ffn_up": nrm(ks[25], (DEPTH, D_MODEL, FFN_HIDDEN), D_MODEL ** -0.5),
        "w_ffn_down": nrm(ks[26], (DEPTH, FFN_HIDDEN, D_MODEL), FFN_HIDDEN ** -0.5),
        "norm_final_g": 1.0 + nrm(ks[27], (D_MODEL,), 0.01),
    }


def reference(x, norm_mix_g, w_in, gate_bias, lru_conv_w, lru_conv_b, lru_wa, lru_ba, lru_wx, lru_bx,
              lru_lambda, attn_rel_bias, ssm_a_re, ssm_a_im, ssm_b_re, ssm_b_im, ssm_c_re, ssm_c_im,
              ssm_d, ssm_log_step, ssm_w_glu, w_branch, w_out, norm_ffn_g, w_ffn_gate, w_ffn_up,
              w_ffn_down, norm_final_g):
    b_, s_, _ = x.shape
    split_pts = [int(p) for p in np.cumsum(IN_SPLITS)]
    for l in range(DEPTH):
        h = rmsnorm(x, norm_mix_g[l])
        proj = h @ w_in[l]
        lru_x, lru_gate, q, k, v, ssm_u, gates = jnp.split(proj, split_pts, axis=-1)
        y_a = rg_lru_branch(lru_x, lru_gate, lru_conv_w[l], lru_conv_b[l], lru_wa[l], lru_ba[l],
                            lru_wx[l], lru_bx[l], lru_lambda[l])
        y_b = chunk_attention_branch(q, k, v, attn_rel_bias[l])
        y_c = jax.nn.gelu(s5_branch(ssm_u, ssm_a_re[l], ssm_a_im[l], ssm_b_re[l], ssm_b_im[l],
                                    ssm_c_re[l], ssm_c_im[l], ssm_d[l], ssm_log_step[l]))
        br_a = y_a @ w_branch[l, 0]
        br_b = y_b @ w_branch[l, 1]
        br_c = (y_c @ w_branch[l, 2]) * jax.nn.sigmoid(y_c @ ssm_w_glu[l])
        g = jax.nn.sigmoid(gates.reshape(b_, s_, N_BRANCH, D_MODEL) + gate_bias[l])
        merged = g[:, :, 0] * br_a + g[:, :, 1] * br_b + g[:, :, 2] * br_c
        x = x + merged @ w_out[l]
        h = rmsnorm(x, norm_ffn_g[l])
        x = x + (jax.nn.silu(h @ w_ffn_gate[l]) * (h @ w_ffn_up[l])) @ w_ffn_down[l]
    return rmsnorm(x, norm_final_g)
```

```python
import functools
import math

import jax
import jax.numpy as jnp
from jax import lax
from jax.experimental import pallas as pl
from jax.experimental.pallas import tpu as pltpu

F32 = jnp.float32
BF16 = jnp.bfloat16

CHUNK = 64
LRU_BW = 64
CONV_W = 4
LRU_C = 8.0
ATT_HD = 128
ATT_LEFT_CHUNKS = 8
MAX_REL = 128
N_REL = 2 * MAX_REL + 1
SSM_GROUP = 16
SSM_P = 64
N_BRANCH = 3
NORM_EPS = 1e-6
MASK_VALUE = -1e30

LANES = 128
SUBLANES = 8
MXU_DIM = 256
VMEM_LIMIT_BYTES = 56 * 1024 * 1024

S5_BLK = MXU_DIM // SSM_GROUP
S5_GROUPS_PER_STEP = 16
ATT_QBLK = 4 * CHUNK
ATT_KWIN = 3 * ATT_QBLK
LRU_CW = MXU_DIM


def _cparams(sem):
    return pltpu.CompilerParams(dimension_semantics=sem, vmem_limit_bytes=VMEM_LIMIT_BYTES)


def _rmsnorm_rows(x, g):
    ms = jnp.mean(x * x, axis=-1, keepdims=True)
    return x * lax.rsqrt(ms + NORM_EPS) * g


def _norm_matmul_body(x_ref, g_ref, w_ref, o_ref, h_ref):
    @pl.when(pl.program_id(1) == 0)
    def _():
        h_ref[...] = _rmsnorm_rows(x_ref[...], g_ref[...]).astype(BF16)

    o_ref[...] = jnp.dot(h_ref[...], w_ref[...], preferred_element_type=F32).astype(o_ref.dtype)


def norm_matmul(x, g, w, *, tm, tn, out_dtype=BF16):
    m, k = x.shape
    n = w.shape[1]
    return pl.pallas_call(
        _norm_matmul_body,
        grid=(m // tm, n // tn),
        in_specs=[
            pl.BlockSpec((tm, k), lambda i, j: (i, 0)),
            pl.BlockSpec((1, k), lambda i, j: (0, 0)),
            pl.BlockSpec((k, tn), lambda i, j: (0, j)),
        ],
        out_specs=pl.BlockSpec((tm, tn), lambda i, j: (i, j)),
        out_shape=jax.ShapeDtypeStruct((m, n), out_dtype),
        scratch_shapes=[pltpu.VMEM((tm, k), BF16)],
        compiler_params=_cparams(("parallel", "arbitrary")),
        name="norm_matmul",
    )(x, g.reshape(1, k), w)


def _softplus(x):
    return jnp.maximum(x, 0.0) + jnp.log1p(jnp.exp(-jnp.abs(x)))


def _lru_body(x_ref, gate_ref, cw_ref, cb_ref, wa_ref, wx_ref, ba_ref, bx_ref, lam_ref,
              o_ref, xbuf_ref, h_ref, *, ts):
    t = pl.program_id(2)

    @pl.when(t == 0)
    def _():
        xbuf_ref[0:SUBLANES, :] = jnp.zeros((SUBLANES, LRU_CW), F32)
        h_ref[...] = jnp.zeros_like(h_ref)

    x = x_ref[...].astype(F32)
    xbuf_ref[SUBLANES:SUBLANES + ts, :] = x
    xc = cb_ref[...] + x * cw_ref[CONV_W - 1:CONV_W, :]
    for k in range(CONV_W - 1):
        back = CONV_W - 1 - k
        xc = xc + xbuf_ref[pl.ds(SUBLANES - back, ts), :] * cw_ref[k:k + 1, :]
    xbuf_ref[0:SUBLANES, :] = x[ts - SUBLANES:ts, :]

    xcb = xc.astype(BF16)
    r = jax.nn.sigmoid(jnp.dot(xcb, wa_ref[...], preferred_element_type=F32) + ba_ref[...])
    i = jax.nn.sigmoid(jnp.dot(xcb, wx_ref[...], preferred_element_type=F32) + bx_ref[...])
    log_a = (-LRU_C) * r * _softplus(-lam_ref[...])
    a = jnp.exp(log_a)
    th = jnp.tanh(log_a)
    b = jnp.sqrt((-2.0 * th) / (1.0 - th)) * (i * xc)

    row = lax.broadcasted_iota(jnp.int32, (ts, LRU_CW), 0)
    s = 1
    while s < ts:
        keep = row >= s
        b_sh = pltpu.roll(b, s, 0)
        a_sh = pltpu.roll(a, s, 0)
        b = b + jnp.where(keep, a * b_sh, 0.0)
        a = jnp.where(keep, a * a_sh, a)
        s *= 2
    h = b + a * h_ref[0:1, :]
    h_ref[...] = h[ts - 1:ts, :]
    o_ref[...] = (h * jax.nn.gelu(gate_ref[...].astype(F32))).astype(o_ref.dtype)


def _block_diag_tiles(w):
    n_blk, bw, _ = w.shape
    per = LRU_CW // bw
    wt = w.reshape(n_blk // per, per, bw, bw)
    eye = jnp.eye(per, dtype=w.dtype)
    tiles = jnp.einsum("cpkj,pq->cpkqj", wt, eye)
    return tiles.reshape(n_blk // per, LRU_CW, LRU_CW)


def lru_branch(proj3, x_col0, gate_col0, conv_w, conv_b, wa, ba, wx, bx, lam, *, ts):
    b_, s_, _ = proj3.shape
    width = conv_w.shape[1]
    n_c = width // LRU_CW
    xc0 = x_col0 // LRU_CW
    gc0 = gate_col0 // LRU_CW
    row = lambda v: v.reshape(1, width).astype(F32)
    vec_spec = pl.BlockSpec((1, LRU_CW), lambda b, c, t: (0, c))
    return pl.pallas_call(
        functools.partial(_lru_body, ts=ts),
        grid=(b_, n_c, s_ // ts),
        in_specs=[
            pl.BlockSpec((None, ts, LRU_CW), lambda b, c, t: (b, t, xc0 + c)),
            pl.BlockSpec((None, ts, LRU_CW), lambda b, c, t: (b, t, gc0 + c)),
            pl.BlockSpec((CONV_W, LRU_CW), lambda b, c, t: (0, c)),
            vec_spec,
            pl.BlockSpec((None, LRU_CW, LRU_CW), lambda b, c, t: (c, 0, 0)),
            pl.BlockSpec((None, LRU_CW, LRU_CW), lambda b, c, t: (c, 0, 0)),
            vec_spec, vec_spec, vec_spec,
        ],
        out_specs=pl.BlockSpec((None, ts, LRU_CW), lambda b, c, t: (b, t, c)),
        out_shape=jax.ShapeDtypeStruct((b_, s_, width), BF16),
        scratch_shapes=[pltpu.VMEM((ts + SUBLANES, LRU_CW), F32), pltpu.VMEM((1, LRU_CW), F32)],
        compiler_params=_cparams(("parallel", "parallel", "arbitrary")),
        name="rg_lru",
    )(proj3, proj3, conv_w.astype(F32), row(conv_b), _block_diag_tiles(wa).astype(BF16),
      _block_diag_tiles(wx).astype(BF16), row(ba), row(bx), row(lam))


def _rel_bias_body(tab_ref, o_ref):
    h = pl.program_id(0)
    width = 4 * ATT_QBLK
    m = lax.broadcasted_iota(jnp.int32, (1, width), 1)
    delta = jnp.where(m < ATT_KWIN, m, m - width)
    idx = jnp.clip(2 * ATT_QBLK - delta, -MAX_REL, MAX_REL) + MAX_REL

    def pick(d, acc):
        return jnp.where(idx == d, tab_ref[h, d], acc)

    gvec = lax.fori_loop(0, N_REL, pick, jnp.zeros((1, width), F32))
    full = pltpu.roll(jnp.broadcast_to(gvec, (ATT_QBLK, width)), 0, 1, stride=1, stride_axis=0)
    o_ref[...] = full[:, :ATT_KWIN]


def rel_bias_matrix(rel_bias):
    n = rel_bias.shape[0]
    return pl.pallas_call(
        _rel_bias_body,
        grid=(n,),
        in_specs=[pl.BlockSpec(memory_space=pltpu.SMEM)],
        out_specs=pl.BlockSpec((None, ATT_QBLK, ATT_KWIN), lambda h: (h, 0, 0)),
        out_shape=jax.ShapeDtypeStruct((n, ATT_QBLK, ATT_KWIN), F32),
        compiler_params=_cparams(("arbitrary",)),
        name="rel_bias",
    )(rel_bias.astype(F32))


def _attn_body(q_ref, k0_ref, k1_ref, k2_ref, v0_ref, v1_ref, v2_ref, bias_ref, o_ref):
    qb = pl.program_id(2)
    q = q_ref[...]
    nt = (((1,), (1,)), ((), ()))
    s = jnp.concatenate(
        [lax.dot_general(q, k_ref[...], nt, preferred_element_type=F32)
         for k_ref in (k0_ref, k1_ref, k2_ref)], axis=1)
    s = s * (ATT_HD ** -0.5) + bias_ref[...]
    chunk_shift = int(math.log2(CHUNK))
    qc = lax.shift_right_logical(lax.broadcasted_iota(jnp.int32, s.shape, 0), chunk_shift)
    kc = lax.shift_right_logical(lax.broadcasted_iota(jnp.int32, s.shape, 1), chunk_shift)
    first_chunk = (qb - 2) * (ATT_QBLK // CHUNK)
    valid = (kc >= qc) & (kc <= qc + ATT_LEFT_CHUNKS) & (kc + first_chunk >= 0)
    s = jnp.where(valid, s, MASK_VALUE)
    m = jnp.max(s, axis=-1, keepdims=True)
    p = jnp.exp(s - m)
    l = jnp.sum(p, axis=-1, keepdims=True)
    pb = p.astype(BF16)
    o = jnp.dot(pb[:, 0:ATT_QBLK], v0_ref[...], preferred_element_type=F32)
    o = o + jnp.dot(pb[:, ATT_QBLK:2 * ATT_QBLK], v1_ref[...], preferred_element_type=F32)
    o = o + jnp.dot(pb[:, 2 * ATT_QBLK:], v2_ref[...], preferred_element_type=F32)
    o_ref[...] = (o / l).astype(o_ref.dtype)


def attention_branch(proj3, q_col0, k_col0, v_col0, bias, layer, n_heads):
    b_, s_, _ = proj3.shape
    qc0, kc0, vc0 = q_col0 // ATT_HD, k_col0 // ATT_HD, v_col0 // ATT_HD

    def kv_spec(col0, back):
        return pl.BlockSpec((None, ATT_QBLK, ATT_HD),
                            lambda h, b, i: (b, jnp.maximum(i - back, 0), col0 + h))

    return pl.pallas_call(
        _attn_body,
        grid=(n_heads, b_, s_ // ATT_QBLK),
        in_specs=[
            pl.BlockSpec((None, ATT_QBLK, ATT_HD), lambda h, b, i: (b, i, qc0 + h)),
            kv_spec(kc0, 2), kv_spec(kc0, 1), kv_spec(kc0, 0),
            kv_spec(vc0, 2), kv_spec(vc0, 1), kv_spec(vc0, 0),
            pl.BlockSpec((None, ATT_QBLK, ATT_KWIN), lambda h, b, i: (layer * n_heads + h, 0, 0)),
        ],
        out_specs=pl.BlockSpec((None, ATT_QBLK, ATT_HD), lambda h, b, i: (b, i, h)),
        out_shape=jax.ShapeDtypeStruct((b_, s_, n_heads * ATT_HD), BF16),
        compiler_params=_cparams(("parallel", "parallel", "parallel")),
        name="chunk_attention",
    )(proj3, proj3, proj3, proj3, proj3, proj3, proj3, bias)


def _s5_prep_body(are2, aim2, are_c, aim_c, lstep, bta, btb, ctre, ctim, w1_ref, m_ref, lam_ref):
    step = jnp.exp(lstep[...])
    blk, grp, npp = S5_BLK, SSM_GROUP, SSM_P
    bw = blk * grp

    def lam_pow(are, aim, n):
        ang = aim * step * n
        mag = jnp.exp(are * step * n)
        return mag * jnp.cos(ang), mag * jnp.sin(ang)

    ar, ai = are2[...], aim2[...]
    lr, li = lam_pow(ar, ai, 1.0)
    den = ar * ar + ai * ai
    nr = lr - 1.0
    cr = (nr * ar + li * ai) / den
    ci = (li * ar - nr * ai) / den
    ba, bb = bta[...], btb[...]
    bb2 = cr * ba + ci * bb
    bb2s = cr * bb - ci * ba

    irow = lax.shift_right_logical(lax.broadcasted_iota(jnp.int32, (bw, 2 * npp), 0),
                                   int(math.log2(grp))).astype(F32)
    pr, pi = lam_pow(ar, ai, (blk - 1.0) - irow)
    n_state = pr * jnp.tile(bb2, (blk, 1)) + pi * jnp.tile(bb2s, (blk, 1))

    acr, aci = are_c[...], aim_c[...]
    tau = lax.shift_right_logical(lax.broadcasted_iota(jnp.int32, (npp, bw), 1),
                                  int(math.log2(grp))).astype(F32)
    cre, cim = ctre[...], ctim[...]

    def c_lam(n):
        qr, qi = lam_pow(acr, aci, n)
        return jnp.concatenate([cre * qr - cim * qi, -(cre * qi + cim * qr)], axis=0)

    m_ref[...] = c_lam(tau + 1.0).astype(BF16)
    ktrow = jnp.dot(bb2, c_lam(tau), preferred_element_type=F32, precision=lax.Precision.HIGHEST)
    lane = lax.broadcasted_iota(jnp.int32, (grp, bw), 1)
    rows = [ktrow]
    for i in range(1, blk):
        rows.append(jnp.where(lane >= i * grp, pltpu.roll(ktrow, i * grp, 1), 0.0))
    w1_ref[:, 0:bw] = jnp.concatenate(rows, axis=0).astype(BF16)
    w1_ref[:, bw:] = n_state.astype(BF16)

    lbr, lbi = lam_pow(ar, ai, float(blk))
    sign = jnp.where(lax.broadcasted_iota(jnp.int32, (1, 2 * npp), 1) < npp, -1.0, 1.0)
    lam_ref[...] = jnp.zeros_like(lam_ref)
    lam_ref[0:1, :] = lbr
    lam_ref[1:2, :] = sign * lbi


def s5_prepare(a_re, a_im, b_re, b_im, c_re, c_im, log_step):
    n_g = a_re.shape[0]
    blk, grp, npp = S5_BLK, SSM_GROUP, SSM_P
    bw = blk * grp
    f = lambda v: v.astype(F32)
    pair = lambda v: jnp.concatenate([f(v), f(v)], axis=-1).reshape(n_g, 1, 2 * npp)
    btr, bti = f(b_re).transpose(0, 2, 1), f(b_im).transpose(0, 2, 1)
    ct = lambda v: jnp.tile(f(v).transpose(0, 2, 1), (1, 1, blk))
    g3 = lambda shp: pl.BlockSpec((None,) + shp, lambda g: (g, 0, 0))
    return pl.pallas_call(
        _s5_prep_body,
        grid=(n_g,),
        in_specs=[g3((1, 2 * npp)), g3((1, 2 * npp)), g3((npp, 1)), g3((npp, 1)), g3((1, 1)),
                  g3((grp, 2 * npp)), g3((grp, 2 * npp)), g3((npp, bw)), g3((npp, bw))],
        out_specs=[g3((bw, bw + 2 * npp)), g3((2 * npp, bw)), g3((SUBLANES, 2 * npp))],
        out_shape=[jax.ShapeDtypeStruct((n_g, bw, bw + 2 * npp), BF16),
                   jax.ShapeDtypeStruct((n_g, 2 * npp, bw), BF16),
                   jax.ShapeDtypeStruct((n_g, SUBLANES, 2 * npp), F32)],
        compiler_params=_cparams(("parallel",)),
        name="s5_prepare",
    )(pair(a_re), pair(a_im), f(a_re).reshape(n_g, npp, 1), f(a_im).reshape(n_g, npp, 1),
      f(log_step).reshape(n_g, 1, 1), jnp.concatenate([btr, bti], axis=-1),
      jnp.concatenate([-bti, btr], axis=-1), ct(c_re), ct(c_im))


def _s5_body(u_ref, w1_ref, m_ref, lam_ref, d_ref, o_ref, *, rows):
    bw = S5_BLK * SSM_GROUP
    sw = 2 * SSM_P
    row = lax.broadcasted_iota(jnp.int32, (rows, sw), 0)
    for g in range(S5_GROUPS_PER_STEP):
        cols = slice(g * bw, (g + 1) * bw)
        u = u_ref[:, cols]
        r = jnp.dot(u, w1_ref[g], preferred_element_type=F32)
        y = r[:, 0:bw]
        z = r[:, bw:]
        ca = lam_ref[g, 0:1, :]
        cb = lam_ref[g, 1:2, :]
        s = 1
        while s < rows:
            zs = jnp.where(row >= s, pltpu.roll(z, s, 0), 0.0)
            z = z + ca * zs + cb * pltpu.roll(zs, SSM_P, 1)
            ca, cb = ca * ca - cb * cb, 2.0 * ca * cb
            s *= 2
        zprev = jnp.where(row >= 1, pltpu.roll(z, 1, 0), 0.0)
        y = y + jnp.dot(zprev.astype(BF16), m_ref[g], preferred_element_type=F32)
        y = y + d_ref[:, cols] * u.astype(F32)
        o_ref[:, cols] = jax.nn.gelu(y).astype(o_ref.dtype)


def s5_branch(ublk, w1, mm, lam, d_blk, *, batch):
    n_rows, width = ublk.shape
    rows = n_rows // batch
    n_g = w1.shape[0]
    gs = S5_GROUPS_PER_STEP
    bw = S5_BLK * SSM_GROUP
    return pl.pallas_call(
        functools.partial(_s5_body, rows=rows),
        grid=(batch, n_g // gs),
        in_specs=[
            pl.BlockSpec((rows, gs * bw), lambda b, c: (b, c)),
            pl.BlockSpec((gs,) + w1.shape[1:], lambda b, c: (c, 0, 0)),
            pl.BlockSpec((gs,) + mm.shape[1:], lambda b, c: (c, 0, 0)),
            pl.BlockSpec((gs,) + lam.shape[1:], lambda b, c: (c, 0, 0)),
            pl.BlockSpec((1, gs * bw), lambda b, c: (0, c)),
        ],
        out_specs=pl.BlockSpec((rows, gs * bw), lambda b, c: (b, c)),
        out_shape=jax.ShapeDtypeStruct((n_rows, width), BF16),
        compiler_params=_cparams(("parallel", "parallel")),
        name="s5_blocked",
    )(ublk, w1, mm, lam, d_blk)


def _merge_body(ya_ref, yb_ref, yc_ref, w0_ref, w1_ref, w2_ref, wg_ref,
                g0_ref, g1_ref, g2_ref, gb_ref, o_ref):
    dot = lambda a, w: jnp.dot(a[...], w[...], preferred_element_type=F32)
    br_a = dot(ya_ref, w0_ref)
    br_b = dot(yb_ref, w1_ref)
    yc = yc_ref[...]
    br_c = jnp.dot(yc, w2_ref[...], preferred_element_type=F32) * jax.nn.sigmoid(
        jnp.dot(yc, wg_ref[...], preferred_element_type=F32))
    gate = lambda g_ref, i: jax.nn.sigmoid(g_ref[...].astype(F32) + gb_ref[i:i + 1, :])
    merged = gate(g0_ref, 0) * br_a + gate(g1_ref, 1) * br_b + gate(g2_ref, 2) * br_c
    o_ref[...] = merged.astype(o_ref.dtype)


def branch_merge(ya, yb, yc, w_branch, w_glu, proj, gate_col0, gate_bias, *, tm, tn):
    m, kdim = ya.shape
    n = w_glu.shape[1]
    gc0 = gate_col0 // tn
    nj = n // tn
    y_spec = pl.BlockSpec((tm, kdim), lambda i, j: (i, 0))
    wb_spec = lambda br: pl.BlockSpec((None, kdim, tn), lambda i, j: (br, 0, j))
    g_spec = lambda br: pl.BlockSpec((tm, tn), lambda i, j: (i, gc0 + br * nj + j))
    return pl.pallas_call(
        _merge_body,
        grid=(m // tm, nj),
        in_specs=[y_spec, y_spec, y_spec, wb_spec(0), wb_spec(1), wb_spec(2),
                  pl.BlockSpec((kdim, tn), lambda i, j: (0, j)),
                  g_spec(0), g_spec(1), g_spec(2),
                  pl.BlockSpec((N_BRANCH, tn), lambda i, j: (0, j))],
        out_specs=pl.BlockSpec((tm, tn), lambda i, j: (i, j)),
        out_shape=jax.ShapeDtypeStruct((m, n), BF16),
        compiler_params=_cparams(("parallel", "arbitrary")),
        name="branch_merge",
    )(ya, yb, yc, w_branch, w_branch, w_branch, w_glu, proj, proj, proj, gate_bias.astype(F32))


def _matmul_res_body(a_ref, w_ref, r_ref, o_ref):
    o_ref[...] = r_ref[...] + jnp.dot(a_ref[...], w_ref[...], preferred_element_type=F32)


def matmul_residual(a, w, res, *, tm, tn):
    m, k = a.shape
    n = w.shape[1]
    return pl.pallas_call(
        _matmul_res_body,
        grid=(m // tm, n // tn),
        in_specs=[pl.BlockSpec((tm, k), lambda i, j: (i, 0)),
                  pl.BlockSpec((k, tn), lambda i, j: (0, j)),
                  pl.BlockSpec((tm, tn), lambda i, j: (i, j))],
        out_specs=pl.BlockSpec((tm, tn), lambda i, j: (i, j)),
        out_shape=jax.ShapeDtypeStruct((m, n), F32),
        compiler_params=_cparams(("parallel", "arbitrary")),
        name="matmul_residual",
    )(a, w, res)


def _ffn_body(x_ref, g_ref, wg_ref, wu_ref, wd_ref, o_ref, h_ref):
    f = pl.program_id(1)

    @pl.when(f == 0)
    def _():
        x = x_ref[...]
        h_ref[...] = _rmsnorm_rows(x, g_ref[...]).astype(BF16)
        o_ref[...] = x

    h = h_ref[...]
    gate = jnp.dot(h, wg_ref[...], preferred_element_type=F32)
    up = jnp.dot(h, wu_ref[...], preferred_element_type=F32)
    act = (jax.nn.silu(gate) * up).astype(BF16)
    o_ref[...] += jnp.dot(act, wd_ref[...], preferred_element_type=F32)


def ffn(x, g, w_gate, w_up, w_down, *, tm, tf):
    m, d = x.shape
    hidden = w_gate.shape[1]
    return pl.pallas_call(
        _ffn_body,
        grid=(m // tm, hidden // tf),
        in_specs=[pl.BlockSpec((tm, d), lambda i, f: (i, 0)),
                  pl.BlockSpec((1, d), lambda i, f: (0, 0)),
                  pl.BlockSpec((d, tf), lambda i, f: (0, f)),
                  pl.BlockSpec((d, tf), lambda i, f: (0, f)),
                  pl.BlockSpec((tf, d), lambda i, f: (f, 0))],
        out_specs=pl.BlockSpec((tm, d), lambda i, f: (i, 0)),
        out_shape=jax.ShapeDtypeStruct((m, d), F32),
        scratch_shapes=[pltpu.VMEM((tm, d), BF16)],
        compiler_params=_cparams(("parallel", "arbitrary")),
        name="swiglu_ffn",
    )(x, g.reshape(1, d), w_gate, w_up, w_down)


def _final_norm_body(x_ref, g_ref, o_ref):
    o_ref[...] = _rmsnorm_rows(x_ref[...], g_ref[...])


def final_norm(x, g, *, tm):
    m, d = x.shape
    return pl.pallas_call(
        _final_norm_body,
        grid=(m // tm,),
        in_specs=[pl.BlockSpec((tm, d), lambda i: (i, 0)), pl.BlockSpec((1, d), lambda i: (0, 0))],
        out_specs=pl.BlockSpec((tm, d), lambda i: (i, 0)),
        out_shape=jax.ShapeDtypeStruct((m, d), F32),
        compiler_params=_cparams(("parallel",)),
        name="final_rmsnorm",
    )(x, g.reshape(1, d))


def _pick_tile(n, want):
    t = min(n, want)
    while n % t:
        t //= 2
    return t


def kernel(x, norm_mix_g, w_in, gate_bias, lru_conv_w, lru_conv_b, lru_wa, lru_ba, lru_wx, lru_bx,
           lru_lambda, attn_rel_bias, ssm_a_re, ssm_a_im, ssm_b_re, ssm_b_im, ssm_c_re, ssm_c_im,
           ssm_d, ssm_log_step, ssm_w_glu, w_branch, w_out, norm_ffn_g, w_ffn_gate, w_ffn_up,
           w_ffn_down, norm_final_g):
    b_, s_, d_model = x.shape
    depth = w_in.shape[0]
    mix_w = lru_conv_w.shape[2]
    n_heads = attn_rel_bias.shape[1]
    n_g = ssm_a_re.shape[1]
    t_ = b_ * s_
    assert s_ % ATT_QBLK == 0 and s_ % S5_BLK == 0 and n_g % S5_GROUPS_PER_STEP == 0
    c_lru_x, c_lru_g, c_q, c_k, c_v, c_u, c_gate = [i * mix_w for i in range(7)]
    in_w = w_in.shape[2]

    tm = _pick_tile(t_, 1024)
    ts_lru = _pick_tile(s_, 512)
    bias = rel_bias_matrix(attn_rel_bias.reshape(depth * n_heads, N_REL))

    xf = x.reshape(t_, d_model).astype(F32)
    for l in range(depth):
        proj = norm_matmul(xf, norm_mix_g[l], w_in[l].astype(BF16), tm=tm, tn=1024)
        proj3 = proj.reshape(b_, s_, in_w)
        y_a = lru_branch(proj3, c_lru_x, c_lru_g, lru_conv_w[l], lru_conv_b[l], lru_wa[l], lru_ba[l],
                         lru_wx[l], lru_bx[l], lru_lambda[l], ts=ts_lru)
        y_b = attention_branch(proj3, c_q, c_k, c_v, bias, l, n_heads)

        w1, mm, lam = s5_prepare(ssm_a_re[l], ssm_a_im[l], ssm_b_re[l], ssm_b_im[l],
                                 ssm_c_re[l], ssm_c_im[l], ssm_log_step[l])
        u = proj3[:, :, c_u:c_u + mix_w].reshape(b_, s_ // S5_BLK, S5_BLK, n_g, SSM_GROUP)
        ublk = u.transpose(0, 1, 3, 2, 4).reshape(t_ // S5_BLK, S5_BLK * mix_w)
        d_blk = jnp.broadcast_to(ssm_d[l].astype(F32).reshape(n_g, 1, SSM_GROUP),
                                 (n_g, S5_BLK, SSM_GROUP)).reshape(1, S5_BLK * mix_w)
        yblk = s5_branch(ublk, w1, mm, lam, d_blk, batch=b_)
        y_c = yblk.reshape(b_, s_ // S5_BLK, n_g, S5_BLK, SSM_GROUP).transpose(0, 1, 3, 2, 4)
        y_c = y_c.reshape(t_, mix_w)

        merged = branch_merge(y_a.reshape(t_, mix_w), y_b.reshape(t_, mix_w), y_c,
                              w_branch[l].astype(BF16), ssm_w_glu[l].astype(BF16), proj, c_gate,
                              gate_bias[l], tm=tm, tn=512)
        xf = matmul_residual(merged, w_out[l].astype(BF16), xf, tm=tm, tn=1024)
        xf = ffn(xf, norm_ffn_g[l], w_ffn_gate[l].astype(BF16), w_ffn_up[l].astype(BF16),
                 w_ffn_down[l].astype(BF16), tm=_pick_tile(t_, 512), tf=512)
    out = final_norm(xf, norm_final_g.astype(F32), tm=_pick_tile(t_, 512))
    return out.reshape(b_, s_, d_model).astype(x.dtype)
```

```python
import functools
import math

import jax
import jax.numpy as jnp
from jax import lax
from jax.experimental import pallas as pl
from jax.experimental.pallas import tpu as pltpu

F32 = jnp.float32
BF16 = jnp.bfloat16

CHUNK = 64
LRU_BW = 64
CONV_W = 4
LRU_C = 8.0
ATT_HD = 128
ATT_LEFT_CHUNKS = 8
MAX_REL = 128
N_REL = 2 * MAX_REL + 1
SSM_GROUP = 16
SSM_P = 64
N_BRANCH = 3
NORM_EPS = 1e-6
MASK_VALUE = -1e30

LANES = 128
SUBLANES = 8
MXU_DIM = 256
VMEM_LIMIT_BYTES = 56 * 1024 * 1024

S5_BLK = 16
S5_GT = LANES // SSM_GROUP
LOG2_E = math.log2(math.e)
ATT_HEADS_PER_STEP = 4
ATT_QBLK = 4 * CHUNK
ATT_KWIN = 3 * ATT_QBLK
LRU_CW = MXU_DIM


def _cparams(sem):
    return pltpu.CompilerParams(dimension_semantics=sem, vmem_limit_bytes=VMEM_LIMIT_BYTES)


def _rmsnorm_rows(x, g):
    ms = jnp.mean(x * x, axis=-1, keepdims=True)
    return x * lax.rsqrt(ms + NORM_EPS) * g


def _norm_matmul_body(x_ref, g_ref, w_ref, o_ref, h_ref):
    @pl.when(pl.program_id(1) == 0)
    def _():
        h_ref[...] = _rmsnorm_rows(x_ref[...], g_ref[...]).astype(BF16)

    o_ref[...] = jnp.dot(h_ref[...], w_ref[...], preferred_element_type=F32).astype(o_ref.dtype)


def norm_matmul(x, g, w, layer, *, tm, tn, out_dtype=BF16):
    m, k = x.shape
    n = w.shape[2]
    return pl.pallas_call(
        _norm_matmul_body,
        grid=(m // tm, n // tn),
        in_specs=[
            pl.BlockSpec((tm, k), lambda i, j: (i, 0)),
            pl.BlockSpec((1, k), lambda i, j: (0, 0)),
            pl.BlockSpec((None, k, tn), lambda i, j: (layer, 0, j)),
        ],
        out_specs=pl.BlockSpec((tm, tn), lambda i, j: (i, j)),
        out_shape=jax.ShapeDtypeStruct((m, n), out_dtype),
        scratch_shapes=[pltpu.VMEM((tm, k), BF16)],
        compiler_params=_cparams(("parallel", "arbitrary")),
        name="norm_matmul",
    )(x, g.reshape(1, k), w)


def _softplus(x):
    return jnp.maximum(x, 0.0) + jnp.log1p(jnp.exp(-jnp.abs(x)))


def _sigmoid(x):
    return 0.5 * jnp.tanh(0.5 * x) + 0.5


def _lru_body(x_ref, gate_ref, cw_ref, cb_ref, wa_ref, wx_ref, ba_ref, bx_ref, lam_ref,
              o_ref, xbuf_ref, h_ref, *, ts):
    t = pl.program_id(2)

    @pl.when(t == 0)
    def _():
        xbuf_ref[0:SUBLANES, :] = jnp.zeros((SUBLANES, LRU_CW), F32)
        h_ref[...] = jnp.zeros_like(h_ref)

    x = x_ref[...].astype(F32)
    xbuf_ref[SUBLANES:SUBLANES + ts, :] = x
    xc = cb_ref[...] + x * cw_ref[CONV_W - 1:CONV_W, :]
    for k in range(CONV_W - 1):
        back = CONV_W - 1 - k
        xc = xc + xbuf_ref[pl.ds(SUBLANES - back, ts), :] * cw_ref[k:k + 1, :]
    xbuf_ref[0:SUBLANES, :] = x[ts - SUBLANES:ts, :]

    xcb = xc.astype(BF16)
    r = _sigmoid(jnp.dot(xcb, wa_ref[...], preferred_element_type=F32) + ba_ref[...])
    i = _sigmoid(jnp.dot(xcb, wx_ref[...], preferred_element_type=F32) + bx_ref[...])
    th = jnp.tanh((-0.5 * LRU_C) * r * _softplus(-lam_ref[...]))
    rinv = 1.0 / (1.0 - th)
    a = (1.0 + th) * rinv
    b = jnp.sqrt(-4.0 * th * rinv * rinv) * (i * xc)

    nt = ts // SUBLANES
    a = a.reshape(nt, SUBLANES, LRU_CW)
    b = b.reshape(nt, SUBLANES, LRU_CW)
    sub = lax.broadcasted_iota(jnp.int32, (nt, SUBLANES, LRU_CW), 1)
    s = 1
    while s < SUBLANES:
        keep = sub >= s
        b_sh = pltpu.roll(b, s, 1)
        a_sh = pltpu.roll(a, s, 1)
        b = b + jnp.where(keep, a * b_sh, 0.0)
        a = jnp.where(keep, a * a_sh, a)
        s *= 2
    carry = h_ref[...]
    tiles = []
    for k in range(nt):
        hk = b[k] + a[k] * carry
        tiles.append(hk)
        carry = hk[SUBLANES - 1:SUBLANES, :]
    h_ref[...] = carry
    h = jnp.concatenate(tiles, axis=0)
    o_ref[...] = (h * jax.nn.gelu(gate_ref[...].astype(F32))).astype(o_ref.dtype)


def _block_diag_tiles(w):
    n_blk, bw, _ = w.shape
    per = LRU_CW // bw
    wt = w.reshape(n_blk // per, per, bw, bw)
    eye = jnp.eye(per, dtype=w.dtype)
    tiles = jnp.einsum("cpkj,pq->cpkqj", wt, eye)
    return tiles.reshape(n_blk // per, LRU_CW, LRU_CW)


def lru_branch(proj3, x_col0, gate_col0, conv_w, conv_b, wa, ba, wx, bx, lam, *, ts):
    b_, s_, _ = proj3.shape
    width = conv_w.shape[1]
    n_c = width // LRU_CW
    xc0 = x_col0 // LRU_CW
    gc0 = gate_col0 // LRU_CW
    row = lambda v: v.reshape(1, width).astype(F32)
    vec_spec = pl.BlockSpec((1, LRU_CW), lambda b, c, t: (0, c))
    return pl.pallas_call(
        functools.partial(_lru_body, ts=ts),
        grid=(b_, n_c, s_ // ts),
        in_specs=[
            pl.BlockSpec((None, ts, LRU_CW), lambda b, c, t: (b, t, xc0 + c)),
            pl.BlockSpec((None, ts, LRU_CW), lambda b, c, t: (b, t, gc0 + c)),
            pl.BlockSpec((CONV_W, LRU_CW), lambda b, c, t: (0, c)),
            vec_spec,
            pl.BlockSpec((None, LRU_CW, LRU_CW), lambda b, c, t: (c, 0, 0)),
            pl.BlockSpec((None, LRU_CW, LRU_CW), lambda b, c, t: (c, 0, 0)),
            vec_spec, vec_spec, vec_spec,
        ],
        out_specs=pl.BlockSpec((None, ts, LRU_CW), lambda b, c, t: (b, t, c)),
        out_shape=jax.ShapeDtypeStruct((b_, s_, width), BF16),
        scratch_shapes=[pltpu.VMEM((ts + SUBLANES, LRU_CW), F32), pltpu.VMEM((1, LRU_CW), F32)],
        compiler_params=_cparams(("parallel", "parallel", "arbitrary")),
        name="rg_lru",
    )(proj3, proj3, conv_w.astype(F32), row(conv_b), _block_diag_tiles(wa).astype(BF16),
      _block_diag_tiles(wx).astype(BF16), row(ba), row(bx), row(lam))


def _rel_bias_body(tab_ref, o_ref):
    h = pl.program_id(0)
    width = 4 * ATT_QBLK
    m = lax.broadcasted_iota(jnp.int32, (1, width), 1)
    delta = jnp.where(m < ATT_KWIN, m, m - width)
    idx = jnp.clip(2 * ATT_QBLK - delta, -MAX_REL, MAX_REL) + MAX_REL

    def pick(d, acc):
        return jnp.where(idx == d, tab_ref[h, d], acc)

    gvec = lax.fori_loop(0, N_REL, pick, jnp.zeros((1, width), F32))
    full = pltpu.roll(jnp.broadcast_to(gvec, (ATT_QBLK, width)), 0, 1, stride=1, stride_axis=0)
    chunk_shift = int(math.log2(CHUNK))
    qc = lax.shift_right_logical(lax.broadcasted_iota(jnp.int32, (ATT_QBLK, ATT_KWIN), 0), chunk_shift)
    kc = lax.shift_right_logical(lax.broadcasted_iota(jnp.int32, (ATT_QBLK, ATT_KWIN), 1), chunk_shift)
    in_band = (kc >= qc) & (kc <= qc + ATT_LEFT_CHUNKS)
    o_ref[...] = jnp.where(in_band, full[:, :ATT_KWIN] * LOG2_E, MASK_VALUE)


def rel_bias_matrix(rel_bias):
    n = rel_bias.shape[0]
    return pl.pallas_call(
        _rel_bias_body,
        grid=(n,),
        in_specs=[pl.BlockSpec(memory_space=pltpu.SMEM)],
        out_specs=pl.BlockSpec((None, ATT_QBLK, ATT_KWIN), lambda h: (h, 0, 0)),
        out_shape=jax.ShapeDtypeStruct((n, ATT_QBLK, ATT_KWIN), F32),
        compiler_params=_cparams(("arbitrary",)),
        name="rel_bias",
    )(rel_bias.astype(F32))


def _attn_body(q_ref, k0_ref, k1_ref, k2_ref, v0_ref, v1_ref, v2_ref, bias_ref, o_ref):
    qb = pl.program_id(2)
    nt = (((1,), (1,)), ((), ()))
    chunk_shift = int(math.log2(CHUNK))
    kc = lax.shift_right_logical(lax.broadcasted_iota(jnp.int32, (1, ATT_KWIN), 1), chunk_shift)
    first_chunk = (qb - 2) * (ATT_QBLK // CHUNK)
    before_start = jnp.where(kc + first_chunk >= 0, 0.0, MASK_VALUE)
    for h in range(ATT_HEADS_PER_STEP):
        cols = slice(h * ATT_HD, (h + 1) * ATT_HD)
        q = q_ref[:, cols]
        s = jnp.concatenate(
            [lax.dot_general(q, k_ref[:, cols], nt, preferred_element_type=F32)
             for k_ref in (k0_ref, k1_ref, k2_ref)], axis=1)
        s = s * (ATT_HD ** -0.5 * LOG2_E) + bias_ref[h] + before_start
        m = jnp.max(s, axis=-1, keepdims=True)
        p = jnp.exp2(s - m)
        l = jnp.sum(p, axis=-1, keepdims=True)
        pb = p.astype(BF16)
        o = jnp.dot(pb[:, 0:ATT_QBLK], v0_ref[:, cols], preferred_element_type=F32)
        o = o + jnp.dot(pb[:, ATT_QBLK:2 * ATT_QBLK], v1_ref[:, cols], preferred_element_type=F32)
        o = o + jnp.dot(pb[:, 2 * ATT_QBLK:], v2_ref[:, cols], preferred_element_type=F32)
        o_ref[:, cols] = (o / l).astype(o_ref.dtype)


def attention_branch(proj3, q_col0, k_col0, v_col0, bias, layer, n_heads):
    b_, s_, _ = proj3.shape
    hp = ATT_HEADS_PER_STEP
    wblk = hp * ATT_HD
    qc0, kc0, vc0 = q_col0 // wblk, k_col0 // wblk, v_col0 // wblk

    def kv_spec(col0, back):
        return pl.BlockSpec((None, ATT_QBLK, wblk),
                            lambda h, b, i: (b, jnp.maximum(i - back, 0), col0 + h))

    return pl.pallas_call(
        _attn_body,
        grid=(n_heads // hp, b_, s_ // ATT_QBLK),
        in_specs=[
            pl.BlockSpec((None, ATT_QBLK, wblk), lambda h, b, i: (b, i, qc0 + h)),
            kv_spec(kc0, 2), kv_spec(kc0, 1), kv_spec(kc0, 0),
            kv_spec(vc0, 2), kv_spec(vc0, 1), kv_spec(vc0, 0),
            pl.BlockSpec((hp, ATT_QBLK, ATT_KWIN), lambda h, b, i: (layer * (n_heads // hp) + h, 0, 0)),
        ],
        out_specs=pl.BlockSpec((None, ATT_QBLK, wblk), lambda h, b, i: (b, i, h)),
        out_shape=jax.ShapeDtypeStruct((b_, s_, n_heads * ATT_HD), BF16),
        compiler_params=_cparams(("parallel", "parallel", "parallel")),
        name="chunk_attention",
    )(proj3, proj3, proj3, proj3, proj3, proj3, proj3, bias)


def _s5_prep_body(are_c, aim_c, ls_c, are_r, aim_r, ls_r, ctre, ctim, bta, btb, w_ref, m_ref, lam_ref):
    blk = S5_BLK
    tw = S5_GT * SSM_GROUP
    sw = S5_GT * 2 * SSM_P
    lg_grp, lg_p = int(math.log2(SSM_GROUP)), int(math.log2(SSM_P))
    group_of_state = lambda q: lax.shift_right_logical(q, lg_p) & (S5_GT - 1)

    def lam_bar(are, aim, ls):
        step = jnp.exp(ls)
        mag = jnp.exp(are * step)
        return mag * jnp.cos(aim * step), mag * jnp.sin(aim * step)

    lr_c, li_c = lam_bar(are_c[...], aim_c[...], ls_c[...])
    rowc = lax.broadcasted_iota(jnp.int32, (sw, tw), 0)
    lanec = lax.broadcasted_iota(jnp.int32, (sw, tw), 1)
    same_c = group_of_state(rowc) == lax.shift_right_logical(lanec, lg_grp)
    im_row = rowc >= sw // 2
    cre = jnp.where(same_c, ctre[...], 0.0)
    cim = jnp.where(same_c, ctim[...], 0.0)
    pr = jnp.ones((sw, tw), F32)
    pi = jnp.zeros((sw, tw), F32)
    cl = []
    for _ in range(blk + 1):
        cl.append(jnp.where(im_row, -(cre * pi + cim * pr), cre * pr - cim * pi))
        pr, pi = pr * lr_c - pi * li_c, pr * li_c + pi * lr_c
    m_ref[...] = jnp.concatenate(cl[1:], axis=1).astype(BF16)
    cl0 = jnp.concatenate(cl[:blk], axis=1)

    ar, ai = are_r[...], aim_r[...]
    lr, li = lam_bar(ar, ai, ls_r[...])
    den = ar * ar + ai * ai
    nr = lr - 1.0
    cr = (nr * ar + li * ai) / den
    ci = (li * ar - nr * ai) / den
    rowr = lax.broadcasted_iota(jnp.int32, (tw, sw), 0)
    laner = lax.broadcasted_iota(jnp.int32, (tw, sw), 1)
    same_r = lax.shift_right_logical(rowr, lg_grp) == group_of_state(laner)
    ba = jnp.where(same_r, bta[...], 0.0)
    bb = jnp.where(same_r, btb[...], 0.0)
    bb2 = cr * ba + ci * bb
    bb2s = cr * bb - ci * ba

    kt = jnp.dot(bb2, cl0, preferred_element_type=F32, precision=lax.Precision.HIGHEST).astype(BF16)
    pr = jnp.ones((1, sw), F32)
    pi = jnp.zeros((1, sw), F32)
    for i in range(blk - 1, -1, -1):
        rows = slice(i * tw, (i + 1) * tw)
        if i:
            w_ref[rows, 0:i * tw] = jnp.zeros((tw, i * tw), BF16)
        w_ref[rows, i * tw:blk * tw] = kt[:, 0:(blk - i) * tw]
        w_ref[rows, blk * tw:] = (pr * bb2 + pi * bb2s).astype(BF16)
        pr, pi = pr * lr - pi * li, pr * li + pi * lr
    sign = jnp.where(lax.broadcasted_iota(jnp.int32, (1, sw), 1) >= sw // 2, 1.0, -1.0)
    lam_ref[...] = jnp.zeros_like(lam_ref)
    lam_ref[0:1, :] = pr
    lam_ref[1:2, :] = sign * pi


def s5_prepare(a_re, a_im, b_re, b_im, c_re, c_im, log_step):
    n_g, npp = a_re.shape
    grp = b_re.shape[2]
    blk, gt = S5_BLK, S5_GT
    n_t = n_g // gt
    tw, sw = gt * grp, gt * 2 * npp
    f = lambda v: v.astype(F32)
    pair = lambda v: jnp.concatenate([f(v).reshape(n_t, gt * npp)] * 2, axis=-1)
    as_col = lambda v: jnp.broadcast_to(v.reshape(n_t, sw, 1), (n_t, sw, tw))
    as_row = lambda v: v.reshape(n_t, 1, sw)
    ls2 = pair(jnp.broadcast_to(f(log_step)[:, None], (n_g, npp)))
    c_t = lambda v: jnp.tile(f(v).transpose(0, 2, 1).reshape(n_t, gt * npp, grp), (1, 1, gt))
    ct = lambda v: jnp.concatenate([c_t(v), c_t(v)], axis=1)
    b_t = lambda v: jnp.broadcast_to(f(v).transpose(0, 2, 1).reshape(n_t, gt, grp, 1, npp),
                                     (n_t, gt, grp, gt, npp)).reshape(n_t, tw, gt * npp)
    btr, bti = b_t(b_re), b_t(b_im)
    bt = lambda lo, hi: jnp.concatenate([lo, hi], axis=-1)
    t3 = lambda shp: pl.BlockSpec((None,) + shp, lambda t: (t, 0, 0))
    return pl.pallas_call(
        _s5_prep_body,
        grid=(n_t,),
        in_specs=[t3((sw, tw)), t3((sw, tw)), t3((sw, tw)), t3((1, sw)), t3((1, sw)), t3((1, sw)),
                  t3((sw, tw)), t3((sw, tw)), t3((tw, sw)), t3((tw, sw))],
        out_specs=[t3((blk * tw, blk * tw + sw)), t3((sw, blk * tw)), t3((SUBLANES, sw))],
        out_shape=[jax.ShapeDtypeStruct((n_t, blk * tw, blk * tw + sw), BF16),
                   jax.ShapeDtypeStruct((n_t, sw, blk * tw), BF16),
                   jax.ShapeDtypeStruct((n_t, SUBLANES, sw), F32)],
        compiler_params=_cparams(("parallel",)),
        name="s5_prepare",
    )(as_col(pair(a_re)), as_col(pair(a_im)), as_col(ls2), as_row(pair(a_re)), as_row(pair(a_im)),
      as_row(ls2), ct(c_re), ct(c_im), bt(btr, bti), bt(-bti, btr))


def _s5_body(*refs, rows):
    blk = S5_BLK
    tw = S5_GT * SSM_GROUP
    u_refs = refs[:blk]
    w_ref, m_ref, lam_ref, d_ref, o_ref = refs[blk:]
    sw = lam_ref.shape[1]
    lhs = jnp.concatenate([u[...] for u in u_refs], axis=1)
    r = jnp.dot(lhs, w_ref[...], preferred_element_type=F32)
    z = r[:, blk * tw:]
    row = lax.broadcasted_iota(jnp.int32, (rows, sw), 0)
    ca = lam_ref[0:1, :]
    cb = lam_ref[1:2, :]
    s = 1
    while s < rows:
        zs = jnp.where(row >= s, pltpu.roll(z, s, 0), 0.0)
        z = z + ca * zs + cb * pltpu.roll(zs, sw // 2, 1)
        ca, cb = ca * ca - cb * cb, 2.0 * ca * cb
        s *= 2
    zprev = jnp.where(row >= 1, pltpu.roll(z, 1, 0), 0.0)
    ys = jnp.dot(zprev.astype(BF16), m_ref[...], preferred_element_type=F32)
    d = d_ref[...]
    for j in range(blk):
        cols = slice(j * tw, (j + 1) * tw)
        y = r[:, cols] + ys[:, cols] + d * u_refs[j][...].astype(F32)
        o_ref[:, cols] = jax.nn.gelu(y).astype(o_ref.dtype)


def s5_branch(proj_blk, u_col0, w, mm, lam, d, *, batch):
    n_rows = proj_blk.shape[0]
    rows = n_rows // batch
    blk = S5_BLK
    in_w = proj_blk.shape[1] // blk
    n_t, sw = w.shape[0], lam.shape[2]
    tw = S5_GT * SSM_GROUP
    u_spec = lambda i: pl.BlockSpec((rows, tw), lambda c, b: (b, (i * in_w + u_col0) // tw + c))
    return pl.pallas_call(
        functools.partial(_s5_body, rows=rows),
        grid=(n_t, batch),
        in_specs=[u_spec(i) for i in range(blk)] + [
            pl.BlockSpec((None,) + w.shape[1:], lambda c, b: (c, 0, 0)),
            pl.BlockSpec((None,) + mm.shape[1:], lambda c, b: (c, 0, 0)),
            pl.BlockSpec((None, SUBLANES, sw), lambda c, b: (c, 0, 0)),
            pl.BlockSpec((1, tw), lambda c, b: (0, c)),
        ],
        out_specs=pl.BlockSpec((None, rows, blk * tw), lambda c, b: (c, b, 0)),
        out_shape=jax.ShapeDtypeStruct((n_t, n_rows, blk * tw), BF16),
        compiler_params=_cparams(("parallel", "arbitrary")),
        name="s5_blocked",
    )(*([proj_blk] * blk), w, mm, lam, d)


def _merge_body(ya_ref, yb_ref, yc_ref, w0_ref, w1_ref, w2_ref, wg_ref,
                g0_ref, g1_ref, g2_ref, gb_ref, o_ref):
    dot = lambda a, w: jnp.dot(a[...], w[...], preferred_element_type=F32)
    br_a = dot(ya_ref, w0_ref)
    br_b = dot(yb_ref, w1_ref)
    yc = yc_ref[...]
    br_c = jnp.dot(yc, w2_ref[...], preferred_element_type=F32) * _sigmoid(
        jnp.dot(yc, wg_ref[...], preferred_element_type=F32))
    gate = lambda g_ref, i: _sigmoid(g_ref[...].astype(F32) + gb_ref[i:i + 1, :])
    merged = gate(g0_ref, 0) * br_a + gate(g1_ref, 1) * br_b + gate(g2_ref, 2) * br_c
    o_ref[...] = merged.astype(o_ref.dtype)


def branch_merge(ya, yb, yc, w_branch, w_glu, layer, proj, gate_col0, gate_bias, *, tm, tn):
    m, kdim = ya.shape
    n = w_glu.shape[2]
    gc0 = gate_col0 // tn
    nj = n // tn
    y_spec = pl.BlockSpec((tm, kdim), lambda i, j: (i, 0))
    wb_spec = lambda br: pl.BlockSpec((None, None, kdim, tn), lambda i, j: (layer, br, 0, j))
    g_spec = lambda br: pl.BlockSpec((tm, tn), lambda i, j: (i, gc0 + br * nj + j))
    return pl.pallas_call(
        _merge_body,
        grid=(m // tm, nj),
        in_specs=[y_spec, y_spec, y_spec, wb_spec(0), wb_spec(1), wb_spec(2),
                  pl.BlockSpec((None, kdim, tn), lambda i, j: (layer, 0, j)),
                  g_spec(0), g_spec(1), g_spec(2),
                  pl.BlockSpec((N_BRANCH, tn), lambda i, j: (0, j))],
        out_specs=pl.BlockSpec((tm, tn), lambda i, j: (i, j)),
        out_shape=jax.ShapeDtypeStruct((m, n), BF16),
        compiler_params=_cparams(("parallel", "arbitrary")),
        name="branch_merge",
    )(ya, yb, yc, w_branch, w_branch, w_branch, w_glu, proj, proj, proj, gate_bias.astype(F32))


def _matmul_res_body(a_ref, w_ref, r_ref, o_ref):
    o_ref[...] = r_ref[...] + jnp.dot(a_ref[...], w_ref[...], preferred_element_type=F32)


def matmul_residual(a, w, layer, res, *, tm, tn):
    m, k = a.shape
    n = w.shape[2]
    return pl.pallas_call(
        _matmul_res_body,
        grid=(m // tm, n // tn),
        in_specs=[pl.BlockSpec((tm, k), lambda i, j: (i, 0)),
                  pl.BlockSpec((None, k, tn), lambda i, j: (layer, 0, j)),
                  pl.BlockSpec((tm, tn), lambda i, j: (i, j))],
        out_specs=pl.BlockSpec((tm, tn), lambda i, j: (i, j)),
        out_shape=jax.ShapeDtypeStruct((m, n), F32),
        compiler_params=_cparams(("parallel", "arbitrary")),
        name="matmul_residual",
    )(a, w, res)


def _ffn_body(x_ref, g_ref, wg_ref, wu_ref, wd_ref, o_ref, h_ref):
    f = pl.program_id(1)

    @pl.when(f == 0)
    def _():
        x = x_ref[...]
        h_ref[...] = _rmsnorm_rows(x, g_ref[...]).astype(BF16)
        o_ref[...] = x

    h = h_ref[...]
    gate = jnp.dot(h, wg_ref[...], preferred_element_type=F32)
    up = jnp.dot(h, wu_ref[...], preferred_element_type=F32)
    act = (gate * _sigmoid(gate) * up).astype(BF16)
    o_ref[...] += jnp.dot(act, wd_ref[...], preferred_element_type=F32)


def ffn(x, g, w_gate, w_up, w_down, layer, *, tm, tf):
    m, d = x.shape
    hidden = w_gate.shape[2]
    return pl.pallas_call(
        _ffn_body,
        grid=(m // tm, hidden // tf),
        in_specs=[pl.BlockSpec((tm, d), lambda i, f: (i, 0)),
                  pl.BlockSpec((1, d), lambda i, f: (0, 0)),
                  pl.BlockSpec((None, d, tf), lambda i, f: (layer, 0, f)),
                  pl.BlockSpec((None, d, tf), lambda i, f: (layer, 0, f)),
                  pl.BlockSpec((None, tf, d), lambda i, f: (layer, f, 0))],
        out_specs=pl.BlockSpec((tm, d), lambda i, f: (i, 0)),
        out_shape=jax.ShapeDtypeStruct((m, d), F32),
        scratch_shapes=[pltpu.VMEM((tm, d), BF16)],
        compiler_params=_cparams(("parallel", "arbitrary")),
        name="swiglu_ffn",
    )(x, g.reshape(1, d), w_gate, w_up, w_down)


def _final_norm_body(x_ref, g_ref, o_ref):
    o_ref[...] = _rmsnorm_rows(x_ref[...], g_ref[...])


def final_norm(x, g, *, tm):
    m, d = x.shape
    return pl.pallas_call(
        _final_norm_body,
        grid=(m // tm,),
        in_specs=[pl.BlockSpec((tm, d), lambda i: (i, 0)), pl.BlockSpec((1, d), lambda i: (0, 0))],
        out_specs=pl.BlockSpec((tm, d), lambda i: (i, 0)),
        out_shape=jax.ShapeDtypeStruct((m, d), F32),
        compiler_params=_cparams(("parallel",)),
        name="final_rmsnorm",
    )(x, g.reshape(1, d))


def _pick_tile(n, want):
    t = min(n, want)
    while n % t:
        t //= 2
    return t


def kernel(x, norm_mix_g, w_in, gate_bias, lru_conv_w, lru_conv_b, lru_wa, lru_ba, lru_wx, lru_bx,
           lru_lambda, attn_rel_bias, ssm_a_re, ssm_a_im, ssm_b_re, ssm_b_im, ssm_c_re, ssm_c_im,
           ssm_d, ssm_log_step, ssm_w_glu, w_branch, w_out, norm_ffn_g, w_ffn_gate, w_ffn_up,
           w_ffn_down, norm_final_g):
    b_, s_, d_model = x.shape
    depth = w_in.shape[0]
    mix_w = lru_conv_w.shape[2]
    n_heads = attn_rel_bias.shape[1]
    n_g = ssm_a_re.shape[1]
    t_ = b_ * s_
    assert s_ % ATT_QBLK == 0 and s_ % S5_BLK == 0 and n_g % S5_GT == 0
    c_lru_x, c_lru_g, c_q, c_k, c_v, c_u, c_gate = [i * mix_w for i in range(7)]
    in_w = w_in.shape[2]

    tm = _pick_tile(t_, 1024)
    ts_lru = _pick_tile(s_, 512)
    bias = rel_bias_matrix(attn_rel_bias.reshape(depth * n_heads, N_REL))
    w_in, w_branch, ssm_w_glu, w_out, w_ffn_gate, w_ffn_up, w_ffn_down = [
        w.astype(BF16) for w in (w_in, w_branch, ssm_w_glu, w_out, w_ffn_gate, w_ffn_up, w_ffn_down)]

    xf = x.reshape(t_, d_model).astype(F32)
    for l in range(depth):
        proj = norm_matmul(xf, norm_mix_g[l], w_in, l, tm=tm, tn=1024)
        proj3 = proj.reshape(b_, s_, in_w)
        y_a = lru_branch(proj3, c_lru_x, c_lru_g, lru_conv_w[l], lru_conv_b[l], lru_wa[l], lru_ba[l],
                         lru_wx[l], lru_bx[l], lru_lambda[l], ts=ts_lru)
        y_b = attention_branch(proj3, c_q, c_k, c_v, bias, l, n_heads)

        w1, mm, lam = s5_prepare(ssm_a_re[l], ssm_a_im[l], ssm_b_re[l], ssm_b_im[l],
                                 ssm_c_re[l], ssm_c_im[l], ssm_log_step[l])
        yblk = s5_branch(proj.reshape(t_ // S5_BLK, S5_BLK * in_w), c_u, w1, mm, lam,
                         ssm_d[l].astype(F32).reshape(1, mix_w), batch=b_)
        y_c = yblk.reshape(n_g // S5_GT, t_ // S5_BLK, S5_BLK, LANES).transpose(1, 2, 0, 3)
        y_c = y_c.reshape(t_, mix_w)

        merged = branch_merge(y_a.reshape(t_, mix_w), y_b.reshape(t_, mix_w), y_c,
                              w_branch, ssm_w_glu, l, proj, c_gate, gate_bias[l], tm=tm, tn=512)
        xf = matmul_residual(merged, w_out, l, xf, tm=tm, tn=1024)
        xf = ffn(xf, norm_ffn_g[l], w_ffn_gate, w_ffn_up, w_ffn_down, l, tm=_pick_tile(t_, 512), tf=512)
    out = final_norm(xf, norm_final_g.astype(F32), tm=_pick_tile(t_, 512))
    return out.reshape(b_, s_, d_model).astype(x.dtype)
```

```python
import functools
import math

import jax
import jax.numpy as jnp
from jax import lax
from jax.experimental import pallas as pl
from jax.experimental.pallas import tpu as pltpu

F32 = jnp.float32
BF16 = jnp.bfloat16

CHUNK = 64
LRU_BW = 64
CONV_W = 4
LRU_C = 8.0
ATT_HD = 128
ATT_LEFT_CHUNKS = 8
MAX_REL = 128
N_REL = 2 * MAX_REL + 1
SSM_GROUP = 16
SSM_P = 64
N_BRANCH = 3
NORM_EPS = 1e-6
MASK_VALUE = -1e30

LANES = 128
SUBLANES = 8
MXU_DIM = 256
VMEM_LIMIT_BYTES = 56 * 1024 * 1024

S5_BLK = 16
S5_GT = LANES // SSM_GROUP
LOG2_E = math.log2(math.e)
ATT_HEADS_PER_STEP = 4
ATT_QBLK = 4 * CHUNK
ATT_KWIN = 3 * ATT_QBLK
LRU_CW = MXU_DIM


def _cparams(sem):
    return pltpu.CompilerParams(dimension_semantics=sem, vmem_limit_bytes=VMEM_LIMIT_BYTES)


def _rmsnorm_rows(x, g):
    ms = jnp.mean(x * x, axis=-1, keepdims=True)
    return x * lax.rsqrt(ms + NORM_EPS) * g


def _norm_matmul_body(x_ref, g_ref, w_ref, o_ref, h_ref):
    @pl.when(pl.program_id(1) == 0)
    def _():
        h_ref[...] = _rmsnorm_rows(x_ref[...], g_ref[...]).astype(BF16)

    o_ref[...] = jnp.dot(h_ref[...], w_ref[...], preferred_element_type=F32).astype(o_ref.dtype)


def norm_matmul(x, g, w, layer, *, tm, tn, out_dtype=BF16):
    m, k = x.shape
    n = w.shape[2]
    return pl.pallas_call(
        _norm_matmul_body,
        grid=(m // tm, n // tn),
        in_specs=[
            pl.BlockSpec((tm, k), lambda i, j: (i, 0)),
            pl.BlockSpec((1, k), lambda i, j: (0, 0)),
            pl.BlockSpec((None, k, tn), lambda i, j: (layer, 0, j)),
        ],
        out_specs=pl.BlockSpec((tm, tn), lambda i, j: (i, j)),
        out_shape=jax.ShapeDtypeStruct((m, n), out_dtype),
        scratch_shapes=[pltpu.VMEM((tm, k), BF16)],
        compiler_params=_cparams(("parallel", "arbitrary")),
        name="norm_matmul",
    )(x, g.reshape(1, k), w)


def _softplus(x):
    return jnp.maximum(x, 0.0) + jnp.log1p(jnp.exp(-jnp.abs(x)))


def _sigmoid(x):
    return 0.5 * jnp.tanh(0.5 * x) + 0.5


def _lru_body(x_ref, gate_ref, cw_ref, cb_ref, wa_ref, wx_ref, ba_ref, bx_ref, lam_ref,
              o_ref, xbuf_ref, h_ref, *, ts):
    t = pl.program_id(2)

    @pl.when(t == 0)
    def _():
        xbuf_ref[0:SUBLANES, :] = jnp.zeros((SUBLANES, LRU_CW), F32)
        h_ref[...] = jnp.zeros_like(h_ref)

    x = x_ref[...].astype(F32)
    xbuf_ref[SUBLANES:SUBLANES + ts, :] = x
    xc = cb_ref[...] + x * cw_ref[CONV_W - 1:CONV_W, :]
    for k in range(CONV_W - 1):
        back = CONV_W - 1 - k
        xc = xc + xbuf_ref[pl.ds(SUBLANES - back, ts), :] * cw_ref[k:k + 1, :]
    xbuf_ref[0:SUBLANES, :] = x[ts - SUBLANES:ts, :]

    xcb = xc.astype(BF16)
    r = _sigmoid(jnp.dot(xcb, wa_ref[...], preferred_element_type=F32) + ba_ref[...])
    i = _sigmoid(jnp.dot(xcb, wx_ref[...], preferred_element_type=F32) + bx_ref[...])
    th = jnp.tanh((-0.5 * LRU_C) * r * _softplus(-lam_ref[...]))
    rinv = 1.0 / (1.0 - th)
    a = (1.0 + th) * rinv
    b = jnp.sqrt(-4.0 * th * rinv * rinv) * (i * xc)

    nt = ts // SUBLANES
    a = a.reshape(nt, SUBLANES, LRU_CW)
    b = b.reshape(nt, SUBLANES, LRU_CW)
    sub = lax.broadcasted_iota(jnp.int32, (nt, SUBLANES, LRU_CW), 1)
    s = 1
    while s < SUBLANES:
        keep = sub >= s
        b_sh = pltpu.roll(b, s, 1)
        a_sh = pltpu.roll(a, s, 1)
        b = b + jnp.where(keep, a * b_sh, 0.0)
        a = jnp.where(keep, a * a_sh, a)
        s *= 2
    carry = h_ref[...]
    tiles = []
    for k in range(nt):
        hk = b[k] + a[k] * carry
        tiles.append(hk)
        carry = hk[SUBLANES - 1:SUBLANES, :]
    h_ref[...] = carry
    h = jnp.concatenate(tiles, axis=0)
    o_ref[...] = (h * jax.nn.gelu(gate_ref[...].astype(F32))).astype(o_ref.dtype)


def _block_diag_tiles(w):
    n_blk, bw, _ = w.shape
    per = LRU_CW // bw
    wt = w.reshape(n_blk // per, per, bw, bw)
    eye = jnp.eye(per, dtype=w.dtype)
    tiles = jnp.einsum("cpkj,pq->cpkqj", wt, eye)
    return tiles.reshape(n_blk // per, LRU_CW, LRU_CW)


def lru_branch(proj3, x_col0, gate_col0, conv_w, conv_b, wa, ba, wx, bx, lam, *, ts):
    b_, s_, _ = proj3.shape
    width = conv_w.shape[1]
    n_c = width // LRU_CW
    xc0 = x_col0 // LRU_CW
    gc0 = gate_col0 // LRU_CW
    row = lambda v: v.reshape(1, width).astype(F32)
    vec_spec = pl.BlockSpec((1, LRU_CW), lambda b, c, t: (0, c))
    return pl.pallas_call(
        functools.partial(_lru_body, ts=ts),
        grid=(b_, n_c, s_ // ts),
        in_specs=[
            pl.BlockSpec((None, ts, LRU_CW), lambda b, c, t: (b, t, xc0 + c)),
            pl.BlockSpec((None, ts, LRU_CW), lambda b, c, t: (b, t, gc0 + c)),
            pl.BlockSpec((CONV_W, LRU_CW), lambda b, c, t: (0, c)),
            vec_spec,
            pl.BlockSpec((None, LRU_CW, LRU_CW), lambda b, c, t: (c, 0, 0)),
            pl.BlockSpec((None, LRU_CW, LRU_CW), lambda b, c, t: (c, 0, 0)),
            vec_spec, vec_spec, vec_spec,
        ],
        out_specs=pl.BlockSpec((None, ts, LRU_CW), lambda b, c, t: (b, t, c)),
        out_shape=jax.ShapeDtypeStruct((b_, s_, width), BF16),
        scratch_shapes=[pltpu.VMEM((ts + SUBLANES, LRU_CW), F32), pltpu.VMEM((1, LRU_CW), F32)],
        compiler_params=_cparams(("parallel", "parallel", "arbitrary")),
        name="rg_lru",
    )(proj3, proj3, conv_w.astype(F32), row(conv_b), _block_diag_tiles(wa).astype(BF16),
      _block_diag_tiles(wx).astype(BF16), row(ba), row(bx), row(lam))


def _rel_bias_body(tab_ref, o_ref):
    h = pl.program_id(0)
    width = 4 * ATT_QBLK
    m = lax.broadcasted_iota(jnp.int32, (1, width), 1)
    delta = jnp.where(m < ATT_KWIN, m, m - width)
    idx = jnp.clip(2 * ATT_QBLK - delta, -MAX_REL, MAX_REL) + MAX_REL

    def pick(d, acc):
        return jnp.where(idx == d, tab_ref[h, d], acc)

    gvec = lax.fori_loop(0, N_REL, pick, jnp.zeros((1, width), F32))
    full = pltpu.roll(jnp.broadcast_to(gvec, (ATT_QBLK, width)), 0, 1, stride=1, stride_axis=0)
    chunk_shift = int(math.log2(CHUNK))
    qc = lax.shift_right_logical(lax.broadcasted_iota(jnp.int32, (ATT_QBLK, ATT_KWIN), 0), chunk_shift)
    kc = lax.shift_right_logical(lax.broadcasted_iota(jnp.int32, (ATT_QBLK, ATT_KWIN), 1), chunk_shift)
    in_band = (kc >= qc) & (kc <= qc + ATT_LEFT_CHUNKS)
    o_ref[...] = jnp.where(in_band, full[:, :ATT_KWIN] * LOG2_E, MASK_VALUE)


def rel_bias_matrix(rel_bias):
    n = rel_bias.shape[0]
    return pl.pallas_call(
        _rel_bias_body,
        grid=(n,),
        in_specs=[pl.BlockSpec(memory_space=pltpu.SMEM)],
        out_specs=pl.BlockSpec((None, ATT_QBLK, ATT_KWIN), lambda h: (h, 0, 0)),
        out_shape=jax.ShapeDtypeStruct((n, ATT_QBLK, ATT_KWIN), F32),
        compiler_params=_cparams(("arbitrary",)),
        name="rel_bias",
    )(rel_bias.astype(F32))


def _attn_body(q_ref, k0_ref, k1_ref, k2_ref, v0_ref, v1_ref, v2_ref, bias_ref, o_ref):
    qb = pl.program_id(2)
    nt = (((1,), (1,)), ((), ()))
    chunk_shift = int(math.log2(CHUNK))
    kc = lax.shift_right_logical(lax.broadcasted_iota(jnp.int32, (1, ATT_KWIN), 1), chunk_shift)
    first_chunk = (qb - 2) * (ATT_QBLK // CHUNK)
    before_start = jnp.where(kc + first_chunk >= 0, 0.0, MASK_VALUE)
    for h in range(ATT_HEADS_PER_STEP):
        cols = slice(h * ATT_HD, (h + 1) * ATT_HD)
        q = q_ref[:, cols]
        s = jnp.concatenate(
            [lax.dot_general(q, k_ref[:, cols], nt, preferred_element_type=F32)
             for k_ref in (k0_ref, k1_ref, k2_ref)], axis=1)
        s = s * (ATT_HD ** -0.5 * LOG2_E) + bias_ref[h] + before_start
        m = jnp.max(s, axis=-1, keepdims=True)
        p = jnp.exp2(s - m)
        l = jnp.sum(p, axis=-1, keepdims=True)
        pb = p.astype(BF16)
        o = jnp.dot(pb[:, 0:ATT_QBLK], v0_ref[:, cols], preferred_element_type=F32)
        o = o + jnp.dot(pb[:, ATT_QBLK:2 * ATT_QBLK], v1_ref[:, cols], preferred_element_type=F32)
        o = o + jnp.dot(pb[:, 2 * ATT_QBLK:], v2_ref[:, cols], preferred_element_type=F32)
        o_ref[:, cols] = (o / l).astype(o_ref.dtype)


def attention_branch(proj3, q_col0, k_col0, v_col0, bias, layer, n_heads):
    b_, s_, _ = proj3.shape
    hp = ATT_HEADS_PER_STEP
    wblk = hp * ATT_HD
    qc0, kc0, vc0 = q_col0 // wblk, k_col0 // wblk, v_col0 // wblk

    def kv_spec(col0, back):
        return pl.BlockSpec((None, ATT_QBLK, wblk),
                            lambda h, b, i: (b, jnp.maximum(i - back, 0), col0 + h))

    return pl.pallas_call(
        _attn_body,
        grid=(n_heads // hp, b_, s_ // ATT_QBLK),
        in_specs=[
            pl.BlockSpec((None, ATT_QBLK, wblk), lambda h, b, i: (b, i, qc0 + h)),
            kv_spec(kc0, 2), kv_spec(kc0, 1), kv_spec(kc0, 0),
            kv_spec(vc0, 2), kv_spec(vc0, 1), kv_spec(vc0, 0),
            pl.BlockSpec((hp, ATT_QBLK, ATT_KWIN), lambda h, b, i: (layer * (n_heads // hp) + h, 0, 0)),
        ],
        out_specs=pl.BlockSpec((None, ATT_QBLK, wblk), lambda h, b, i: (b, i, h)),
        out_shape=jax.ShapeDtypeStruct((b_, s_, n_heads * ATT_HD), BF16),
        compiler_params=_cparams(("parallel", "parallel", "parallel")),
        name="chunk_attention",
    )(proj3, proj3, proj3, proj3, proj3, proj3, proj3, bias)


def _s5_prep_body(are_c, aim_c, ls_c, are_r, aim_r, ls_r, ctre, ctim, bta, btb, w_ref, m_ref, lam_ref):
    blk = S5_BLK
    tw = S5_GT * SSM_GROUP
    sw = S5_GT * 2 * SSM_P
    lg_grp, lg_p = int(math.log2(SSM_GROUP)), int(math.log2(SSM_P))
    group_of_state = lambda q: lax.shift_right_logical(q, lg_p) & (S5_GT - 1)

    def lam_bar(are, aim, ls):
        step = jnp.exp(ls)
        mag = jnp.exp(are * step)
        return mag * jnp.cos(aim * step), mag * jnp.sin(aim * step)

    lr_c, li_c = lam_bar(are_c[...], aim_c[...], ls_c[...])
    rowc = lax.broadcasted_iota(jnp.int32, (sw, tw), 0)
    lanec = lax.broadcasted_iota(jnp.int32, (sw, tw), 1)
    same_c = group_of_state(rowc) == lax.shift_right_logical(lanec, lg_grp)
    im_row = rowc >= sw // 2
    cre = jnp.where(same_c, ctre[...], 0.0)
    cim = jnp.where(same_c, ctim[...], 0.0)
    pr = jnp.ones((sw, tw), F32)
    pi = jnp.zeros((sw, tw), F32)
    cl = []
    for _ in range(blk + 1):
        cl.append(jnp.where(im_row, -(cre * pi + cim * pr), cre * pr - cim * pi))
        pr, pi = pr * lr_c - pi * li_c, pr * li_c + pi * lr_c
    m_ref[...] = jnp.concatenate(cl[1:], axis=1).astype(BF16)
    cl0 = jnp.concatenate(cl[:blk], axis=1)

    ar, ai = are_r[...], aim_r[...]
    lr, li = lam_bar(ar, ai, ls_r[...])
    den = ar * ar + ai * ai
    nr = lr - 1.0
    cr = (nr * ar + li * ai) / den
    ci = (li * ar - nr * ai) / den
    rowr = lax.broadcasted_iota(jnp.int32, (tw, sw), 0)
    laner = lax.broadcasted_iota(jnp.int32, (tw, sw), 1)
    same_r = lax.shift_right_logical(rowr, lg_grp) == group_of_state(laner)
    ba = jnp.where(same_r, bta[...], 0.0)
    bb = jnp.where(same_r, btb[...], 0.0)
    bb2 = cr * ba + ci * bb
    bb2s = cr * bb - ci * ba

    kt = jnp.dot(bb2, cl0, preferred_element_type=F32, precision=lax.Precision.HIGHEST).astype(BF16)
    pr = jnp.ones((1, sw), F32)
    pi = jnp.zeros((1, sw), F32)
    for i in range(blk - 1, -1, -1):
        rows = slice(i * tw, (i + 1) * tw)
        if i:
            w_ref[rows, 0:i * tw] = jnp.zeros((tw, i * tw), BF16)
        w_ref[rows, i * tw:blk * tw] = kt[:, 0:(blk - i) * tw]
        w_ref[rows, blk * tw:] = (pr * bb2 + pi * bb2s).astype(BF16)
        pr, pi = pr * lr - pi * li, pr * li + pi * lr
    sign = jnp.where(lax.broadcasted_iota(jnp.int32, (1, sw), 1) >= sw // 2, 1.0, -1.0)
    lam_ref[...] = jnp.zeros_like(lam_ref)
    lam_ref[0:1, :] = pr
    lam_ref[1:2, :] = sign * pi


def s5_prepare(a_re, a_im, b_re, b_im, c_re, c_im, log_step):
    n_g, npp = a_re.shape
    grp = b_re.shape[2]
    blk, gt = S5_BLK, S5_GT
    n_t = n_g // gt
    tw, sw = gt * grp, gt * 2 * npp
    f = lambda v: v.astype(F32)
    pair = lambda v: jnp.concatenate([f(v).reshape(n_t, gt * npp)] * 2, axis=-1)
    as_col = lambda v: jnp.broadcast_to(v.reshape(n_t, sw, 1), (n_t, sw, tw))
    as_row = lambda v: v.reshape(n_t, 1, sw)
    ls2 = pair(jnp.broadcast_to(f(log_step)[:, None], (n_g, npp)))
    c_t = lambda v: jnp.tile(f(v).transpose(0, 2, 1).reshape(n_t, gt * npp, grp), (1, 1, gt))
    ct = lambda v: jnp.concatenate([c_t(v), c_t(v)], axis=1)
    b_t = lambda v: jnp.broadcast_to(f(v).transpose(0, 2, 1).reshape(n_t, gt, grp, 1, npp),
                                     (n_t, gt, grp, gt, npp)).reshape(n_t, tw, gt * npp)
    btr, bti = b_t(b_re), b_t(b_im)
    bt = lambda lo, hi: jnp.concatenate([lo, hi], axis=-1)
    t3 = lambda shp: pl.BlockSpec((None,) + shp, lambda t: (t, 0, 0))
    return pl.pallas_call(
        _s5_prep_body,
        grid=(n_t,),
        in_specs=[t3((sw, tw)), t3((sw, tw)), t3((sw, tw)), t3((1, sw)), t3((1, sw)), t3((1, sw)),
                  t3((sw, tw)), t3((sw, tw)), t3((tw, sw)), t3((tw, sw))],
        out_specs=[t3((blk * tw, blk * tw + sw)), t3((sw, blk * tw)), t3((SUBLANES, sw))],
        out_shape=[jax.ShapeDtypeStruct((n_t, blk * tw, blk * tw + sw), BF16),
                   jax.ShapeDtypeStruct((n_t, sw, blk * tw), BF16),
                   jax.ShapeDtypeStruct((n_t, SUBLANES, sw), F32)],
        compiler_params=_cparams(("parallel",)),
        name="s5_prepare",
    )(as_col(pair(a_re)), as_col(pair(a_im)), as_col(ls2), as_row(pair(a_re)), as_row(pair(a_im)),
      as_row(ls2), ct(c_re), ct(c_im), bt(btr, bti), bt(-bti, btr))


def _s5_body(*refs, rows):
    blk = S5_BLK
    tw = S5_GT * SSM_GROUP
    u_refs = refs[:blk]
    w_ref, m_ref, lam_ref, d_ref, o_ref = refs[blk:]
    sw = lam_ref.shape[1]
    lhs = jnp.concatenate([u[...] for u in u_refs], axis=1)
    r = jnp.dot(lhs, w_ref[...], preferred_element_type=F32)
    z = r[:, blk * tw:]
    row = lax.broadcasted_iota(jnp.int32, (rows, sw), 0)
    ca = lam_ref[0:1, :]
    cb = lam_ref[1:2, :]
    s = 1
    while s < rows:
        zs = jnp.where(row >= s, pltpu.roll(z, s, 0), 0.0)
        z = z + ca * zs + cb * pltpu.roll(zs, sw // 2, 1)
        ca, cb = ca * ca - cb * cb, 2.0 * ca * cb
        s *= 2
    zprev = jnp.where(row >= 1, pltpu.roll(z, 1, 0), 0.0)
    ys = jnp.dot(zprev.astype(BF16), m_ref[...], preferred_element_type=F32)
    d = d_ref[...]
    for j in range(blk):
        cols = slice(j * tw, (j + 1) * tw)
        y = r[:, cols] + ys[:, cols] + d * u_refs[j][...].astype(F32)
        o_ref[j] = jax.nn.gelu(y).astype(o_ref.dtype)


def s5_branch(proj_blk, u_col0, w, mm, lam, d, *, batch):
    n_rows = proj_blk.shape[0]
    rows = n_rows // batch
    blk = S5_BLK
    in_w = proj_blk.shape[1] // blk
    n_t, sw = w.shape[0], lam.shape[2]
    tw = S5_GT * SSM_GROUP
    u_spec = lambda i: pl.BlockSpec((rows, tw), lambda c, b: (b, (i * in_w + u_col0) // tw + c))
    return pl.pallas_call(
        functools.partial(_s5_body, rows=rows),
        grid=(n_t, batch),
        in_specs=[u_spec(i) for i in range(blk)] + [
            pl.BlockSpec((None,) + w.shape[1:], lambda c, b: (c, 0, 0)),
            pl.BlockSpec((None,) + mm.shape[1:], lambda c, b: (c, 0, 0)),
            pl.BlockSpec((None, SUBLANES, sw), lambda c, b: (c, 0, 0)),
            pl.BlockSpec((1, tw), lambda c, b: (0, c)),
        ],
        out_specs=pl.BlockSpec((blk, rows, tw), lambda c, b: (0, b, c)),
        out_shape=jax.ShapeDtypeStruct((blk, n_rows, n_t * tw), BF16),
        compiler_params=_cparams(("parallel", "arbitrary")),
        name="s5_blocked",
    )(*([proj_blk] * blk), w, mm, lam, d)


def _merge_body(ya_ref, yb_ref, yc_ref, w0_ref, w1_ref, w2_ref, wg_ref,
                g0_ref, g1_ref, g2_ref, gb_ref, o_ref):
    dot = lambda a, w: jnp.dot(a[...], w[...], preferred_element_type=F32)
    br_a = dot(ya_ref, w0_ref)
    br_b = dot(yb_ref, w1_ref)
    yc = yc_ref[...]
    br_c = jnp.dot(yc, w2_ref[...], preferred_element_type=F32) * _sigmoid(
        jnp.dot(yc, wg_ref[...], preferred_element_type=F32))
    gate = lambda g_ref, i: _sigmoid(g_ref[...].astype(F32) + gb_ref[i:i + 1, :])
    merged = gate(g0_ref, 0) * br_a + gate(g1_ref, 1) * br_b + gate(g2_ref, 2) * br_c
    o_ref[...] = merged.astype(o_ref.dtype)


def branch_merge(ya, yb, yc_steps, w_branch, w_glu, layer, proj, gate_col0, gate_bias, *, tm, tn):
    t_, kdim = ya.shape
    blk, m, _ = yc_steps.shape
    n = w_glu.shape[2]
    in_w = proj.shape[1]
    gc0 = gate_col0 // tn
    nj = n // tn
    blocked = lambda v: v.reshape(m, blk * v.shape[1])
    y_spec = pl.BlockSpec((tm, kdim), lambda j, i, c: (i, j))
    wb_spec = lambda br: pl.BlockSpec((None, None, kdim, tn), lambda j, i, c: (layer, br, 0, c))
    g_spec = lambda br: pl.BlockSpec((tm, tn), lambda j, i, c: (i, j * (in_w // tn) + gc0 + br * nj + c))
    pb = blocked(proj)
    out = pl.pallas_call(
        _merge_body,
        grid=(blk, m // tm, nj),
        in_specs=[y_spec, y_spec, pl.BlockSpec((None, tm, kdim), lambda j, i, c: (j, i, 0)),
                  wb_spec(0), wb_spec(1), wb_spec(2),
                  pl.BlockSpec((None, kdim, tn), lambda j, i, c: (layer, 0, c)),
                  g_spec(0), g_spec(1), g_spec(2),
                  pl.BlockSpec((N_BRANCH, tn), lambda j, i, c: (0, c))],
        out_specs=pl.BlockSpec((tm, tn), lambda j, i, c: (i, j * nj + c)),
        out_shape=jax.ShapeDtypeStruct((m, blk * n), BF16),
        compiler_params=_cparams(("parallel", "parallel", "arbitrary")),
        name="branch_merge",
    )(blocked(ya), blocked(yb), yc_steps, w_branch, w_branch, w_branch, w_glu, pb, pb, pb,
      gate_bias.astype(F32))
    return out.reshape(t_, n)


def _matmul_res_body(a_ref, w_ref, r_ref, o_ref):
    o_ref[...] = r_ref[...] + jnp.dot(a_ref[...], w_ref[...], preferred_element_type=F32)


def matmul_residual(a, w, layer, res, *, tm, tn):
    m, k = a.shape
    n = w.shape[2]
    return pl.pallas_call(
        _matmul_res_body,
        grid=(m // tm, n // tn),
        in_specs=[pl.BlockSpec((tm, k), lambda i, j: (i, 0)),
                  pl.BlockSpec((None, k, tn), lambda i, j: (layer, 0, j)),
                  pl.BlockSpec((tm, tn), lambda i, j: (i, j))],
        out_specs=pl.BlockSpec((tm, tn), lambda i, j: (i, j)),
        out_shape=jax.ShapeDtypeStruct((m, n), F32),
        compiler_params=_cparams(("parallel", "arbitrary")),
        name="matmul_residual",
    )(a, w, res)


def _ffn_body(x_ref, g_ref, wg_ref, wu_ref, wd_ref, o_ref, h_ref):
    f = pl.program_id(1)

    @pl.when(f == 0)
    def _():
        x = x_ref[...]
        h_ref[...] = _rmsnorm_rows(x, g_ref[...]).astype(BF16)
        o_ref[...] = x

    h = h_ref[...]
    gate = jnp.dot(h, wg_ref[...], preferred_element_type=F32)
    up = jnp.dot(h, wu_ref[...], preferred_element_type=F32)
    act = (gate * _sigmoid(gate) * up).astype(BF16)
    o_ref[...] += jnp.dot(act, wd_ref[...], preferred_element_type=F32)


def ffn(x, g, w_gate, w_up, w_down, layer, *, tm, tf):
    m, d = x.shape
    hidden = w_gate.shape[2]
    return pl.pallas_call(
        _ffn_body,
        grid=(m // tm, hidden // tf),
        in_specs=[pl.BlockSpec((tm, d), lambda i, f: (i, 0)),
                  pl.BlockSpec((1, d), lambda i, f: (0, 0)),
                  pl.BlockSpec((None, d, tf), lambda i, f: (layer, 0, f)),
                  pl.BlockSpec((None, d, tf), lambda i, f: (layer, 0, f)),
                  pl.BlockSpec((None, tf, d), lambda i, f: (layer, f, 0))],
        out_specs=pl.BlockSpec((tm, d), lambda i, f: (i, 0)),
        out_shape=jax.ShapeDtypeStruct((m, d), F32),
        scratch_shapes=[pltpu.VMEM((tm, d), BF16)],
        compiler_params=_cparams(("parallel", "arbitrary")),
        name="swiglu_ffn",
    )(x, g.reshape(1, d), w_gate, w_up, w_down)


def _final_norm_body(x_ref, g_ref, o_ref):
    o_ref[...] = _rmsnorm_rows(x_ref[...], g_ref[...])


def final_norm(x, g, *, tm):
    m, d = x.shape
    return pl.pallas_call(
        _final_norm_body,
        grid=(m // tm,),
        in_specs=[pl.BlockSpec((tm, d), lambda i: (i, 0)), pl.BlockSpec((1, d), lambda i: (0, 0))],
        out_specs=pl.BlockSpec((tm, d), lambda i: (i, 0)),
        out_shape=jax.ShapeDtypeStruct((m, d), F32),
        compiler_params=_cparams(("parallel",)),
        name="final_rmsnorm",
    )(x, g.reshape(1, d))


def _pick_tile(n, want):
    t = min(n, want)
    while n % t:
        t //= 2
    return t


def kernel(x, norm_mix_g, w_in, gate_bias, lru_conv_w, lru_conv_b, lru_wa, lru_ba, lru_wx, lru_bx,
           lru_lambda, attn_rel_bias, ssm_a_re, ssm_a_im, ssm_b_re, ssm_b_im, ssm_c_re, ssm_c_im,
           ssm_d, ssm_log_step, ssm_w_glu, w_branch, w_out, norm_ffn_g, w_ffn_gate, w_ffn_up,
           w_ffn_down, norm_final_g):
    b_, s_, d_model = x.shape
    depth = w_in.shape[0]
    mix_w = lru_conv_w.shape[2]
    n_heads = attn_rel_bias.shape[1]
    n_g = ssm_a_re.shape[1]
    t_ = b_ * s_
    assert s_ % ATT_QBLK == 0 and s_ % S5_BLK == 0 and n_g % S5_GT == 0
    c_lru_x, c_lru_g, c_q, c_k, c_v, c_u, c_gate = [i * mix_w for i in range(7)]
    in_w = w_in.shape[2]

    tm = _pick_tile(t_, 1024)
    ts_lru = _pick_tile(s_, 512)
    bias = rel_bias_matrix(attn_rel_bias.reshape(depth * n_heads, N_REL))
    w_in, w_branch, ssm_w_glu, w_out, w_ffn_gate, w_ffn_up, w_ffn_down = [
        w.astype(BF16) for w in (w_in, w_branch, ssm_w_glu, w_out, w_ffn_gate, w_ffn_up, w_ffn_down)]

    xf = x.reshape(t_, d_model).astype(F32)
    for l in range(depth):
        proj = norm_matmul(xf, norm_mix_g[l], w_in, l, tm=tm, tn=1024)
        proj3 = proj.reshape(b_, s_, in_w)
        y_a = lru_branch(proj3, c_lru_x, c_lru_g, lru_conv_w[l], lru_conv_b[l], lru_wa[l], lru_ba[l],
                         lru_wx[l], lru_bx[l], lru_lambda[l], ts=ts_lru)
        y_b = attention_branch(proj3, c_q, c_k, c_v, bias, l, n_heads)

        w1, mm, lam = s5_prepare(ssm_a_re[l], ssm_a_im[l], ssm_b_re[l], ssm_b_im[l],
                                 ssm_c_re[l], ssm_c_im[l], ssm_log_step[l])
        y_c = s5_branch(proj.reshape(t_ // S5_BLK, S5_BLK * in_w), c_u, w1, mm, lam,
                        ssm_d[l].astype(F32).reshape(1, mix_w), batch=b_)

        merged = branch_merge(y_a.reshape(t_, mix_w), y_b.reshape(t_, mix_w), y_c, w_branch, ssm_w_glu, l,
                              proj, c_gate, gate_bias[l], tm=_pick_tile(t_ // S5_BLK, 1024), tn=512)
        xf = matmul_residual(merged, w_out, l, xf, tm=tm, tn=1024)
        xf = ffn(xf, norm_ffn_g[l], w_ffn_gate, w_ffn_up, w_ffn_down, l, tm=_pick_tile(t_, 512), tf=512)
    out = final_norm(xf, norm_final_g.astype(F32), tm=_pick_tile(t_, 512))
    return out.reshape(b_, s_, d_model).astype(x.dtype)
```

```python
import functools
import math

import jax
import jax.numpy as jnp
from jax import lax
from jax.experimental import pallas as pl
from jax.experimental.pallas import tpu as pltpu

F32 = jnp.float32
BF16 = jnp.bfloat16

CHUNK = 64
LRU_BW = 64
CONV_W = 4
LRU_C = 8.0
ATT_HD = 128
ATT_LEFT_CHUNKS = 8
MAX_REL = 128
N_REL = 2 * MAX_REL + 1
SSM_GROUP = 16
SSM_P = 64
N_BRANCH = 3
NORM_EPS = 1e-6
MASK_VALUE = -1e30

LANES = 128
SUBLANES = 8
MXU_DIM = 256
VMEM_LIMIT_BYTES = 56 * 1024 * 1024

S5_BLK = 16
S5_GT = LANES // SSM_GROUP
LOG2_E = math.log2(math.e)
ATT_HEADS_PER_STEP = 4
ATT_QBLK = 4 * CHUNK
ATT_KWIN = 3 * ATT_QBLK
LRU_CW = MXU_DIM


def _cparams(sem):
    return pltpu.CompilerParams(dimension_semantics=sem, vmem_limit_bytes=VMEM_LIMIT_BYTES)


def _rmsnorm_rows(x, g):
    ms = jnp.mean(x * x, axis=-1, keepdims=True)
    return x * lax.rsqrt(ms + NORM_EPS) * g


def _norm_matmul_body(x_ref, g_ref, w_ref, o_ref, o32_ref, h_ref, *, f32_block):
    j = pl.program_id(1)

    @pl.when(j == 0)
    def _():
        h_ref[...] = _rmsnorm_rows(x_ref[...], g_ref[...]).astype(BF16)

    acc = jnp.dot(h_ref[...], w_ref[...], preferred_element_type=F32)
    o_ref[...] = acc.astype(o_ref.dtype)

    @pl.when(j == f32_block)
    def _():
        o32_ref[...] = acc


def norm_matmul(x, g, w, layer, *, tm, tn, f32_col0):
    m, k = x.shape
    n = w.shape[2]
    return pl.pallas_call(
        functools.partial(_norm_matmul_body, f32_block=f32_col0 // tn),
        grid=(m // tm, n // tn),
        in_specs=[
            pl.BlockSpec((tm, k), lambda i, j: (i, 0)),
            pl.BlockSpec((1, k), lambda i, j: (0, 0)),
            pl.BlockSpec((None, k, tn), lambda i, j: (layer, 0, j)),
        ],
        out_specs=[pl.BlockSpec((tm, tn), lambda i, j: (i, j)),
                   pl.BlockSpec((tm, tn), lambda i, j: (i, 0))],
        out_shape=[jax.ShapeDtypeStruct((m, n), BF16), jax.ShapeDtypeStruct((m, tn), F32)],
        scratch_shapes=[pltpu.VMEM((tm, k), BF16)],
        compiler_params=_cparams(("parallel", "arbitrary")),
        name="norm_matmul",
    )(x, g.reshape(1, k), w)


def _softplus(x):
    return jnp.maximum(x, 0.0) + jnp.log1p(jnp.exp(-jnp.abs(x)))


def _sigmoid(x):
    return 0.5 * jnp.tanh(0.5 * x) + 0.5


def _lru_body(x_ref, gate_ref, cw_ref, cb_ref, wa_ref, wx_ref, ba_ref, bx_ref, lam_ref,
              o_ref, xbuf_ref, h_ref, *, ts):
    t = pl.program_id(2)

    @pl.when(t == 0)
    def _():
        xbuf_ref[0:SUBLANES, :] = jnp.zeros((SUBLANES, LRU_CW), F32)
        h_ref[...] = jnp.zeros_like(h_ref)

    x = x_ref[...].astype(F32)
    xbuf_ref[SUBLANES:SUBLANES + ts, :] = x
    xc = cb_ref[...] + x * cw_ref[CONV_W - 1:CONV_W, :]
    for k in range(CONV_W - 1):
        back = CONV_W - 1 - k
        xc = xc + xbuf_ref[pl.ds(SUBLANES - back, ts), :] * cw_ref[k:k + 1, :]
    xbuf_ref[0:SUBLANES, :] = x[ts - SUBLANES:ts, :]

    xcb = xc.astype(BF16)
    r = _sigmoid(jnp.dot(xcb, wa_ref[...], preferred_element_type=F32) + ba_ref[...])
    i = _sigmoid(jnp.dot(xcb, wx_ref[...], preferred_element_type=F32) + bx_ref[...])
    th = jnp.tanh((-0.5 * LRU_C) * r * _softplus(-lam_ref[...]))
    rinv = 1.0 / (1.0 - th)
    a = (1.0 + th) * rinv
    b = jnp.sqrt(-4.0 * th * rinv * rinv) * (i * xc)

    nt = ts // SUBLANES
    a = a.reshape(nt, SUBLANES, LRU_CW)
    b = b.reshape(nt, SUBLANES, LRU_CW)
    sub = lax.broadcasted_iota(jnp.int32, (nt, SUBLANES, LRU_CW), 1)
    s = 1
    while s < SUBLANES:
        keep = sub >= s
        b_sh = pltpu.roll(b, s, 1)
        a_sh = pltpu.roll(a, s, 1)
        b = b + jnp.where(keep, a * b_sh, 0.0)
        a = jnp.where(keep, a * a_sh, a)
        s *= 2
    carry = h_ref[...]
    tiles = []
    for k in range(nt):
        hk = b[k] + a[k] * carry
        tiles.append(hk)
        carry = hk[SUBLANES - 1:SUBLANES, :]
    h_ref[...] = carry
    h = jnp.concatenate(tiles, axis=0)
    o_ref[...] = (h * jax.nn.gelu(gate_ref[...].astype(F32))).astype(o_ref.dtype)


def _block_diag_tiles(w):
    n_blk, bw, _ = w.shape
    per = LRU_CW // bw
    wt = w.reshape(n_blk // per, per, bw, bw)
    eye = jnp.eye(per, dtype=w.dtype)
    tiles = jnp.einsum("cpkj,pq->cpkqj", wt, eye)
    return tiles.reshape(n_blk // per, LRU_CW, LRU_CW)


def lru_branch(proj3, x_col0, gate_col0, conv_w, conv_b, wa, ba, wx, bx, lam, *, ts):
    b_, s_, _ = proj3.shape
    width = conv_w.shape[1]
    n_c = width // LRU_CW
    xc0 = x_col0 // LRU_CW
    gc0 = gate_col0 // LRU_CW
    row = lambda v: v.reshape(1, width).astype(F32)
    vec_spec = pl.BlockSpec((1, LRU_CW), lambda b, c, t: (0, c))
    return pl.pallas_call(
        functools.partial(_lru_body, ts=ts),
        grid=(b_, n_c, s_ // ts),
        in_specs=[
            pl.BlockSpec((None, ts, LRU_CW), lambda b, c, t: (b, t, xc0 + c)),
            pl.BlockSpec((None, ts, LRU_CW), lambda b, c, t: (b, t, gc0 + c)),
            pl.BlockSpec((CONV_W, LRU_CW), lambda b, c, t: (0, c)),
            vec_spec,
            pl.BlockSpec((None, LRU_CW, LRU_CW), lambda b, c, t: (c, 0, 0)),
            pl.BlockSpec((None, LRU_CW, LRU_CW), lambda b, c, t: (c, 0, 0)),
            vec_spec, vec_spec, vec_spec,
        ],
        out_specs=pl.BlockSpec((None, ts, LRU_CW), lambda b, c, t: (b, t, c)),
        out_shape=jax.ShapeDtypeStruct((b_, s_, width), BF16),
        scratch_shapes=[pltpu.VMEM((ts + SUBLANES, LRU_CW), F32), pltpu.VMEM((1, LRU_CW), F32)],
        compiler_params=_cparams(("parallel", "parallel", "arbitrary")),
        name="rg_lru",
    )(proj3, proj3, conv_w.astype(F32), row(conv_b), _block_diag_tiles(wa).astype(BF16),
      _block_diag_tiles(wx).astype(BF16), row(ba), row(bx), row(lam))


def _rel_bias_body(tab_ref, o_ref):
    h = pl.program_id(0)
    width = 4 * ATT_QBLK
    m = lax.broadcasted_iota(jnp.int32, (1, width), 1)
    delta = jnp.where(m < ATT_KWIN, m, m - width)
    idx = jnp.clip(2 * ATT_QBLK - delta, -MAX_REL, MAX_REL) + MAX_REL

    def pick(d, acc):
        return jnp.where(idx == d, tab_ref[h, d], acc)

    gvec = lax.fori_loop(0, N_REL, pick, jnp.zeros((1, width), F32))
    full = pltpu.roll(jnp.broadcast_to(gvec, (ATT_QBLK, width)), 0, 1, stride=1, stride_axis=0)
    chunk_shift = int(math.log2(CHUNK))
    qc = lax.shift_right_logical(lax.broadcasted_iota(jnp.int32, (ATT_QBLK, ATT_KWIN), 0), chunk_shift)
    kc = lax.shift_right_logical(lax.broadcasted_iota(jnp.int32, (ATT_QBLK, ATT_KWIN), 1), chunk_shift)
    in_band = (kc >= qc) & (kc <= qc + ATT_LEFT_CHUNKS)
    o_ref[...] = jnp.where(in_band, full[:, :ATT_KWIN] * LOG2_E, MASK_VALUE)


def rel_bias_matrix(rel_bias):
    n = rel_bias.shape[0]
    return pl.pallas_call(
        _rel_bias_body,
        grid=(n,),
        in_specs=[pl.BlockSpec(memory_space=pltpu.SMEM)],
        out_specs=pl.BlockSpec((None, ATT_QBLK, ATT_KWIN), lambda h: (h, 0, 0)),
        out_shape=jax.ShapeDtypeStruct((n, ATT_QBLK, ATT_KWIN), F32),
        compiler_params=_cparams(("arbitrary",)),
        name="rel_bias",
    )(rel_bias.astype(F32))


def _attn_body(q_ref, k0_ref, k1_ref, k2_ref, v0_ref, v1_ref, v2_ref, bias_ref, o_ref):
    qb = pl.program_id(2)
    nt = (((1,), (1,)), ((), ()))
    chunk_shift = int(math.log2(CHUNK))
    kc = lax.shift_right_logical(lax.broadcasted_iota(jnp.int32, (1, ATT_KWIN), 1), chunk_shift)
    first_chunk = (qb - 2) * (ATT_QBLK // CHUNK)
    before_start = jnp.where(kc + first_chunk >= 0, 0.0, MASK_VALUE)
    for h in range(ATT_HEADS_PER_STEP):
        cols = slice(h * ATT_HD, (h + 1) * ATT_HD)
        q = q_ref[:, cols]
        s = jnp.concatenate(
            [lax.dot_general(q, k_ref[:, cols], nt, preferred_element_type=F32)
             for k_ref in (k0_ref, k1_ref, k2_ref)], axis=1)
        s = s * (ATT_HD ** -0.5 * LOG2_E) + bias_ref[h] + before_start
        m = jnp.max(s, axis=-1, keepdims=True)
        p = jnp.exp2(s - m)
        l = jnp.sum(p, axis=-1, keepdims=True)
        pb = p.astype(BF16)
        o = jnp.dot(pb[:, 0:ATT_QBLK], v0_ref[:, cols], preferred_element_type=F32)
        o = o + jnp.dot(pb[:, ATT_QBLK:2 * ATT_QBLK], v1_ref[:, cols], preferred_element_type=F32)
        o = o + jnp.dot(pb[:, 2 * ATT_QBLK:], v2_ref[:, cols], preferred_element_type=F32)
        o_ref[:, cols] = (o / l).astype(o_ref.dtype)


def attention_branch(proj3, q_col0, k_col0, v_col0, bias, layer, n_heads):
    b_, s_, _ = proj3.shape
    hp = ATT_HEADS_PER_STEP
    wblk = hp * ATT_HD
    qc0, kc0, vc0 = q_col0 // wblk, k_col0 // wblk, v_col0 // wblk

    def kv_spec(col0, back):
        return pl.BlockSpec((None, ATT_QBLK, wblk),
                            lambda h, b, i: (b, jnp.maximum(i - back, 0), col0 + h))

    return pl.pallas_call(
        _attn_body,
        grid=(n_heads // hp, b_, s_ // ATT_QBLK),
        in_specs=[
            pl.BlockSpec((None, ATT_QBLK, wblk), lambda h, b, i: (b, i, qc0 + h)),
            kv_spec(kc0, 2), kv_spec(kc0, 1), kv_spec(kc0, 0),
            kv_spec(vc0, 2), kv_spec(vc0, 1), kv_spec(vc0, 0),
            pl.BlockSpec((hp, ATT_QBLK, ATT_KWIN), lambda h, b, i: (layer * (n_heads // hp) + h, 0, 0)),
        ],
        out_specs=pl.BlockSpec((None, ATT_QBLK, wblk), lambda h, b, i: (b, i, h)),
        out_shape=jax.ShapeDtypeStruct((b_, s_, n_heads * ATT_HD), BF16),
        compiler_params=_cparams(("parallel", "parallel", "parallel")),
        name="chunk_attention",
    )(proj3, proj3, proj3, proj3, proj3, proj3, proj3, bias)


def _s5_prep_body(are_c, aim_c, ls_c, are_r, aim_r, ls_r, ctre, ctim, bta, btb, w_ref, m_ref, lam_ref):
    blk = S5_BLK
    tw = S5_GT * SSM_GROUP
    sw = S5_GT * 2 * SSM_P
    lg_grp, lg_p = int(math.log2(SSM_GROUP)), int(math.log2(SSM_P))
    group_of_state = lambda q: lax.shift_right_logical(q, lg_p) & (S5_GT - 1)

    def lam_bar(are, aim, ls):
        step = jnp.exp(ls)
        mag = jnp.exp(are * step)
        return mag * jnp.cos(aim * step), mag * jnp.sin(aim * step)

    lr_c, li_c = lam_bar(are_c[...], aim_c[...], ls_c[...])
    rowc = lax.broadcasted_iota(jnp.int32, (sw, tw), 0)
    lanec = lax.broadcasted_iota(jnp.int32, (sw, tw), 1)
    same_c = group_of_state(rowc) == lax.shift_right_logical(lanec, lg_grp)
    im_row = rowc >= sw // 2
    cre = jnp.where(same_c, ctre[...], 0.0)
    cim = jnp.where(same_c, ctim[...], 0.0)
    pr = jnp.ones((sw, tw), F32)
    pi = jnp.zeros((sw, tw), F32)
    cl = []
    for _ in range(blk + 1):
        cl.append(jnp.where(im_row, -(cre * pi + cim * pr), cre * pr - cim * pi))
        pr, pi = pr * lr_c - pi * li_c, pr * li_c + pi * lr_c
    m_ref[...] = jnp.concatenate(cl[1:], axis=1).astype(BF16)
    cl0 = jnp.concatenate(cl[:blk], axis=1)

    ar, ai = are_r[...], aim_r[...]
    lr, li = lam_bar(ar, ai, ls_r[...])
    den = ar * ar + ai * ai
    nr = lr - 1.0
    cr = (nr * ar + li * ai) / den
    ci = (li * ar - nr * ai) / den
    rowr = lax.broadcasted_iota(jnp.int32, (tw, sw), 0)
    laner = lax.broadcasted_iota(jnp.int32, (tw, sw), 1)
    same_r = lax.shift_right_logical(rowr, lg_grp) == group_of_state(laner)
    ba = jnp.where(same_r, bta[...], 0.0)
    bb = jnp.where(same_r, btb[...], 0.0)
    bb2 = cr * ba + ci * bb
    bb2s = cr * bb - ci * ba

    kt = jnp.dot(bb2, cl0, preferred_element_type=F32, precision=lax.Precision.HIGHEST).astype(BF16)
    pr = jnp.ones((1, sw), F32)
    pi = jnp.zeros((1, sw), F32)
    for i in range(blk - 1, -1, -1):
        rows = slice(i * tw, (i + 1) * tw)
        if i:
            w_ref[rows, 0:i * tw] = jnp.zeros((tw, i * tw), BF16)
        w_ref[rows, i * tw:blk * tw] = kt[:, 0:(blk - i) * tw]
        w_ref[rows, blk * tw:] = (pr * bb2 + pi * bb2s).astype(BF16)
        pr, pi = pr * lr - pi * li, pr * li + pi * lr
    sign = jnp.where(lax.broadcasted_iota(jnp.int32, (1, sw), 1) >= sw // 2, 1.0, -1.0)
    lam_ref[...] = jnp.zeros_like(lam_ref)
    lam_ref[0:1, :] = pr
    lam_ref[1:2, :] = sign * pi


def s5_prepare(a_re, a_im, b_re, b_im, c_re, c_im, log_step):
    n_g, npp = a_re.shape
    grp = b_re.shape[2]
    blk, gt = S5_BLK, S5_GT
    n_t = n_g // gt
    tw, sw = gt * grp, gt * 2 * npp
    f = lambda v: v.astype(F32)
    pair = lambda v: jnp.concatenate([f(v).reshape(n_t, gt * npp)] * 2, axis=-1)
    as_col = lambda v: jnp.broadcast_to(v.reshape(n_t, sw, 1), (n_t, sw, tw))
    as_row = lambda v: v.reshape(n_t, 1, sw)
    ls2 = pair(jnp.broadcast_to(f(log_step)[:, None], (n_g, npp)))
    c_t = lambda v: jnp.tile(f(v).transpose(0, 2, 1).reshape(n_t, gt * npp, grp), (1, 1, gt))
    ct = lambda v: jnp.concatenate([c_t(v), c_t(v)], axis=1)
    b_t = lambda v: jnp.broadcast_to(f(v).transpose(0, 2, 1).reshape(n_t, gt, grp, 1, npp),
                                     (n_t, gt, grp, gt, npp)).reshape(n_t, tw, gt * npp)
    btr, bti = b_t(b_re), b_t(b_im)
    bt = lambda lo, hi: jnp.concatenate([lo, hi], axis=-1)
    t3 = lambda shp: pl.BlockSpec((None,) + shp, lambda t: (t, 0, 0))
    return pl.pallas_call(
        _s5_prep_body,
        grid=(n_t,),
        in_specs=[t3((sw, tw)), t3((sw, tw)), t3((sw, tw)), t3((1, sw)), t3((1, sw)), t3((1, sw)),
                  t3((sw, tw)), t3((sw, tw)), t3((tw, sw)), t3((tw, sw))],
        out_specs=[t3((blk * tw, blk * tw + sw)), t3((sw, blk * tw)), t3((SUBLANES, sw))],
        out_shape=[jax.ShapeDtypeStruct((n_t, blk * tw, blk * tw + sw), BF16),
                   jax.ShapeDtypeStruct((n_t, sw, blk * tw), BF16),
                   jax.ShapeDtypeStruct((n_t, SUBLANES, sw), F32)],
        compiler_params=_cparams(("parallel",)),
        name="s5_prepare",
    )(as_col(pair(a_re)), as_col(pair(a_im)), as_col(ls2), as_row(pair(a_re)), as_row(pair(a_im)),
      as_row(ls2), ct(c_re), ct(c_im), bt(btr, bti), bt(-bti, btr))


def _s5_body(u_ref, w_ref, m_ref, lam_ref, d_ref, o_ref, *, rows):
    blk = S5_BLK
    tw = S5_GT * SSM_GROUP
    sw = lam_ref.shape[1]
    u_steps = [u_ref[pl.ds(i, rows, stride=blk), :] for i in range(blk)]
    lhs = jnp.concatenate([u.astype(BF16) for u in u_steps], axis=1)
    r = jnp.dot(lhs, w_ref[...], preferred_element_type=F32)
    z = r[:, blk * tw:]
    row = lax.broadcasted_iota(jnp.int32, (rows, sw), 0)
    ca = lam_ref[0:1, :]
    cb = lam_ref[1:2, :]
    s = 1
    while s < rows:
        zs = jnp.where(row >= s, pltpu.roll(z, s, 0), 0.0)
        z = z + ca * zs + cb * pltpu.roll(zs, sw // 2, 1)
        ca, cb = ca * ca - cb * cb, 2.0 * ca * cb
        s *= 2
    zprev = jnp.where(row >= 1, pltpu.roll(z, 1, 0), 0.0)
    ys = jnp.dot(zprev.astype(BF16), m_ref[...], preferred_element_type=F32)
    d = d_ref[...]
    for j in range(blk):
        cols = slice(j * tw, (j + 1) * tw)
        y = r[:, cols] + ys[:, cols] + d * u_steps[j]
        o_ref[pl.ds(j, rows, stride=blk), :] = jax.nn.gelu(y)


def s5_branch(u, w, mm, lam, d, *, batch):
    t_, width = u.shape
    seq = t_ // batch
    n_t, sw = w.shape[0], lam.shape[2]
    tw = S5_GT * SSM_GROUP
    return pl.pallas_call(
        functools.partial(_s5_body, rows=seq // S5_BLK),
        grid=(n_t, batch),
        in_specs=[
            pl.BlockSpec((seq, tw), lambda c, b: (b, c)),
            pl.BlockSpec((None,) + w.shape[1:], lambda c, b: (c, 0, 0)),
            pl.BlockSpec((None,) + mm.shape[1:], lambda c, b: (c, 0, 0)),
            pl.BlockSpec((None, SUBLANES, sw), lambda c, b: (c, 0, 0)),
            pl.BlockSpec((1, tw), lambda c, b: (0, c)),
        ],
        out_specs=pl.BlockSpec((seq, tw), lambda c, b: (b, c)),
        out_shape=jax.ShapeDtypeStruct((t_, width), F32),
        compiler_params=_cparams(("parallel", "arbitrary")),
        name="s5_blocked",
    )(u, w, mm, lam, d)


def _merge_body(ya_ref, yb_ref, yc_ref, w0_ref, w1_ref, w2_ref, wg_ref,
                g0_ref, g1_ref, g2_ref, gb_ref, o_ref):
    dot = lambda a, w: jnp.dot(a[...], w[...], preferred_element_type=F32)
    br_a = dot(ya_ref, w0_ref)
    br_b = dot(yb_ref, w1_ref)
    yc = yc_ref[...].astype(BF16)
    br_c = jnp.dot(yc, w2_ref[...], preferred_element_type=F32) * _sigmoid(
        jnp.dot(yc, wg_ref[...], preferred_element_type=F32))
    gate = lambda g_ref, i: _sigmoid(g_ref[...].astype(F32) + gb_ref[i:i + 1, :])
    merged = gate(g0_ref, 0) * br_a + gate(g1_ref, 1) * br_b + gate(g2_ref, 2) * br_c
    o_ref[...] = merged.astype(o_ref.dtype)


def branch_merge(ya, yb, yc, w_branch, w_glu, layer, proj, gate_col0, gate_bias, *, tm, tn):
    m, kdim = ya.shape
    n = w_glu.shape[2]
    gc0 = gate_col0 // tn
    nj = n // tn
    y_spec = pl.BlockSpec((tm, kdim), lambda i, j: (i, 0))
    wb_spec = lambda br: pl.BlockSpec((None, None, kdim, tn), lambda i, j: (layer, br, 0, j))
    g_spec = lambda br: pl.BlockSpec((tm, tn), lambda i, j: (i, gc0 + br * nj + j))
    return pl.pallas_call(
        _merge_body,
        grid=(m // tm, nj),
        in_specs=[y_spec, y_spec, y_spec, wb_spec(0), wb_spec(1), wb_spec(2),
                  pl.BlockSpec((None, kdim, tn), lambda i, j: (layer, 0, j)),
                  g_spec(0), g_spec(1), g_spec(2),
                  pl.BlockSpec((N_BRANCH, tn), lambda i, j: (0, j))],
        out_specs=pl.BlockSpec((tm, tn), lambda i, j: (i, j)),
        out_shape=jax.ShapeDtypeStruct((m, n), BF16),
        compiler_params=_cparams(("parallel", "arbitrary")),
        name="branch_merge",
    )(ya, yb, yc, w_branch, w_branch, w_branch, w_glu, proj, proj, proj, gate_bias.astype(F32))


def _matmul_res_body(a_ref, w_ref, r_ref, o_ref):
    o_ref[...] = r_ref[...] + jnp.dot(a_ref[...], w_ref[...], preferred_element_type=F32)


def matmul_residual(a, w, layer, res, *, tm, tn):
    m, k = a.shape
    n = w.shape[2]
    return pl.pallas_call(
        _matmul_res_body,
        grid=(m // tm, n // tn),
        in_specs=[pl.BlockSpec((tm, k), lambda i, j: (i, 0)),
                  pl.BlockSpec((None, k, tn), lambda i, j: (layer, 0, j)),
                  pl.BlockSpec((tm, tn), lambda i, j: (i, j))],
        out_specs=pl.BlockSpec((tm, tn), lambda i, j: (i, j)),
        out_shape=jax.ShapeDtypeStruct((m, n), F32),
        compiler_params=_cparams(("parallel", "arbitrary")),
        name="matmul_residual",
    )(a, w, res)


def _ffn_body(x_ref, g_ref, wg_ref, wu_ref, wd_ref, o_ref, h_ref):
    f = pl.program_id(1)

    @pl.when(f == 0)
    def _():
        x = x_ref[...]
        h_ref[...] = _rmsnorm_rows(x, g_ref[...]).astype(BF16)
        o_ref[...] = x

    h = h_ref[...]
    gate = jnp.dot(h, wg_ref[...], preferred_element_type=F32)
    up = jnp.dot(h, wu_ref[...], preferred_element_type=F32)
    act = (gate * _sigmoid(gate) * up).astype(BF16)
    o_ref[...] += jnp.dot(act, wd_ref[...], preferred_element_type=F32)


def ffn(x, g, w_gate, w_up, w_down, layer, *, tm, tf):
    m, d = x.shape
    hidden = w_gate.shape[2]
    return pl.pallas_call(
        _ffn_body,
        grid=(m // tm, hidden // tf),
        in_specs=[pl.BlockSpec((tm, d), lambda i, f: (i, 0)),
                  pl.BlockSpec((1, d), lambda i, f: (0, 0)),
                  pl.BlockSpec((None, d, tf), lambda i, f: (layer, 0, f)),
                  pl.BlockSpec((None, d, tf), lambda i, f: (layer, 0, f)),
                  pl.BlockSpec((None, tf, d), lambda i, f: (layer, f, 0))],
        out_specs=pl.BlockSpec((tm, d), lambda i, f: (i, 0)),
        out_shape=jax.ShapeDtypeStruct((m, d), F32),
        scratch_shapes=[pltpu.VMEM((tm, d), BF16)],
        compiler_params=_cparams(("parallel", "arbitrary")),
        name="swiglu_ffn",
    )(x, g.reshape(1, d), w_gate, w_up, w_down)


def _final_norm_body(x_ref, g_ref, o_ref):
    o_ref[...] = _rmsnorm_rows(x_ref[...], g_ref[...])


def final_norm(x, g, *, tm):
    m, d = x.shape
    return pl.pallas_call(
        _final_norm_body,
        grid=(m // tm,),
        in_specs=[pl.BlockSpec((tm, d), lambda i: (i, 0)), pl.BlockSpec((1, d), lambda i: (0, 0))],
        out_specs=pl.BlockSpec((tm, d), lambda i: (i, 0)),
        out_shape=jax.ShapeDtypeStruct((m, d), F32),
        compiler_params=_cparams(("parallel",)),
        name="final_rmsnorm",
    )(x, g.reshape(1, d))


def _pick_tile(n, want):
    t = min(n, want)
    while n % t:
        t //= 2
    return t


def kernel(x, norm_mix_g, w_in, gate_bias, lru_conv_w, lru_conv_b, lru_wa, lru_ba, lru_wx, lru_bx,
           lru_lambda, attn_rel_bias, ssm_a_re, ssm_a_im, ssm_b_re, ssm_b_im, ssm_c_re, ssm_c_im,
           ssm_d, ssm_log_step, ssm_w_glu, w_branch, w_out, norm_ffn_g, w_ffn_gate, w_ffn_up,
           w_ffn_down, norm_final_g):
    b_, s_, d_model = x.shape
    depth = w_in.shape[0]
    mix_w = lru_conv_w.shape[2]
    n_heads = attn_rel_bias.shape[1]
    n_g = ssm_a_re.shape[1]
    t_ = b_ * s_
    assert s_ % ATT_QBLK == 0 and s_ % S5_BLK == 0 and n_g % S5_GT == 0
    c_lru_x, c_lru_g, c_q, c_k, c_v, c_u, c_gate = [i * mix_w for i in range(7)]
    in_w = w_in.shape[2]

    tm = _pick_tile(t_, 1024)
    ts_lru = _pick_tile(s_, 512)
    bias = rel_bias_matrix(attn_rel_bias.reshape(depth * n_heads, N_REL))
    w_in, w_branch, ssm_w_glu, w_out, w_ffn_gate, w_ffn_up, w_ffn_down = [
        w.astype(BF16) for w in (w_in, w_branch, ssm_w_glu, w_out, w_ffn_gate, w_ffn_up, w_ffn_down)]

    xf = x.reshape(t_, d_model).astype(F32)
    for l in range(depth):
        proj, u32 = norm_matmul(xf, norm_mix_g[l], w_in, l, tm=tm, tn=mix_w, f32_col0=c_u)
        proj3 = proj.reshape(b_, s_, in_w)
        y_a = lru_branch(proj3, c_lru_x, c_lru_g, lru_conv_w[l], lru_conv_b[l], lru_wa[l], lru_ba[l],
                         lru_wx[l], lru_bx[l], lru_lambda[l], ts=ts_lru)
        y_b = attention_branch(proj3, c_q, c_k, c_v, bias, l, n_heads)

        w1, mm, lam = s5_prepare(ssm_a_re[l], ssm_a_im[l], ssm_b_re[l], ssm_b_im[l],
                                 ssm_c_re[l], ssm_c_im[l], ssm_log_step[l])
        y_c = s5_branch(u32, w1, mm, lam, ssm_d[l].astype(F32).reshape(1, mix_w), batch=b_)

        merged = branch_merge(y_a.reshape(t_, mix_w), y_b.reshape(t_, mix_w), y_c, w_branch, ssm_w_glu, l,
                              proj, c_gate, gate_bias[l], tm=tm, tn=512)
        xf = matmul_residual(merged, w_out, l, xf, tm=tm, tn=1024)
        xf = ffn(xf, norm_ffn_g[l], w_ffn_gate, w_ffn_up, w_ffn_down, l, tm=_pick_tile(t_, 512), tf=512)
    out = final_norm(xf, norm_final_g.astype(F32), tm=_pick_tile(t_, 512))
    return out.reshape(b_, s_, d_model).astype(x.dtype)
```

```python
import functools
import math

import jax
import jax.numpy as jnp
from jax import lax
from jax.experimental import pallas as pl
from jax.experimental.pallas import tpu as pltpu

F32 = jnp.float32
BF16 = jnp.bfloat16

CHUNK = 64
LRU_BW = 64
CONV_W = 4
LRU_C = 8.0
ATT_HD = 128
ATT_LEFT_CHUNKS = 8
MAX_REL = 128
N_REL = 2 * MAX_REL + 1
SSM_GROUP = 16
SSM_P = 64
N_BRANCH = 3
NORM_EPS = 1e-6
MASK_VALUE = -1e30

LANES = 128
SUBLANES = 8
MXU_DIM = 256
VMEM_LIMIT_BYTES = 56 * 1024 * 1024

S5_BLK = 16
S5_GT = LANES // SSM_GROUP
LOG2_E = math.log2(math.e)
ATT_HEADS_PER_STEP = 8
ATT_QBLK = 4 * CHUNK
ATT_HALF = ATT_QBLK // 2
ATT_LEFT = ATT_LEFT_CHUNKS * CHUNK
ATT_HWIN = ATT_HALF + ATT_LEFT
LRU_CW = MXU_DIM


def _cparams(sem):
    return pltpu.CompilerParams(dimension_semantics=sem, vmem_limit_bytes=VMEM_LIMIT_BYTES)


def _rmsnorm_rows(x, g):
    ms = jnp.mean(x * x, axis=-1, keepdims=True)
    return x * lax.rsqrt(ms + NORM_EPS) * g


def _norm_matmul_body(x_ref, g_ref, w_ref, o_ref, o32_ref, h_ref, *, f32_block):
    j = pl.program_id(1)

    @pl.when(j == 0)
    def _():
        h_ref[...] = _rmsnorm_rows(x_ref[...], g_ref[...]).astype(BF16)

    acc = jnp.dot(h_ref[...], w_ref[...], preferred_element_type=F32)
    o_ref[...] = acc.astype(o_ref.dtype)

    @pl.when(j == f32_block)
    def _():
        o32_ref[...] = acc


def norm_matmul(x, g, w, layer, *, tm, tn, f32_col0):
    m, k = x.shape
    n = w.shape[2]
    return pl.pallas_call(
        functools.partial(_norm_matmul_body, f32_block=f32_col0 // tn),
        grid=(m // tm, n // tn),
        in_specs=[
            pl.BlockSpec((tm, k), lambda i, j: (i, 0)),
            pl.BlockSpec((1, k), lambda i, j: (0, 0)),
            pl.BlockSpec((None, k, tn), lambda i, j: (layer, 0, j)),
        ],
        out_specs=[pl.BlockSpec((tm, tn), lambda i, j: (i, j)),
                   pl.BlockSpec((tm, tn), lambda i, j: (i, 0))],
        out_shape=[jax.ShapeDtypeStruct((m, n), BF16), jax.ShapeDtypeStruct((m, tn), F32)],
        scratch_shapes=[pltpu.VMEM((tm, k), BF16)],
        compiler_params=_cparams(("parallel", "arbitrary")),
        name="norm_matmul",
    )(x, g.reshape(1, k), w)


def _softplus(x):
    return jnp.maximum(x, 0.0) + jnp.log1p(jnp.exp(-jnp.abs(x)))


def _sigmoid(x):
    return 0.5 * jnp.tanh(0.5 * x) + 0.5


def _lru_body(x_ref, gate_ref, cw_ref, cb_ref, wa_ref, wx_ref, ba_ref, bx_ref, lam_ref,
              o_ref, xbuf_ref, h_ref, *, ts):
    t = pl.program_id(2)

    @pl.when(t == 0)
    def _():
        xbuf_ref[0:SUBLANES, :] = jnp.zeros((SUBLANES, LRU_CW), F32)
        h_ref[...] = jnp.zeros_like(h_ref)

    x = x_ref[...].astype(F32)
    xbuf_ref[SUBLANES:SUBLANES + ts, :] = x
    xc = cb_ref[...] + x * cw_ref[CONV_W - 1:CONV_W, :]
    for k in range(CONV_W - 1):
        back = CONV_W - 1 - k
        xc = xc + xbuf_ref[pl.ds(SUBLANES - back, ts), :] * cw_ref[k:k + 1, :]
    xbuf_ref[0:SUBLANES, :] = x[ts - SUBLANES:ts, :]

    xcb = xc.astype(BF16)
    ta = jnp.tanh(jnp.dot(xcb, wa_ref[...], preferred_element_type=F32) + ba_ref[...])
    tx = jnp.tanh(jnp.dot(xcb, wx_ref[...], preferred_element_type=F32) + bx_ref[...])
    c1 = (-0.25 * LRU_C) * _softplus(-lam_ref[...])
    th = jnp.tanh(c1 * ta + c1)
    rinv = 1.0 / (1.0 - th)
    a = (1.0 + th) * rinv
    xh = 0.5 * xc
    b = (2.0 * rinv) * jnp.sqrt(-th) * (xh * tx + xh)

    nt = ts // SUBLANES
    a = a.reshape(nt, SUBLANES, LRU_CW)
    b = b.reshape(nt, SUBLANES, LRU_CW)
    sub = lax.broadcasted_iota(jnp.int32, (nt, SUBLANES, LRU_CW), 1)
    s = 1
    while s < SUBLANES:
        keep = sub >= s
        b_sh = pltpu.roll(b, s, 1)
        a_sh = pltpu.roll(a, s, 1)
        b = b + jnp.where(keep, a * b_sh, 0.0)
        a = jnp.where(keep, a * a_sh, a)
        s *= 2
    carry = h_ref[...]
    tiles = []
    for k in range(nt):
        hk = b[k] + a[k] * carry
        tiles.append(hk)
        carry = hk[SUBLANES - 1:SUBLANES, :]
    h_ref[...] = carry
    h = jnp.concatenate(tiles, axis=0)
    o_ref[...] = (h * jax.nn.gelu(gate_ref[...].astype(F32))).astype(o_ref.dtype)


def _block_diag_tiles(w):
    n_blk, bw, _ = w.shape
    per = LRU_CW // bw
    wt = w.reshape(n_blk // per, per, bw, bw)
    eye = jnp.eye(per, dtype=w.dtype)
    tiles = jnp.einsum("cpkj,pq->cpkqj", wt, eye)
    return tiles.reshape(n_blk // per, LRU_CW, LRU_CW)


def lru_branch(proj3, x_col0, gate_col0, conv_w, conv_b, wa, ba, wx, bx, lam, *, ts):
    b_, s_, _ = proj3.shape
    width = conv_w.shape[1]
    n_c = width // LRU_CW
    xc0 = x_col0 // LRU_CW
    gc0 = gate_col0 // LRU_CW
    row = lambda v: v.reshape(1, width).astype(F32)
    vec_spec = pl.BlockSpec((1, LRU_CW), lambda b, c, t: (0, c))
    return pl.pallas_call(
        functools.partial(_lru_body, ts=ts),
        grid=(b_, n_c, s_ // ts),
        in_specs=[
            pl.BlockSpec((None, ts, LRU_CW), lambda b, c, t: (b, t, xc0 + c)),
            pl.BlockSpec((None, ts, LRU_CW), lambda b, c, t: (b, t, gc0 + c)),
            pl.BlockSpec((CONV_W, LRU_CW), lambda b, c, t: (0, c)),
            vec_spec,
            pl.BlockSpec((None, LRU_CW, LRU_CW), lambda b, c, t: (c, 0, 0)),
            pl.BlockSpec((None, LRU_CW, LRU_CW), lambda b, c, t: (c, 0, 0)),
            vec_spec, vec_spec, vec_spec,
        ],
        out_specs=pl.BlockSpec((None, ts, LRU_CW), lambda b, c, t: (b, t, c)),
        out_shape=jax.ShapeDtypeStruct((b_, s_, width), BF16),
        scratch_shapes=[pltpu.VMEM((ts + SUBLANES, LRU_CW), F32), pltpu.VMEM((1, LRU_CW), F32)],
        compiler_params=_cparams(("parallel", "parallel", "arbitrary")),
        name="rg_lru",
    )(proj3, proj3, conv_w.astype(F32), row(conv_b), _block_diag_tiles(0.5 * wa).astype(BF16),
      _block_diag_tiles(0.5 * wx).astype(BF16), row(0.5 * ba), row(0.5 * bx), row(lam))


def _rel_bias_body(tab_ref, o_ref):
    h = pl.program_id(0)
    width = 1 << (ATT_HWIN + ATT_HALF - 1).bit_length()
    m = lax.broadcasted_iota(jnp.int32, (1, width), 1)
    delta = jnp.where(m < ATT_HWIN, m, m - width)
    idx = jnp.clip(ATT_LEFT - delta, -MAX_REL, MAX_REL) + MAX_REL

    def pick(d, acc):
        return jnp.where(idx == d, tab_ref[h, d], acc)

    gvec = lax.fori_loop(0, N_REL, pick, jnp.zeros((1, width), F32))
    full = pltpu.roll(jnp.broadcast_to(gvec, (ATT_HALF, width)), 0, 1, stride=1, stride_axis=0)
    chunk_shift = int(math.log2(CHUNK))
    qc = lax.shift_right_logical(lax.broadcasted_iota(jnp.int32, (ATT_HALF, ATT_HWIN), 0), chunk_shift)
    kc = lax.shift_right_logical(lax.broadcasted_iota(jnp.int32, (ATT_HALF, ATT_HWIN), 1), chunk_shift)
    in_band = (kc >= qc) & (kc <= qc + ATT_LEFT_CHUNKS)
    o_ref[...] = jnp.where(in_band, full[:, :ATT_HWIN] * LOG2_E, MASK_VALUE)


def rel_bias_matrix(rel_bias):
    n = rel_bias.shape[0]
    return pl.pallas_call(
        _rel_bias_body,
        grid=(n,),
        in_specs=[pl.BlockSpec(memory_space=pltpu.SMEM)],
        out_specs=pl.BlockSpec((None, ATT_HALF, ATT_HWIN), lambda h: (h, 0, 0)),
        out_shape=jax.ShapeDtypeStruct((n, ATT_HALF, ATT_HWIN), F32),
        compiler_params=_cparams(("arbitrary",)),
        name="rel_bias",
    )(rel_bias.astype(F32))


def _attn_body(q_ref, k0_ref, k1_ref, k2_ref, v0_ref, v1_ref, v2_ref, bias_ref, o_ref):
    qb = pl.program_id(2)
    nt = (((1,), (1,)), ((), ()))
    chunk_shift = int(math.log2(CHUNK))
    kc = lax.shift_right_logical(lax.broadcasted_iota(jnp.int32, (1, ATT_HWIN), 1), chunk_shift)
    k_refs = (k0_ref, k1_ref, k2_ref)
    v_refs = (v0_ref, v1_ref, v2_ref)
    units = [(r0, h) for r0 in range(0, ATT_QBLK, ATT_HALF) for h in range(ATT_HEADS_PER_STEP)]
    parts_of = lambda r0: ((0, r0, ATT_QBLK), (1, 0, ATT_QBLK), (2, 0, r0 + ATT_HALF))
    def score(r0, h):
        cols = slice(h * ATT_HD, (h + 1) * ATT_HD)
        first_chunk = (qb - 2) * (ATT_QBLK // CHUNK) + r0 // CHUNK
        before_start = jnp.where(kc + first_chunk >= 0, 0.0, MASK_VALUE)
        q = q_ref[r0:r0 + ATT_HALF, cols]
        s = jnp.concatenate(
            [lax.dot_general(q, k_refs[blk][a:b, cols], nt, preferred_element_type=F32)
             for blk, a, b in parts_of(r0)], axis=1)
        return s * (ATT_HD ** -0.5 * LOG2_E) + bias_ref[h] + before_start

    def softmax_terms(s):
        p = jnp.exp2(s - jnp.max(s, axis=-1, keepdims=True))
        return p.astype(BF16), jnp.sum(p, axis=-1, keepdims=True)

    def weighted_values(r0, h, pb, l):
        cols = slice(h * ATT_HD, (h + 1) * ATT_HD)
        o = jnp.zeros((ATT_HALF, ATT_HD), F32)
        c0 = 0
        for blk, a, b in parts_of(r0):
            o = o + jnp.dot(pb[:, c0:c0 + b - a], v_refs[blk][a:b, cols], preferred_element_type=F32)
            c0 += b - a
        o_ref[r0:r0 + ATT_HALF, cols] = (o / l).astype(o_ref.dtype)

    scores = [score(*u) for u in units]
    probs = [softmax_terms(s) for s in scores]
    for u, (pb, l) in zip(units, probs):
        weighted_values(*u, pb, l)


def attention_branch(proj3, q_col0, k_col0, v_col0, bias, layer, n_heads):
    b_, s_, _ = proj3.shape
    hp = ATT_HEADS_PER_STEP
    wblk = hp * ATT_HD
    qc0, kc0, vc0 = q_col0 // wblk, k_col0 // wblk, v_col0 // wblk

    def kv_spec(col0, back):
        return pl.BlockSpec((None, ATT_QBLK, wblk),
                            lambda h, b, i: (b, jnp.maximum(i - back, 0), col0 + h))

    return pl.pallas_call(
        _attn_body,
        grid=(n_heads // hp, b_, s_ // ATT_QBLK),
        in_specs=[
            pl.BlockSpec((None, ATT_QBLK, wblk), lambda h, b, i: (b, i, qc0 + h)),
            kv_spec(kc0, 2), kv_spec(kc0, 1), kv_spec(kc0, 0),
            kv_spec(vc0, 2), kv_spec(vc0, 1), kv_spec(vc0, 0),
            pl.BlockSpec((hp, ATT_HALF, ATT_HWIN), lambda h, b, i: (layer * (n_heads // hp) + h, 0, 0)),
        ],
        out_specs=pl.BlockSpec((None, ATT_QBLK, wblk), lambda h, b, i: (b, i, h)),
        out_shape=jax.ShapeDtypeStruct((b_, s_, n_heads * ATT_HD), BF16),
        compiler_params=_cparams(("parallel", "parallel", "parallel")),
        name="chunk_attention",
    )(proj3, proj3, proj3, proj3, proj3, proj3, proj3, bias)


def _s5_prep_body(are_c, aim_c, ls_c, are_r, aim_r, ls_r, ctre, ctim, bta, btb, w_ref, m_ref, lam_ref):
    blk = S5_BLK
    tw = S5_GT * SSM_GROUP
    sw = S5_GT * 2 * SSM_P
    lg_grp, lg_p = int(math.log2(SSM_GROUP)), int(math.log2(SSM_P))
    group_of_state = lambda q: lax.shift_right_logical(q, lg_p) & (S5_GT - 1)

    def lam_bar(are, aim, ls):
        step = jnp.exp(ls)
        mag = jnp.exp(are * step)
        return mag * jnp.cos(aim * step), mag * jnp.sin(aim * step)

    lr_c, li_c = lam_bar(are_c[...], aim_c[...], ls_c[...])
    rowc = lax.broadcasted_iota(jnp.int32, (sw, tw), 0)
    lanec = lax.broadcasted_iota(jnp.int32, (sw, tw), 1)
    same_c = group_of_state(rowc) == lax.shift_right_logical(lanec, lg_grp)
    im_row = rowc >= sw // 2
    cre = jnp.where(same_c, ctre[...], 0.0)
    cim = jnp.where(same_c, ctim[...], 0.0)
    pr = jnp.ones((sw, tw), F32)
    pi = jnp.zeros((sw, tw), F32)
    cl = []
    for _ in range(blk + 1):
        cl.append(jnp.where(im_row, -(cre * pi + cim * pr), cre * pr - cim * pi))
        pr, pi = pr * lr_c - pi * li_c, pr * li_c + pi * lr_c
    m_ref[...] = jnp.concatenate(cl[1:], axis=1).astype(BF16)
    cl0 = jnp.concatenate(cl[:blk], axis=1)

    ar, ai = are_r[...], aim_r[...]
    lr, li = lam_bar(ar, ai, ls_r[...])
    den = ar * ar + ai * ai
    nr = lr - 1.0
    cr = (nr * ar + li * ai) / den
    ci = (li * ar - nr * ai) / den
    rowr = lax.broadcasted_iota(jnp.int32, (tw, sw), 0)
    laner = lax.broadcasted_iota(jnp.int32, (tw, sw), 1)
    same_r = lax.shift_right_logical(rowr, lg_grp) == group_of_state(laner)
    ba = jnp.where(same_r, bta[...], 0.0)
    bb = jnp.where(same_r, btb[...], 0.0)
    bb2 = cr * ba + ci * bb
    bb2s = cr * bb - ci * ba

    kt = jnp.dot(bb2, cl0, preferred_element_type=F32, precision=lax.Precision.HIGHEST).astype(BF16)
    pr = jnp.ones((1, sw), F32)
    pi = jnp.zeros((1, sw), F32)
    for i in range(blk - 1, -1, -1):
        rows = slice(i * tw, (i + 1) * tw)
        if i:
            w_ref[rows, 0:i * tw] = jnp.zeros((tw, i * tw), BF16)
        w_ref[rows, i * tw:blk * tw] = kt[:, 0:(blk - i) * tw]
        w_ref[rows, blk * tw:] = (pr * bb2 + pi * bb2s).astype(BF16)
        pr, pi = pr * lr - pi * li, pr * li + pi * lr
    sign = jnp.where(lax.broadcasted_iota(jnp.int32, (1, sw), 1) >= sw // 2, 1.0, -1.0)
    lam_ref[...] = jnp.zeros_like(lam_ref)
    lam_ref[0:1, :] = pr
    lam_ref[1:2, :] = sign * pi


def s5_prepare(a_re, a_im, b_re, b_im, c_re, c_im, log_step):
    n_g, npp = a_re.shape
    grp = b_re.shape[2]
    blk, gt = S5_BLK, S5_GT
    n_t = n_g // gt
    tw, sw = gt * grp, gt * 2 * npp
    f = lambda v: v.astype(F32)
    pair = lambda v: jnp.concatenate([f(v).reshape(n_t, gt * npp)] * 2, axis=-1)
    as_col = lambda v: jnp.broadcast_to(v.reshape(n_t, sw, 1), (n_t, sw, tw))
    as_row = lambda v: v.reshape(n_t, 1, sw)
    ls2 = pair(jnp.broadcast_to(f(log_step)[:, None], (n_g, npp)))
    c_t = lambda v: jnp.tile(f(v).transpose(0, 2, 1).reshape(n_t, gt * npp, grp), (1, 1, gt))
    ct = lambda v: jnp.concatenate([c_t(v), c_t(v)], axis=1)
    b_t = lambda v: jnp.broadcast_to(f(v).transpose(0, 2, 1).reshape(n_t, gt, grp, 1, npp),
                                     (n_t, gt, grp, gt, npp)).reshape(n_t, tw, gt * npp)
    btr, bti = b_t(b_re), b_t(b_im)
    bt = lambda lo, hi: jnp.concatenate([lo, hi], axis=-1)
    t3 = lambda shp: pl.BlockSpec((None,) + shp, lambda t: (t, 0, 0))
    return pl.pallas_call(
        _s5_prep_body,
        grid=(n_t,),
        in_specs=[t3((sw, tw)), t3((sw, tw)), t3((sw, tw)), t3((1, sw)), t3((1, sw)), t3((1, sw)),
                  t3((sw, tw)), t3((sw, tw)), t3((tw, sw)), t3((tw, sw))],
        out_specs=[t3((blk * tw, blk * tw + sw)), t3((sw, blk * tw)), t3((SUBLANES, sw))],
        out_shape=[jax.ShapeDtypeStruct((n_t, blk * tw, blk * tw + sw), BF16),
                   jax.ShapeDtypeStruct((n_t, sw, blk * tw), BF16),
                   jax.ShapeDtypeStruct((n_t, SUBLANES, sw), F32)],
        compiler_params=_cparams(("parallel",)),
        name="s5_prepare",
    )(as_col(pair(a_re)), as_col(pair(a_im)), as_col(ls2), as_row(pair(a_re)), as_row(pair(a_im)),
      as_row(ls2), ct(c_re), ct(c_im), bt(btr, bti), bt(-bti, btr))


def _s5_body(u_ref, w_ref, m_ref, lam_ref, d_ref, o_ref, *, rows):
    blk = S5_BLK
    tw = S5_GT * SSM_GROUP
    sw = lam_ref.shape[1]
    u_steps = [u_ref[pl.ds(i, rows, stride=blk), :] for i in range(blk)]
    lhs = jnp.concatenate([u.astype(BF16) for u in u_steps], axis=1)
    z = jnp.dot(lhs, w_ref[:, blk * tw:], preferred_element_type=F32)
    r = jnp.dot(lhs, w_ref[:, 0:blk * tw], preferred_element_type=F32)
    row = lax.broadcasted_iota(jnp.int32, (rows, sw), 0)
    ca = lam_ref[0:1, :]
    cb = lam_ref[1:2, :]
    s = 1
    while s < rows:
        zs = jnp.where(row >= s, pltpu.roll(z, s, 0), 0.0)
        z = z + ca * zs + cb * pltpu.roll(zs, sw // 2, 1)
        ca, cb = ca * ca - cb * cb, 2.0 * ca * cb
        s *= 2
    zprev = jnp.where(row >= 1, pltpu.roll(z, 1, 0), 0.0)
    ys = jnp.dot(zprev.astype(BF16), m_ref[...], preferred_element_type=F32)
    d = d_ref[...]
    for j in range(blk):
        cols = slice(j * tw, (j + 1) * tw)
        y = r[:, cols] + ys[:, cols] + d * u_steps[j]
        o_ref[pl.ds(j, rows, stride=blk), :] = jax.nn.gelu(y)


def s5_branch(u, w, mm, lam, d, *, batch):
    t_, width = u.shape
    seq = t_ // batch
    n_t, sw = w.shape[0], lam.shape[2]
    tw = S5_GT * SSM_GROUP
    return pl.pallas_call(
        functools.partial(_s5_body, rows=seq // S5_BLK),
        grid=(n_t, batch),
        in_specs=[
            pl.BlockSpec((seq, tw), lambda c, b: (b, c)),
            pl.BlockSpec((None,) + w.shape[1:], lambda c, b: (c, 0, 0)),
            pl.BlockSpec((None,) + mm.shape[1:], lambda c, b: (c, 0, 0)),
            pl.BlockSpec((None, SUBLANES, sw), lambda c, b: (c, 0, 0)),
            pl.BlockSpec((1, tw), lambda c, b: (0, c)),
        ],
        out_specs=pl.BlockSpec((seq, tw), lambda c, b: (b, c)),
        out_shape=jax.ShapeDtypeStruct((t_, width), F32),
        compiler_params=_cparams(("parallel", "arbitrary")),
        name="s5_blocked",
    )(u, w, mm, lam, d)


def _merge_body(ya_ref, yb_ref, yc_ref, w0_ref, w1_ref, w2_ref, wg_ref,
                g0_ref, g1_ref, g2_ref, gb_ref, o_ref):
    gate = lambda g_ref, i: _sigmoid(g_ref[...].astype(F32) + gb_ref[i:i + 1, :])
    g_a, g_b, g_c = gate(g0_ref, 0), gate(g1_ref, 1), gate(g2_ref, 2)
    dot = lambda a, w: jnp.dot(a, w[...], preferred_element_type=F32)
    yc = yc_ref[...].astype(BF16)
    br_a = dot(ya_ref[...], w0_ref)
    br_b = dot(yb_ref[...], w1_ref)
    br_c = dot(yc, w2_ref)
    glu = dot(yc, wg_ref)
    merged = g_a * br_a + g_b * br_b + g_c * (br_c * _sigmoid(glu))
    o_ref[...] = merged.astype(o_ref.dtype)


def branch_merge(ya, yb, yc, w_branch, w_glu, layer, proj, gate_col0, gate_bias, *, tm, tn):
    m, kdim = ya.shape
    n = w_glu.shape[2]
    gc0 = gate_col0 // tn
    nj = n // tn
    y_spec = pl.BlockSpec((tm, kdim), lambda i, j: (i, 0))
    wb_spec = lambda br: pl.BlockSpec((None, None, kdim, tn), lambda i, j: (layer, br, 0, j))
    g_spec = lambda br: pl.BlockSpec((tm, tn), lambda i, j: (i, gc0 + br * nj + j))
    return pl.pallas_call(
        _merge_body,
        grid=(m // tm, nj),
        in_specs=[y_spec, y_spec, y_spec, wb_spec(0), wb_spec(1), wb_spec(2),
                  pl.BlockSpec((None, kdim, tn), lambda i, j: (layer, 0, j)),
                  g_spec(0), g_spec(1), g_spec(2),
                  pl.BlockSpec((N_BRANCH, tn), lambda i, j: (0, j))],
        out_specs=pl.BlockSpec((tm, tn), lambda i, j: (i, j)),
        out_shape=jax.ShapeDtypeStruct((m, n), BF16),
        compiler_params=_cparams(("parallel", "arbitrary")),
        name="branch_merge",
    )(ya, yb, yc, w_branch, w_branch, w_branch, w_glu, proj, proj, proj, gate_bias.astype(F32))


def _matmul_res_body(a_ref, w_ref, r_ref, o_ref):
    o_ref[...] = r_ref[...] + jnp.dot(a_ref[...], w_ref[...], preferred_element_type=F32)


def matmul_residual(a, w, layer, res, *, tm, tn):
    m, k = a.shape
    n = w.shape[2]
    return pl.pallas_call(
        _matmul_res_body,
        grid=(m // tm, n // tn),
        in_specs=[pl.BlockSpec((tm, k), lambda i, j: (i, 0)),
                  pl.BlockSpec((None, k, tn), lambda i, j: (layer, 0, j)),
                  pl.BlockSpec((tm, tn), lambda i, j: (i, j))],
        out_specs=pl.BlockSpec((tm, tn), lambda i, j: (i, j)),
        out_shape=jax.ShapeDtypeStruct((m, n), F32),
        compiler_params=_cparams(("parallel", "arbitrary")),
        name="matmul_residual",
    )(a, w, res)


def _ffn_body(x_ref, g_ref, wg_ref, wu_ref, wd_ref, og_ref, o_ref, h_ref, *, norm_output):
    f = pl.program_id(1)

    @pl.when(f == 0)
    def _():
        x = x_ref[...]
        h_ref[...] = _rmsnorm_rows(x, g_ref[...]).astype(BF16)
        o_ref[...] = x

    h = h_ref[...]
    gate = jnp.dot(h, wg_ref[...], preferred_element_type=F32)
    up = jnp.dot(h, wu_ref[...], preferred_element_type=F32)
    act = (gate * _sigmoid(gate) * up).astype(BF16)
    o_ref[...] += jnp.dot(act, wd_ref[...], preferred_element_type=F32)

    if norm_output:
        @pl.when(f == pl.num_programs(1) - 1)
        def _():
            o_ref[...] = _rmsnorm_rows(o_ref[...], og_ref[...])


def ffn(x, g, w_gate, w_up, w_down, layer, out_g, *, norm_output, tm, tf):
    m, d = x.shape
    hidden = w_gate.shape[2]
    return pl.pallas_call(
        functools.partial(_ffn_body, norm_output=norm_output),
        grid=(m // tm, hidden // tf),
        in_specs=[pl.BlockSpec((tm, d), lambda i, f: (i, 0)),
                  pl.BlockSpec((1, d), lambda i, f: (0, 0)),
                  pl.BlockSpec((None, d, tf), lambda i, f: (layer, 0, f)),
                  pl.BlockSpec((None, d, tf), lambda i, f: (layer, 0, f)),
                  pl.BlockSpec((None, tf, d), lambda i, f: (layer, f, 0)),
                  pl.BlockSpec((1, d), lambda i, f: (0, 0))],
        out_specs=pl.BlockSpec((tm, d), lambda i, f: (i, 0)),
        out_shape=jax.ShapeDtypeStruct((m, d), F32),
        scratch_shapes=[pltpu.VMEM((tm, d), BF16)],
        compiler_params=_cparams(("parallel", "arbitrary")),
        name="swiglu_ffn",
    )(x, g.reshape(1, d), w_gate, w_up, w_down, out_g.reshape(1, d))


def _pick_tile(n, want):
    t = min(n, want)
    while n % t:
        t //= 2
    return t


def kernel(x, norm_mix_g, w_in, gate_bias, lru_conv_w, lru_conv_b, lru_wa, lru_ba, lru_wx, lru_bx,
           lru_lambda, attn_rel_bias, ssm_a_re, ssm_a_im, ssm_b_re, ssm_b_im, ssm_c_re, ssm_c_im,
           ssm_d, ssm_log_step, ssm_w_glu, w_branch, w_out, norm_ffn_g, w_ffn_gate, w_ffn_up,
           w_ffn_down, norm_final_g):
    b_, s_, d_model = x.shape
    depth = w_in.shape[0]
    mix_w = lru_conv_w.shape[2]
    n_heads = attn_rel_bias.shape[1]
    n_g = ssm_a_re.shape[1]
    t_ = b_ * s_
    assert s_ % ATT_QBLK == 0 and s_ % S5_BLK == 0 and n_g % S5_GT == 0
    c_lru_x, c_lru_g, c_q, c_k, c_v, c_u, c_gate = [i * mix_w for i in range(7)]
    in_w = w_in.shape[2]

    tm = _pick_tile(t_, 1024)
    ts_lru = _pick_tile(s_, 512)
    bias = rel_bias_matrix(attn_rel_bias.reshape(depth * n_heads, N_REL))
    w_in, w_branch, ssm_w_glu, w_out, w_ffn_gate, w_ffn_up, w_ffn_down = [
        w.astype(BF16) for w in (w_in, w_branch, ssm_w_glu, w_out, w_ffn_gate, w_ffn_up, w_ffn_down)]

    xf = x.reshape(t_, d_model).astype(F32)
    for l in range(depth):
        proj, u32 = norm_matmul(xf, norm_mix_g[l], w_in, l, tm=tm, tn=mix_w, f32_col0=c_u)
        proj3 = proj.reshape(b_, s_, in_w)
        y_a = lru_branch(proj3, c_lru_x, c_lru_g, lru_conv_w[l], lru_conv_b[l], lru_wa[l], lru_ba[l],
                         lru_wx[l], lru_bx[l], lru_lambda[l], ts=ts_lru)
        y_b = attention_branch(proj3, c_q, c_k, c_v, bias, l, n_heads)

        w1, mm, lam = s5_prepare(ssm_a_re[l], ssm_a_im[l], ssm_b_re[l], ssm_b_im[l],
                                 ssm_c_re[l], ssm_c_im[l], ssm_log_step[l])
        y_c = s5_branch(u32, w1, mm, lam, ssm_d[l].astype(F32).reshape(1, mix_w), batch=b_)

        merged = branch_merge(y_a.reshape(t_, mix_w), y_b.reshape(t_, mix_w), y_c, w_branch, ssm_w_glu, l,
                              proj, c_gate, gate_bias[l], tm=tm, tn=512)
        xf = matmul_residual(merged, w_out, l, xf, tm=tm, tn=1024)
        xf = ffn(xf, norm_ffn_g[l], w_ffn_gate, w_ffn_up, w_ffn_down, l, norm_final_g.astype(F32),
                 norm_output=(l == depth - 1), tm=_pick_tile(t_, 512), tf=512)
    return xf.reshape(b_, s_, d_model).astype(x.dtype)
```

```python
import functools
import math

import jax
import jax.numpy as jnp
from jax import lax
from jax.experimental import pallas as pl
from jax.experimental.pallas import tpu as pltpu

F32 = jnp.float32
BF16 = jnp.bfloat16

CHUNK = 64
LRU_BW = 64
CONV_W = 4
LRU_C = 8.0
ATT_HD = 128
ATT_LEFT_CHUNKS = 8
MAX_REL = 128
N_REL = 2 * MAX_REL + 1
SSM_GROUP = 16
SSM_P = 64
N_BRANCH = 3
NORM_EPS = 1e-6
MASK_VALUE = -1e30

LANES = 128
SUBLANES = 8
MXU_DIM = 256
VMEM_LIMIT_BYTES = 56 * 1024 * 1024

S5_BLK = 16
S5_GT = LANES // SSM_GROUP
LOG2_E = math.log2(math.e)
ATT_HEADS_PER_STEP = 8
ATT_QBLK = 4 * CHUNK
ATT_HALF = ATT_QBLK // 2
ATT_LEFT = ATT_LEFT_CHUNKS * CHUNK
ATT_HWIN = ATT_HALF + ATT_LEFT
LRU_CW = MXU_DIM
FFN_ACC_COLS = 2 * MXU_DIM


def _cparams(sem):
    return pltpu.CompilerParams(dimension_semantics=sem, vmem_limit_bytes=VMEM_LIMIT_BYTES)


def _rmsnorm_rows(x, g):
    ms = jnp.mean(x * x, axis=-1, keepdims=True)
    return x * lax.rsqrt(ms + NORM_EPS) * g


def _norm_matmul_body(x_ref, g_ref, w_ref, o_ref, o32_ref, h_ref, *, f32_block):
    j = pl.program_id(1)

    @pl.when(j == 0)
    def _():
        h_ref[...] = _rmsnorm_rows(x_ref[...], g_ref[...]).astype(BF16)

    acc = jnp.dot(h_ref[...], w_ref[...], preferred_element_type=F32)
    o_ref[...] = acc.astype(o_ref.dtype)

    @pl.when(j == f32_block)
    def _():
        o32_ref[...] = acc


def norm_matmul(x, g, w, layer, *, tm, tn, f32_col0):
    m, k = x.shape
    n = w.shape[2]
    return pl.pallas_call(
        functools.partial(_norm_matmul_body, f32_block=f32_col0 // tn),
        grid=(m // tm, n // tn),
        in_specs=[
            pl.BlockSpec((tm, k), lambda i, j: (i, 0)),
            pl.BlockSpec((1, k), lambda i, j: (0, 0)),
            pl.BlockSpec((None, k, tn), lambda i, j: (layer, 0, j)),
        ],
        out_specs=[pl.BlockSpec((tm, tn), lambda i, j: (i, j)),
                   pl.BlockSpec((tm, tn), lambda i, j: (i, 0))],
        out_shape=[jax.ShapeDtypeStruct((m, n), BF16), jax.ShapeDtypeStruct((m, tn), F32)],
        scratch_shapes=[pltpu.VMEM((tm, k), BF16)],
        compiler_params=_cparams(("parallel", "arbitrary")),
        name="norm_matmul",
    )(x, g.reshape(1, k), w)


def _softplus(x):
    return jnp.maximum(x, 0.0) + jnp.log1p(jnp.exp(-jnp.abs(x)))


def _sigmoid(x):
    return 0.5 * jnp.tanh(0.5 * x) + 0.5


def _lru_body(x_ref, gate_ref, cw_ref, cb_ref, wa_ref, wx_ref, ba_ref, bx_ref, lam_ref,
              o_ref, xbuf_ref, h_ref, *, ts):
    t = pl.program_id(2)

    @pl.when(t == 0)
    def _():
        xbuf_ref[0:SUBLANES, :] = jnp.zeros((SUBLANES, LRU_CW), F32)
        h_ref[...] = jnp.zeros_like(h_ref)

    x = x_ref[...].astype(F32)
    xbuf_ref[SUBLANES:SUBLANES + ts, :] = x
    xc = cb_ref[...] + x * cw_ref[CONV_W - 1:CONV_W, :]
    for k in range(CONV_W - 1):
        back = CONV_W - 1 - k
        xc = xc + xbuf_ref[pl.ds(SUBLANES - back, ts), :] * cw_ref[k:k + 1, :]
    xbuf_ref[0:SUBLANES, :] = x[ts - SUBLANES:ts, :]

    xcb = xc.astype(BF16)
    ta = jnp.tanh(jnp.dot(xcb, wa_ref[...], preferred_element_type=F32) + ba_ref[...])
    tx = jnp.tanh(jnp.dot(xcb, wx_ref[...], preferred_element_type=F32) + bx_ref[...])
    c1 = (-0.25 * LRU_C) * _softplus(-lam_ref[...])
    th = jnp.tanh(c1 * ta + c1)
    rinv = 1.0 / (1.0 - th)
    a = (1.0 + th) * rinv
    xh = 0.5 * xc
    b = (2.0 * rinv) * jnp.sqrt(-th) * (xh * tx + xh)

    nt = ts // SUBLANES
    a = a.reshape(nt, SUBLANES, LRU_CW)
    b = b.reshape(nt, SUBLANES, LRU_CW)
    sub = lax.broadcasted_iota(jnp.int32, (nt, SUBLANES, LRU_CW), 1)
    s = 1
    while s < SUBLANES:
        keep = sub >= s
        b_sh = pltpu.roll(b, s, 1)
        a_sh = pltpu.roll(a, s, 1)
        b = b + jnp.where(keep, a * b_sh, 0.0)
        a = jnp.where(keep, a * a_sh, a)
        s *= 2
    carry = h_ref[...]
    tiles = []
    for k in range(nt):
        hk = b[k] + a[k] * carry
        tiles.append(hk)
        carry = hk[SUBLANES - 1:SUBLANES, :]
    h_ref[...] = carry
    h = jnp.concatenate(tiles, axis=0)
    o_ref[...] = (h * jax.nn.gelu(gate_ref[...].astype(F32))).astype(o_ref.dtype)


def _block_diag_tiles(w):
    n_blk, bw, _ = w.shape
    per = LRU_CW // bw
    wt = w.reshape(n_blk // per, per, bw, bw)
    eye = jnp.eye(per, dtype=w.dtype)
    tiles = jnp.einsum("cpkj,pq->cpkqj", wt, eye)
    return tiles.reshape(n_blk // per, LRU_CW, LRU_CW)


def lru_branch(proj3, x_col0, gate_col0, conv_w, conv_b, wa, ba, wx, bx, lam, *, ts):
    b_, s_, _ = proj3.shape
    width = conv_w.shape[1]
    n_c = width // LRU_CW
    xc0 = x_col0 // LRU_CW
    gc0 = gate_col0 // LRU_CW
    row = lambda v: v.reshape(1, width).astype(F32)
    vec_spec = pl.BlockSpec((1, LRU_CW), lambda b, c, t: (0, c))
    return pl.pallas_call(
        functools.partial(_lru_body, ts=ts),
        grid=(b_, n_c, s_ // ts),
        in_specs=[
            pl.BlockSpec((None, ts, LRU_CW), lambda b, c, t: (b, t, xc0 + c)),
            pl.BlockSpec((None, ts, LRU_CW), lambda b, c, t: (b, t, gc0 + c)),
            pl.BlockSpec((CONV_W, LRU_CW), lambda b, c, t: (0, c)),
            vec_spec,
            pl.BlockSpec((None, LRU_CW, LRU_CW), lambda b, c, t: (c, 0, 0)),
            pl.BlockSpec((None, LRU_CW, LRU_CW), lambda b, c, t: (c, 0, 0)),
            vec_spec, vec_spec, vec_spec,
        ],
        out_specs=pl.BlockSpec((None, ts, LRU_CW), lambda b, c, t: (b, t, c)),
        out_shape=jax.ShapeDtypeStruct((b_, s_, width), BF16),
        scratch_shapes=[pltpu.VMEM((ts + SUBLANES, LRU_CW), F32), pltpu.VMEM((1, LRU_CW), F32)],
        compiler_params=_cparams(("parallel", "parallel", "arbitrary")),
        name="rg_lru",
    )(proj3, proj3, conv_w.astype(F32), row(conv_b), _block_diag_tiles(0.5 * wa).astype(BF16),
      _block_diag_tiles(0.5 * wx).astype(BF16), row(0.5 * ba), row(0.5 * bx), row(lam))


def _rel_bias_body(tab_ref, o_ref):
    h = pl.program_id(0)
    width = 1 << (ATT_HWIN + ATT_HALF - 1).bit_length()
    m = lax.broadcasted_iota(jnp.int32, (1, width), 1)
    delta = jnp.where(m < ATT_HWIN, m, m - width)
    idx = jnp.clip(ATT_LEFT - delta, -MAX_REL, MAX_REL) + MAX_REL

    def pick(d, acc):
        return jnp.where(idx == d, tab_ref[h, d], acc)

    gvec = lax.fori_loop(0, N_REL, pick, jnp.zeros((1, width), F32))
    full = pltpu.roll(jnp.broadcast_to(gvec, (ATT_HALF, width)), 0, 1, stride=1, stride_axis=0)
    chunk_shift = int(math.log2(CHUNK))
    qc = lax.shift_right_logical(lax.broadcasted_iota(jnp.int32, (ATT_HALF, ATT_HWIN), 0), chunk_shift)
    kc = lax.shift_right_logical(lax.broadcasted_iota(jnp.int32, (ATT_HALF, ATT_HWIN), 1), chunk_shift)
    in_band = (kc >= qc) & (kc <= qc + ATT_LEFT_CHUNKS)
    o_ref[...] = jnp.where(in_band, full[:, :ATT_HWIN] * LOG2_E, MASK_VALUE)


def rel_bias_matrix(rel_bias):
    n = rel_bias.shape[0]
    return pl.pallas_call(
        _rel_bias_body,
        grid=(n,),
        in_specs=[pl.BlockSpec(memory_space=pltpu.SMEM)],
        out_specs=pl.BlockSpec((None, ATT_HALF, ATT_HWIN), lambda h: (h, 0, 0)),
        out_shape=jax.ShapeDtypeStruct((n, ATT_HALF, ATT_HWIN), F32),
        compiler_params=_cparams(("arbitrary",)),
        name="rel_bias",
    )(rel_bias.astype(F32))


def _attn_body(q_ref, k0_ref, k1_ref, k2_ref, v0_ref, v1_ref, v2_ref, bias_ref, o_ref):
    qb = pl.program_id(2)
    nt = (((1,), (1,)), ((), ()))
    chunk_shift = int(math.log2(CHUNK))
    kc = lax.shift_right_logical(lax.broadcasted_iota(jnp.int32, (1, ATT_HWIN), 1), chunk_shift)
    k_refs = (k0_ref, k1_ref, k2_ref)
    v_refs = (v0_ref, v1_ref, v2_ref)
    units = [(r0, h) for r0 in range(0, ATT_QBLK, ATT_HALF) for h in range(ATT_HEADS_PER_STEP)]
    parts_of = lambda r0: ((0, r0, ATT_QBLK), (1, 0, ATT_QBLK), (2, 0, r0 + ATT_HALF))
    def score(r0, h):
        cols = slice(h * ATT_HD, (h + 1) * ATT_HD)
        first_chunk = (qb - 2) * (ATT_QBLK // CHUNK) + r0 // CHUNK
        before_start = jnp.where(kc + first_chunk >= 0, 0.0, MASK_VALUE)
        q = q_ref[r0:r0 + ATT_HALF, cols]
        s = jnp.concatenate(
            [lax.dot_general(q, k_refs[blk][a:b, cols], nt, preferred_element_type=F32)
             for blk, a, b in parts_of(r0)], axis=1)
        return s * (ATT_HD ** -0.5 * LOG2_E) + bias_ref[h] + before_start

    def softmax_terms(s):
        p = jnp.exp2(s - jnp.max(s, axis=-1, keepdims=True))
        return p.astype(BF16), jnp.sum(p, axis=-1, keepdims=True)

    def weighted_values(r0, h, pb, l):
        cols = slice(h * ATT_HD, (h + 1) * ATT_HD)
        o = jnp.zeros((ATT_HALF, ATT_HD), F32)
        c0 = 0
        for blk, a, b in parts_of(r0):
            o = o + jnp.dot(pb[:, c0:c0 + b - a], v_refs[blk][a:b, cols], preferred_element_type=F32)
            c0 += b - a
        o_ref[r0:r0 + ATT_HALF, cols] = (o / l).astype(o_ref.dtype)

    scores = [score(*u) for u in units]
    probs = [softmax_terms(s) for s in scores]
    for u, (pb, l) in zip(units, probs):
        weighted_values(*u, pb, l)


def attention_branch(proj3, q_col0, k_col0, v_col0, bias, layer, n_heads):
    b_, s_, _ = proj3.shape
    hp = ATT_HEADS_PER_STEP
    wblk = hp * ATT_HD
    qc0, kc0, vc0 = q_col0 // wblk, k_col0 // wblk, v_col0 // wblk

    def kv_spec(col0, back):
        return pl.BlockSpec((None, ATT_QBLK, wblk),
                            lambda h, b, i: (b, jnp.maximum(i - back, 0), col0 + h))

    return pl.pallas_call(
        _attn_body,
        grid=(n_heads // hp, b_, s_ // ATT_QBLK),
        in_specs=[
            pl.BlockSpec((None, ATT_QBLK, wblk), lambda h, b, i: (b, i, qc0 + h)),
            kv_spec(kc0, 2), kv_spec(kc0, 1), kv_spec(kc0, 0),
            kv_spec(vc0, 2), kv_spec(vc0, 1), kv_spec(vc0, 0),
            pl.BlockSpec((hp, ATT_HALF, ATT_HWIN), lambda h, b, i: (layer * (n_heads // hp) + h, 0, 0)),
        ],
        out_specs=pl.BlockSpec((None, ATT_QBLK, wblk), lambda h, b, i: (b, i, h)),
        out_shape=jax.ShapeDtypeStruct((b_, s_, n_heads * ATT_HD), BF16),
        compiler_params=_cparams(("parallel", "parallel", "parallel")),
        name="chunk_attention",
    )(proj3, proj3, proj3, proj3, proj3, proj3, proj3, bias)


def _s5_prep_body(are_c, aim_c, ls_c, are_r, aim_r, ls_r, ctre, ctim, bta, btb, w_ref, m_ref, lam_ref):
    blk = S5_BLK
    tw = S5_GT * SSM_GROUP
    sw = S5_GT * 2 * SSM_P
    lg_grp, lg_p = int(math.log2(SSM_GROUP)), int(math.log2(SSM_P))
    group_of_state = lambda q: lax.shift_right_logical(q, lg_p) & (S5_GT - 1)

    def lam_bar(are, aim, ls):
        step = jnp.exp(ls)
        mag = jnp.exp(are * step)
        return mag * jnp.cos(aim * step), mag * jnp.sin(aim * step)

    hw = sw // 2
    lr_c, li_c = lam_bar(are_c[...], aim_c[...], ls_c[...])
    rowc = lax.broadcasted_iota(jnp.int32, (hw, tw), 0)
    lanec = lax.broadcasted_iota(jnp.int32, (hw, tw), 1)
    same_c = group_of_state(rowc) == lax.shift_right_logical(lanec, lg_grp)
    cre = jnp.where(same_c, ctre[...], 0.0)
    cim = jnp.where(same_c, ctim[...], 0.0)
    pr = jnp.ones((hw, tw), F32)
    pi = jnp.zeros((hw, tw), F32)
    cl_re, cl_im = [], []
    for _ in range(blk + 1):
        cl_re.append(cre * pr - cim * pi)
        cl_im.append(-(cre * pi + cim * pr))
        pr, pi = pr * lr_c - pi * li_c, pr * li_c + pi * lr_c
    m_ref[0:hw, :] = jnp.concatenate(cl_re[1:], axis=1).astype(BF16)
    m_ref[hw:sw, :] = jnp.concatenate(cl_im[1:], axis=1).astype(BF16)
    cl0 = jnp.concatenate([jnp.concatenate(cl_re[:blk], axis=1),
                           jnp.concatenate(cl_im[:blk], axis=1)], axis=0)

    ar, ai = are_r[...], aim_r[...]
    lr, li = lam_bar(ar, ai, ls_r[...])
    den = ar * ar + ai * ai
    nr = lr - 1.0
    cr = (nr * ar + li * ai) / den
    ci = (li * ar - nr * ai) / den
    rowr = lax.broadcasted_iota(jnp.int32, (tw, sw), 0)
    laner = lax.broadcasted_iota(jnp.int32, (tw, sw), 1)
    same_r = lax.shift_right_logical(rowr, lg_grp) == group_of_state(laner)
    ba = jnp.where(same_r, bta[...], 0.0)
    bb = jnp.where(same_r, btb[...], 0.0)
    bb2 = cr * ba + ci * bb
    bb2s = cr * bb - ci * ba

    kt = jnp.dot(bb2, cl0, preferred_element_type=F32, precision=lax.Precision.HIGHEST).astype(BF16)
    pr = jnp.ones((1, sw), F32)
    pi = jnp.zeros((1, sw), F32)
    for i in range(blk - 1, -1, -1):
        rows = slice(i * tw, (i + 1) * tw)
        if i:
            w_ref[rows, 0:i * tw] = jnp.zeros((tw, i * tw), BF16)
        w_ref[rows, i * tw:blk * tw] = kt[:, 0:(blk - i) * tw]
        w_ref[rows, blk * tw:] = (pr * bb2 + pi * bb2s).astype(BF16)
        pr, pi = pr * lr - pi * li, pr * li + pi * lr
    sign = jnp.where(lax.broadcasted_iota(jnp.int32, (1, sw), 1) >= sw // 2, 1.0, -1.0)
    lam_ref[...] = jnp.zeros_like(lam_ref)
    lam_ref[0:1, :] = pr
    lam_ref[1:2, :] = sign * pi


def s5_prepare(a_re, a_im, b_re, b_im, c_re, c_im, log_step):
    n_g, npp = a_re.shape
    grp = b_re.shape[2]
    blk, gt = S5_BLK, S5_GT
    n_t = n_g // gt
    tw, sw = gt * grp, gt * 2 * npp
    f = lambda v: v.astype(F32)
    pair = lambda v: jnp.concatenate([f(v).reshape(n_t, gt * npp)] * 2, axis=-1)
    hw = sw // 2
    as_col = lambda v: jnp.broadcast_to(f(v).reshape(n_t, hw, 1), (n_t, hw, tw))
    as_row = lambda v: v.reshape(n_t, 1, sw)
    ls = jnp.broadcast_to(f(log_step)[:, None], (n_g, npp))
    ct = lambda v: jnp.tile(f(v).transpose(0, 2, 1).reshape(n_t, hw, grp), (1, 1, gt))
    b_t = lambda v: jnp.broadcast_to(f(v).transpose(0, 2, 1).reshape(n_t, gt, grp, 1, npp),
                                     (n_t, gt, grp, gt, npp)).reshape(n_t, tw, gt * npp)
    btr, bti = b_t(b_re), b_t(b_im)
    bt = lambda lo, hi: jnp.concatenate([lo, hi], axis=-1)
    t3 = lambda shp: pl.BlockSpec((None,) + shp, lambda t: (t, 0, 0))
    return pl.pallas_call(
        _s5_prep_body,
        grid=(n_t,),
        in_specs=[t3((hw, tw)), t3((hw, tw)), t3((hw, tw)), t3((1, sw)), t3((1, sw)), t3((1, sw)),
                  t3((hw, tw)), t3((hw, tw)), t3((tw, sw)), t3((tw, sw))],
        out_specs=[t3((blk * tw, blk * tw + sw)), t3((sw, blk * tw)), t3((SUBLANES, sw))],
        out_shape=[jax.ShapeDtypeStruct((n_t, blk * tw, blk * tw + sw), BF16),
                   jax.ShapeDtypeStruct((n_t, sw, blk * tw), BF16),
                   jax.ShapeDtypeStruct((n_t, SUBLANES, sw), F32)],
        compiler_params=_cparams(("parallel",)),
        name="s5_prepare",
    )(as_col(a_re), as_col(a_im), as_col(ls), as_row(pair(a_re)), as_row(pair(a_im)),
      as_row(pair(ls)), ct(c_re), ct(c_im), bt(btr, bti), bt(-bti, btr))


def _s5_body(u_ref, w_ref, m_ref, lam_ref, d_ref, o_ref, *, rows):
    blk = S5_BLK
    tw = S5_GT * SSM_GROUP
    sw = lam_ref.shape[1]
    u_steps = [u_ref[pl.ds(i, rows, stride=blk), :] for i in range(blk)]
    lhs = jnp.concatenate([u.astype(BF16) for u in u_steps], axis=1)
    z = jnp.dot(lhs, w_ref[:, blk * tw:], preferred_element_type=F32)
    r = jnp.dot(lhs, w_ref[:, 0:blk * tw], preferred_element_type=F32)
    row = lax.broadcasted_iota(jnp.int32, (rows, sw), 0)
    ca = lam_ref[0:1, :]
    cb = lam_ref[1:2, :]
    s = 1
    while s < rows:
        zs = jnp.where(row >= s, pltpu.roll(z, s, 0), 0.0)
        z = z + ca * zs + cb * pltpu.roll(zs, sw // 2, 1)
        ca, cb = ca * ca - cb * cb, 2.0 * ca * cb
        s *= 2
    zprev = jnp.where(row >= 1, pltpu.roll(z, 1, 0), 0.0)
    ys = jnp.dot(zprev.astype(BF16), m_ref[...], preferred_element_type=F32)
    d = d_ref[...]
    for j in range(blk):
        cols = slice(j * tw, (j + 1) * tw)
        y = r[:, cols] + ys[:, cols] + d * u_steps[j]
        o_ref[pl.ds(j, rows, stride=blk), :] = jax.nn.gelu(y)


def s5_branch(u, w, mm, lam, d, *, batch):
    t_, width = u.shape
    seq = t_ // batch
    n_t, sw = w.shape[0], lam.shape[2]
    tw = S5_GT * SSM_GROUP
    return pl.pallas_call(
        functools.partial(_s5_body, rows=seq // S5_BLK),
        grid=(n_t, batch),
        in_specs=[
            pl.BlockSpec((seq, tw), lambda c, b: (b, c)),
            pl.BlockSpec((None,) + w.shape[1:], lambda c, b: (c, 0, 0)),
            pl.BlockSpec((None,) + mm.shape[1:], lambda c, b: (c, 0, 0)),
            pl.BlockSpec((None, SUBLANES, sw), lambda c, b: (c, 0, 0)),
            pl.BlockSpec((1, tw), lambda c, b: (0, c)),
        ],
        out_specs=pl.BlockSpec((seq, tw), lambda c, b: (b, c)),
        out_shape=jax.ShapeDtypeStruct((t_, width), F32),
        compiler_params=_cparams(("parallel", "arbitrary")),
        name="s5_blocked",
    )(u, w, mm, lam, d)


def _merge_body(ya_ref, yb_ref, yc_ref, w0_ref, w1_ref, w2_ref, wg_ref,
                g0_ref, g1_ref, g2_ref, gb_ref, o_ref):
    gate = lambda g_ref, i: _sigmoid(g_ref[...].astype(F32) + gb_ref[i:i + 1, :])
    g_a, g_b, g_c = gate(g0_ref, 0), gate(g1_ref, 1), gate(g2_ref, 2)
    dot = lambda a, w: jnp.dot(a, w[...], preferred_element_type=F32)
    yc = yc_ref[...].astype(BF16)
    br_a = dot(ya_ref[...], w0_ref)
    br_b = dot(yb_ref[...], w1_ref)
    br_c = dot(yc, w2_ref)
    glu = dot(yc, wg_ref)
    merged = g_a * br_a + g_b * br_b + g_c * (br_c * _sigmoid(glu))
    o_ref[...] = merged.astype(o_ref.dtype)


def branch_merge(ya, yb, yc, w_branch, w_glu, layer, proj, gate_col0, gate_bias, *, tm, tn):
    m, kdim = ya.shape
    n = w_glu.shape[2]
    gc0 = gate_col0 // tn
    nj = n // tn
    y_spec = pl.BlockSpec((tm, kdim), lambda i, j: (i, 0))
    wb_spec = lambda br: pl.BlockSpec((None, None, kdim, tn), lambda i, j: (layer, br, 0, j))
    g_spec = lambda br: pl.BlockSpec((tm, tn), lambda i, j: (i, gc0 + br * nj + j))
    return pl.pallas_call(
        _merge_body,
        grid=(m // tm, nj),
        in_specs=[y_spec, y_spec, y_spec, wb_spec(0), wb_spec(1), wb_spec(2),
                  pl.BlockSpec((None, kdim, tn), lambda i, j: (layer, 0, j)),
                  g_spec(0), g_spec(1), g_spec(2),
                  pl.BlockSpec((N_BRANCH, tn), lambda i, j: (0, j))],
        out_specs=pl.BlockSpec((tm, tn), lambda i, j: (i, j)),
        out_shape=jax.ShapeDtypeStruct((m, n), BF16),
        compiler_params=_cparams(("parallel", "arbitrary")),
        name="branch_merge",
    )(ya, yb, yc, w_branch, w_branch, w_branch, w_glu, proj, proj, proj, gate_bias.astype(F32))


def _matmul_res_body(a_ref, w_ref, r_ref, o_ref):
    o_ref[...] = r_ref[...] + jnp.dot(a_ref[...], w_ref[...], preferred_element_type=F32)


def matmul_residual(a, w, layer, res, *, tm, tn):
    m, k = a.shape
    n = w.shape[2]
    return pl.pallas_call(
        _matmul_res_body,
        grid=(m // tm, n // tn),
        in_specs=[pl.BlockSpec((tm, k), lambda i, j: (i, 0)),
                  pl.BlockSpec((None, k, tn), lambda i, j: (layer, 0, j)),
                  pl.BlockSpec((tm, tn), lambda i, j: (i, j))],
        out_specs=pl.BlockSpec((tm, tn), lambda i, j: (i, j)),
        out_shape=jax.ShapeDtypeStruct((m, n), F32),
        compiler_params=_cparams(("parallel", "arbitrary")),
        name="matmul_residual",
    )(a, w, res)


def _ffn_body(x_ref, g_ref, wg_ref, wu_ref, wd_ref, og_ref, o_ref, h_ref, *, norm_output):
    f = pl.program_id(1)

    @pl.when(f == 0)
    def _():
        x = x_ref[...]
        h_ref[...] = _rmsnorm_rows(x, g_ref[...]).astype(BF16)
        o_ref[...] = x

    h = h_ref[...]
    gate = jnp.dot(h, wg_ref[...], preferred_element_type=F32)
    up = jnp.dot(h, wu_ref[...], preferred_element_type=F32)
    act = (gate * _sigmoid(gate) * up).astype(BF16)
    for c in range(0, o_ref.shape[1], FFN_ACC_COLS):
        cols = slice(c, c + FFN_ACC_COLS)
        o_ref[:, cols] += jnp.dot(act, wd_ref[:, cols], preferred_element_type=F32)

    if norm_output:
        @pl.when(f == pl.num_programs(1) - 1)
        def _():
            o_ref[...] = _rmsnorm_rows(o_ref[...], og_ref[...])


def ffn(x, g, w_gate, w_up, w_down, layer, out_g, *, norm_output, tm, tf):
    m, d = x.shape
    hidden = w_gate.shape[2]
    return pl.pallas_call(
        functools.partial(_ffn_body, norm_output=norm_output),
        grid=(m // tm, hidden // tf),
        in_specs=[pl.BlockSpec((tm, d), lambda i, f: (i, 0)),
                  pl.BlockSpec((1, d), lambda i, f: (0, 0)),
                  pl.BlockSpec((None, d, tf), lambda i, f: (layer, 0, f)),
                  pl.BlockSpec((None, d, tf), lambda i, f: (layer, 0, f)),
                  pl.BlockSpec((None, tf, d), lambda i, f: (layer, f, 0)),
                  pl.BlockSpec((1, d), lambda i, f: (0, 0))],
        out_specs=pl.BlockSpec((tm, d), lambda i, f: (i, 0)),
        out_shape=jax.ShapeDtypeStruct((m, d), F32),
        scratch_shapes=[pltpu.VMEM((tm, d), BF16)],
        compiler_params=_cparams(("parallel", "arbitrary")),
        name="swiglu_ffn",
    )(x, g.reshape(1, d), w_gate, w_up, w_down, out_g.reshape(1, d))


def _pick_tile(n, want):
    t = min(n, want)
    while n % t:
        t //= 2
    return t


def kernel(x, norm_mix_g, w_in, gate_bias, lru_conv_w, lru_conv_b, lru_wa, lru_ba, lru_wx, lru_bx,
           lru_lambda, attn_rel_bias, ssm_a_re, ssm_a_im, ssm_b_re, ssm_b_im, ssm_c_re, ssm_c_im,
           ssm_d, ssm_log_step, ssm_w_glu, w_branch, w_out, norm_ffn_g, w_ffn_gate, w_ffn_up,
           w_ffn_down, norm_final_g):
    b_, s_, d_model = x.shape
    depth = w_in.shape[0]
    mix_w = lru_conv_w.shape[2]
    n_heads = attn_rel_bias.shape[1]
    n_g = ssm_a_re.shape[1]
    t_ = b_ * s_
    assert s_ % ATT_QBLK == 0 and s_ % S5_BLK == 0 and n_g % S5_GT == 0
    c_lru_x, c_lru_g, c_q, c_k, c_v, c_u, c_gate = [i * mix_w for i in range(7)]
    in_w = w_in.shape[2]

    tm = _pick_tile(t_, 1024)
    ts_lru = _pick_tile(s_, 512)
    bias = rel_bias_matrix(attn_rel_bias.reshape(depth * n_heads, N_REL))
    w_in, w_branch, ssm_w_glu, w_out, w_ffn_gate, w_ffn_up, w_ffn_down = [
        w.astype(BF16) for w in (w_in, w_branch, ssm_w_glu, w_out, w_ffn_gate, w_ffn_up, w_ffn_down)]

    xf = x.reshape(t_, d_model).astype(F32)
    for l in range(depth):
        proj, u32 = norm_matmul(xf, norm_mix_g[l], w_in, l, tm=tm, tn=mix_w, f32_col0=c_u)
        proj3 = proj.reshape(b_, s_, in_w)
        y_a = lru_branch(proj3, c_lru_x, c_lru_g, lru_conv_w[l], lru_conv_b[l], lru_wa[l], lru_ba[l],
                         lru_wx[l], lru_bx[l], lru_lambda[l], ts=ts_lru)
        y_b = attention_branch(proj3, c_q, c_k, c_v, bias, l, n_heads)

        w1, mm, lam = s5_prepare(ssm_a_re[l], ssm_a_im[l], ssm_b_re[l], ssm_b_im[l],
                                 ssm_c_re[l], ssm_c_im[l], ssm_log_step[l])
        y_c = s5_branch(u32, w1, mm, lam, ssm_d[l].astype(F32).reshape(1, mix_w), batch=b_)

        merged = branch_merge(y_a.reshape(t_, mix_w), y_b.reshape(t_, mix_w), y_c, w_branch, ssm_w_glu, l,
                              proj, c_gate, gate_bias[l], tm=tm, tn=512)
        xf = matmul_residual(merged, w_out, l, xf, tm=_pick_tile(t_, 512), tn=d_model)
        xf = ffn(xf, norm_ffn_g[l], w_ffn_gate, w_ffn_up, w_ffn_down, l, norm_final_g.astype(F32),
                 norm_output=(l == depth - 1), tm=tm, tf=256)
    return xf.reshape(b_, s_, d_model).astype(x.dtype)
```

```python
import functools
import math

import jax
import jax.numpy as jnp
from jax import lax
from jax.experimental import pallas as pl
from jax.experimental.pallas import tpu as pltpu

F32 = jnp.float32
BF16 = jnp.bfloat16

CHUNK = 64
LRU_BW = 64
CONV_W = 4
LRU_C = 8.0
ATT_HD = 128
ATT_LEFT_CHUNKS = 8
MAX_REL = 128
N_REL = 2 * MAX_REL + 1
SSM_GROUP = 16
SSM_P = 64
N_BRANCH = 3
NORM_EPS = 1e-6
MASK_VALUE = -1e30

LANES = 128
SUBLANES = 8
MXU_DIM = 256
VMEM_LIMIT_BYTES = 56 * 1024 * 1024

S5_BLK = 16
S5_GT = LANES // SSM_GROUP
LOG2_E = math.log2(math.e)
ATT_HEADS_PER_STEP = 8
ATT_QBLK = 4 * CHUNK
ATT_HALF = ATT_QBLK // 2
ATT_LEFT = ATT_LEFT_CHUNKS * CHUNK
ATT_HWIN = ATT_HALF + ATT_LEFT
LRU_CW = MXU_DIM


def _cparams(sem):
    return pltpu.CompilerParams(dimension_semantics=sem, vmem_limit_bytes=VMEM_LIMIT_BYTES)


def _rmsnorm_rows(x, g):
    ms = jnp.mean(x * x, axis=-1, keepdims=True)
    return x * lax.rsqrt(ms + NORM_EPS) * g


def _norm_matmul_body(x_ref, g_ref, w_ref, o_ref, o32_ref, h_ref, *, f32_block):
    j = pl.program_id(1)

    @pl.when(j == 0)
    def _():
        h_ref[...] = _rmsnorm_rows(x_ref[...], g_ref[...]).astype(BF16)

    acc = jnp.dot(h_ref[...], w_ref[...], preferred_element_type=F32)
    o_ref[...] = acc.astype(o_ref.dtype)

    @pl.when(j == f32_block)
    def _():
        o32_ref[...] = acc


def norm_matmul(x, g, w, layer, *, tm, tn, f32_col0):
    m, k = x.shape
    n = w.shape[2]
    return pl.pallas_call(
        functools.partial(_norm_matmul_body, f32_block=f32_col0 // tn),
        grid=(m // tm, n // tn),
        in_specs=[
            pl.BlockSpec((tm, k), lambda i, j: (i, 0)),
            pl.BlockSpec((1, k), lambda i, j: (0, 0)),
            pl.BlockSpec((None, k, tn), lambda i, j: (layer, 0, j)),
        ],
        out_specs=[pl.BlockSpec((tm, tn), lambda i, j: (i, j)),
                   pl.BlockSpec((tm, tn), lambda i, j: (i, 0))],
        out_shape=[jax.ShapeDtypeStruct((m, n), BF16), jax.ShapeDtypeStruct((m, tn), F32)],
        scratch_shapes=[pltpu.VMEM((tm, k), BF16)],
        compiler_params=_cparams(("parallel", "arbitrary")),
        name="norm_matmul",
    )(x, g.reshape(1, k), w)


def _softplus(x):
    return jnp.maximum(x, 0.0) + jnp.log1p(jnp.exp(-jnp.abs(x)))


def _sigmoid(x):
    return 0.5 * jnp.tanh(0.5 * x) + 0.5


def _lru_body(x_ref, gate_ref, cw_ref, cb_ref, wa_ref, wx_ref, ba_ref, bx_ref, lam_ref,
              o_ref, xbuf_ref, h_ref, *, ts):
    t = pl.program_id(2)

    @pl.when(t == 0)
    def _():
        xbuf_ref[0:SUBLANES, :] = jnp.zeros((SUBLANES, LRU_CW), F32)
        h_ref[...] = jnp.zeros_like(h_ref)

    x = x_ref[...].astype(F32)
    xbuf_ref[SUBLANES:SUBLANES + ts, :] = x
    xc = cb_ref[...] + x * cw_ref[CONV_W - 1:CONV_W, :]
    for k in range(CONV_W - 1):
        back = CONV_W - 1 - k
        xc = xc + xbuf_ref[pl.ds(SUBLANES - back, ts), :] * cw_ref[k:k + 1, :]
    xbuf_ref[0:SUBLANES, :] = x[ts - SUBLANES:ts, :]

    xcb = xc.astype(BF16)
    ta = jnp.tanh(jnp.dot(xcb, wa_ref[...], preferred_element_type=F32) + ba_ref[...])
    tx = jnp.tanh(jnp.dot(xcb, wx_ref[...], preferred_element_type=F32) + bx_ref[...])
    c1 = (-0.25 * LRU_C) * _softplus(-lam_ref[...])
    th = jnp.tanh(c1 * ta + c1)
    rinv = 1.0 / (1.0 - th)
    a = (1.0 + th) * rinv
    xh = 0.5 * xc
    b = (2.0 * rinv) * jnp.sqrt(-th) * (xh * tx + xh)

    nt = ts // SUBLANES
    a = a.reshape(nt, SUBLANES, LRU_CW)
    b = b.reshape(nt, SUBLANES, LRU_CW)
    sub = lax.broadcasted_iota(jnp.int32, (nt, SUBLANES, LRU_CW), 1)
    s = 1
    while s < SUBLANES:
        keep = sub >= s
        b_sh = pltpu.roll(b, s, 1)
        a_sh = pltpu.roll(a, s, 1)
        b = b + jnp.where(keep, a * b_sh, 0.0)
        a = jnp.where(keep, a * a_sh, a)
        s *= 2
    carry = h_ref[...]
    tiles = []
    for k in range(nt):
        hk = b[k] + a[k] * carry
        tiles.append(hk)
        carry = hk[SUBLANES - 1:SUBLANES, :]
    h_ref[...] = carry
    h = jnp.concatenate(tiles, axis=0)
    o_ref[...] = (h * jax.nn.gelu(gate_ref[...].astype(F32))).astype(o_ref.dtype)


def _block_diag_tiles(w):
    n_blk, bw, _ = w.shape
    per = LRU_CW // bw
    wt = w.reshape(n_blk // per, per, bw, bw)
    eye = jnp.eye(per, dtype=w.dtype)
    tiles = jnp.einsum("cpkj,pq->cpkqj", wt, eye)
    return tiles.reshape(n_blk // per, LRU_CW, LRU_CW)


def lru_branch(proj3, x_col0, gate_col0, conv_w, conv_b, wa, ba, wx, bx, lam, *, ts):
    b_, s_, _ = proj3.shape
    width = conv_w.shape[1]
    n_c = width // LRU_CW
    xc0 = x_col0 // LRU_CW
    gc0 = gate_col0 // LRU_CW
    row = lambda v: v.reshape(1, width).astype(F32)
    vec_spec = pl.BlockSpec((1, LRU_CW), lambda b, c, t: (0, c))
    return pl.pallas_call(
        functools.partial(_lru_body, ts=ts),
        grid=(b_, n_c, s_ // ts),
        in_specs=[
            pl.BlockSpec((None, ts, LRU_CW), lambda b, c, t: (b, t, xc0 + c)),
            pl.BlockSpec((None, ts, LRU_CW), lambda b, c, t: (b, t, gc0 + c)),
            pl.BlockSpec((CONV_W, LRU_CW), lambda b, c, t: (0, c)),
            vec_spec,
            pl.BlockSpec((None, LRU_CW, LRU_CW), lambda b, c, t: (c, 0, 0)),
            pl.BlockSpec((None, LRU_CW, LRU_CW), lambda b, c, t: (c, 0, 0)),
            vec_spec, vec_spec, vec_spec,
        ],
        out_specs=pl.BlockSpec((None, ts, LRU_CW), lambda b, c, t: (b, t, c)),
        out_shape=jax.ShapeDtypeStruct((b_, s_, width), BF16),
        scratch_shapes=[pltpu.VMEM((ts + SUBLANES, LRU_CW), F32), pltpu.VMEM((1, LRU_CW), F32)],
        compiler_params=_cparams(("parallel", "parallel", "arbitrary")),
        name="rg_lru",
    )(proj3, proj3, conv_w.astype(F32), row(conv_b), _block_diag_tiles(0.5 * wa).astype(BF16),
      _block_diag_tiles(0.5 * wx).astype(BF16), row(0.5 * ba), row(0.5 * bx), row(lam))


def _rel_bias_body(tab_ref, o_ref):
    h = pl.program_id(0)
    width = 1 << (ATT_HWIN + ATT_HALF - 1).bit_length()
    m = lax.broadcasted_iota(jnp.int32, (1, width), 1)
    delta = jnp.where(m < ATT_HWIN, m, m - width)
    idx = jnp.clip(ATT_LEFT - delta, -MAX_REL, MAX_REL) + MAX_REL

    def pick(d, acc):
        return jnp.where(idx == d, tab_ref[h, d], acc)

    gvec = lax.fori_loop(0, N_REL, pick, jnp.zeros((1, width), F32), unroll=N_REL)
    full = pltpu.roll(jnp.broadcast_to(gvec, (ATT_HALF, width)), 0, 1, stride=1, stride_axis=0)
    chunk_shift = int(math.log2(CHUNK))
    qc = lax.shift_right_logical(lax.broadcasted_iota(jnp.int32, (ATT_HALF, ATT_HWIN), 0), chunk_shift)
    kc = lax.shift_right_logical(lax.broadcasted_iota(jnp.int32, (ATT_HALF, ATT_HWIN), 1), chunk_shift)
    in_band = (kc >= qc) & (kc <= qc + ATT_LEFT_CHUNKS)
    o_ref[...] = jnp.where(in_band, full[:, :ATT_HWIN] * LOG2_E, MASK_VALUE)


def rel_bias_matrix(rel_bias):
    n = rel_bias.shape[0]
    return pl.pallas_call(
        _rel_bias_body,
        grid=(n,),
        in_specs=[pl.BlockSpec(memory_space=pltpu.SMEM)],
        out_specs=pl.BlockSpec((None, ATT_HALF, ATT_HWIN), lambda h: (h, 0, 0)),
        out_shape=jax.ShapeDtypeStruct((n, ATT_HALF, ATT_HWIN), F32),
        compiler_params=_cparams(("arbitrary",)),
        name="rel_bias",
    )(rel_bias.astype(F32))


def _attn_body(q_ref, k0_ref, k1_ref, k2_ref, v0_ref, v1_ref, v2_ref, bias_ref, o_ref):
    qb = pl.program_id(2)
    nt = (((1,), (1,)), ((), ()))
    chunk_shift = int(math.log2(CHUNK))
    kc = lax.shift_right_logical(lax.broadcasted_iota(jnp.int32, (1, ATT_HWIN), 1), chunk_shift)
    k_refs = (k0_ref, k1_ref, k2_ref)
    v_refs = (v0_ref, v1_ref, v2_ref)
    units = [(r0, h) for r0 in range(0, ATT_QBLK, ATT_HALF) for h in range(ATT_HEADS_PER_STEP)]
    parts_of = lambda r0: ((0, r0, ATT_QBLK), (1, 0, ATT_QBLK), (2, 0, r0 + ATT_HALF))
    def score(r0, h):
        cols = slice(h * ATT_HD, (h + 1) * ATT_HD)
        first_chunk = (qb - 2) * (ATT_QBLK // CHUNK) + r0 // CHUNK
        before_start = jnp.where(kc + first_chunk >= 0, 0.0, MASK_VALUE)
        q = q_ref[r0:r0 + ATT_HALF, cols]
        s = jnp.concatenate(
            [lax.dot_general(q, k_refs[blk][a:b, cols], nt, preferred_element_type=F32)
             for blk, a, b in parts_of(r0)], axis=1)
        return s * (ATT_HD ** -0.5 * LOG2_E) + bias_ref[h] + before_start

    def softmax_terms(s):
        p = jnp.exp2(s - jnp.max(s, axis=-1, keepdims=True))
        return p.astype(BF16), jnp.sum(p, axis=-1, keepdims=True)

    def weighted_values(r0, h, pb, l):
        cols = slice(h * ATT_HD, (h + 1) * ATT_HD)
        o = jnp.zeros((ATT_HALF, ATT_HD), F32)
        c0 = 0
        for blk, a, b in parts_of(r0):
            o = o + jnp.dot(pb[:, c0:c0 + b - a], v_refs[blk][a:b, cols], preferred_element_type=F32)
            c0 += b - a
        o_ref[r0:r0 + ATT_HALF, cols] = (o / l).astype(o_ref.dtype)

    scores = [score(*u) for u in units]
    probs = [softmax_terms(s) for s in scores]
    for u, (pb, l) in zip(units, probs):
        weighted_values(*u, pb, l)


def attention_branch(proj3, q_col0, k_col0, v_col0, bias, layer, n_heads):
    b_, s_, _ = proj3.shape
    hp = ATT_HEADS_PER_STEP
    wblk = hp * ATT_HD
    qc0, kc0, vc0 = q_col0 // wblk, k_col0 // wblk, v_col0 // wblk

    def kv_spec(col0, back):
        return pl.BlockSpec((None, ATT_QBLK, wblk),
                            lambda h, b, i: (b, jnp.maximum(i - back, 0), col0 + h))

    return pl.pallas_call(
        _attn_body,
        grid=(n_heads // hp, b_, s_ // ATT_QBLK),
        in_specs=[
            pl.BlockSpec((None, ATT_QBLK, wblk), lambda h, b, i: (b, i, qc0 + h)),
            kv_spec(kc0, 2), kv_spec(kc0, 1), kv_spec(kc0, 0),
            kv_spec(vc0, 2), kv_spec(vc0, 1), kv_spec(vc0, 0),
            pl.BlockSpec((hp, ATT_HALF, ATT_HWIN), lambda h, b, i: (layer * (n_heads // hp) + h, 0, 0)),
        ],
        out_specs=pl.BlockSpec((None, ATT_QBLK, wblk), lambda h, b, i: (b, i, h)),
        out_shape=jax.ShapeDtypeStruct((b_, s_, n_heads * ATT_HD), BF16),
        compiler_params=_cparams(("parallel", "parallel", "parallel")),
        name="chunk_attention",
    )(proj3, proj3, proj3, proj3, proj3, proj3, proj3, bias)


def _s5_prep_body(are_c, aim_c, ls_c, are_r, aim_r, ls_r, ctre, ctim, bta, btb, w_ref, m_ref, lam_ref):
    blk = S5_BLK
    tw = S5_GT * SSM_GROUP
    sw = S5_GT * 2 * SSM_P
    lg_grp, lg_p = int(math.log2(SSM_GROUP)), int(math.log2(SSM_P))
    group_of_state = lambda q: lax.shift_right_logical(q, lg_p) & (S5_GT - 1)

    def lam_bar(are, aim, ls):
        step = jnp.exp(ls)
        mag = jnp.exp(are * step)
        return mag * jnp.cos(aim * step), mag * jnp.sin(aim * step)

    hw = sw // 2
    lr_c, li_c = lam_bar(are_c[...], aim_c[...], ls_c[...])
    rowc = lax.broadcasted_iota(jnp.int32, (hw, tw), 0)
    lanec = lax.broadcasted_iota(jnp.int32, (hw, tw), 1)
    same_c = group_of_state(rowc) == lax.shift_right_logical(lanec, lg_grp)
    cre = jnp.where(same_c, ctre[...], 0.0)
    cim = jnp.where(same_c, ctim[...], 0.0)
    pr = jnp.ones((hw, tw), F32)
    pi = jnp.zeros((hw, tw), F32)
    cl_re, cl_im = [], []
    for _ in range(blk + 1):
        cl_re.append(cre * pr - cim * pi)
        cl_im.append(-(cre * pi + cim * pr))
        pr, pi = pr * lr_c - pi * li_c, pr * li_c + pi * lr_c
    m_ref[0:hw, :] = jnp.concatenate(cl_re[1:], axis=1).astype(BF16)
    m_ref[hw:sw, :] = jnp.concatenate(cl_im[1:], axis=1).astype(BF16)
    cl0 = jnp.concatenate([jnp.concatenate(cl_re[:blk], axis=1),
                           jnp.concatenate(cl_im[:blk], axis=1)], axis=0)

    ar, ai = are_r[...], aim_r[...]
    lr, li = lam_bar(ar, ai, ls_r[...])
    den = ar * ar + ai * ai
    nr = lr - 1.0
    cr = (nr * ar + li * ai) / den
    ci = (li * ar - nr * ai) / den
    rowr = lax.broadcasted_iota(jnp.int32, (tw, sw), 0)
    laner = lax.broadcasted_iota(jnp.int32, (tw, sw), 1)
    same_r = lax.shift_right_logical(rowr, lg_grp) == group_of_state(laner)
    ba = jnp.where(same_r, bta[...], 0.0)
    bb = jnp.where(same_r, btb[...], 0.0)
    bb2 = cr * ba + ci * bb
    bb2s = cr * bb - ci * ba

    kt = jnp.dot(bb2, cl0, preferred_element_type=F32, precision=lax.Precision.HIGHEST).astype(BF16)
    pr = jnp.ones((1, sw), F32)
    pi = jnp.zeros((1, sw), F32)
    for i in range(blk - 1, -1, -1):
        rows = slice(i * tw, (i + 1) * tw)
        if i:
            w_ref[rows, 0:i * tw] = jnp.zeros((tw, i * tw), BF16)
        w_ref[rows, i * tw:blk * tw] = kt[:, 0:(blk - i) * tw]
        w_ref[rows, blk * tw:] = (pr * bb2 + pi * bb2s).astype(BF16)
        pr, pi = pr * lr - pi * li, pr * li + pi * lr
    sign = jnp.where(lax.broadcasted_iota(jnp.int32, (1, sw), 1) >= sw // 2, 1.0, -1.0)
    lam_ref[...] = jnp.zeros_like(lam_ref)
    lam_ref[0:1, :] = pr
    lam_ref[1:2, :] = sign * pi


def s5_prepare(a_re, a_im, b_re, b_im, c_re, c_im, log_step):
    n_g, npp = a_re.shape
    grp = b_re.shape[2]
    blk, gt = S5_BLK, S5_GT
    n_t = n_g // gt
    tw, sw = gt * grp, gt * 2 * npp
    f = lambda v: v.astype(F32)
    pair = lambda v: jnp.concatenate([f(v).reshape(n_t, gt * npp)] * 2, axis=-1)
    hw = sw // 2
    as_col = lambda v: jnp.broadcast_to(f(v).reshape(n_t, hw, 1), (n_t, hw, tw))
    as_row = lambda v: v.reshape(n_t, 1, sw)
    ls = jnp.broadcast_to(f(log_step)[:, None], (n_g, npp))
    ct = lambda v: jnp.tile(f(v).transpose(0, 2, 1).reshape(n_t, hw, grp), (1, 1, gt))
    b_t = lambda v: jnp.broadcast_to(f(v).transpose(0, 2, 1).reshape(n_t, gt, grp, 1, npp),
                                     (n_t, gt, grp, gt, npp)).reshape(n_t, tw, gt * npp)
    btr, bti = b_t(b_re), b_t(b_im)
    bt = lambda lo, hi: jnp.concatenate([lo, hi], axis=-1)
    t3 = lambda shp: pl.BlockSpec((None,) + shp, lambda t: (t, 0, 0))
    return pl.pallas_call(
        _s5_prep_body,
        grid=(n_t,),
        in_specs=[t3((hw, tw)), t3((hw, tw)), t3((hw, tw)), t3((1, sw)), t3((1, sw)), t3((1, sw)),
                  t3((hw, tw)), t3((hw, tw)), t3((tw, sw)), t3((tw, sw))],
        out_specs=[t3((blk * tw, blk * tw + sw)), t3((sw, blk * tw)), t3((SUBLANES, sw))],
        out_shape=[jax.ShapeDtypeStruct((n_t, blk * tw, blk * tw + sw), BF16),
                   jax.ShapeDtypeStruct((n_t, sw, blk * tw), BF16),
                   jax.ShapeDtypeStruct((n_t, SUBLANES, sw), F32)],
        compiler_params=_cparams(("parallel",)),
        name="s5_prepare",
    )(as_col(a_re), as_col(a_im), as_col(ls), as_row(pair(a_re)), as_row(pair(a_im)),
      as_row(pair(ls)), ct(c_re), ct(c_im), bt(btr, bti), bt(-bti, btr))


def _s5_body(u_ref, w_ref, m_ref, lam_ref, d_ref, o_ref, *, rows):
    blk = S5_BLK
    tw = S5_GT * SSM_GROUP
    sw = lam_ref.shape[1]
    u_steps = [u_ref[pl.ds(i, rows, stride=blk), :] for i in range(blk)]
    lhs = jnp.concatenate([u.astype(BF16) for u in u_steps], axis=1)
    z = jnp.dot(lhs, w_ref[:, blk * tw:], preferred_element_type=F32)
    r = jnp.dot(lhs, w_ref[:, 0:blk * tw], preferred_element_type=F32)
    row = lax.broadcasted_iota(jnp.int32, (rows, sw), 0)
    ca = lam_ref[0:1, :]
    cb = lam_ref[1:2, :]
    s = 1
    while s < rows:
        zs = jnp.where(row >= s, pltpu.roll(z, s, 0), 0.0)
        z = z + ca * zs + cb * pltpu.roll(zs, sw // 2, 1)
        ca, cb = ca * ca - cb * cb, 2.0 * ca * cb
        s *= 2
    zprev = jnp.where(row >= 1, pltpu.roll(z, 1, 0), 0.0)
    ys = jnp.dot(zprev.astype(BF16), m_ref[...], preferred_element_type=F32)
    d = d_ref[...]
    for j in range(blk):
        cols = slice(j * tw, (j + 1) * tw)
        y = r[:, cols] + ys[:, cols] + d * u_steps[j]
        o_ref[pl.ds(j, rows, stride=blk), :] = jax.nn.gelu(y)


def s5_branch(u, w, mm, lam, d, *, batch):
    t_, width = u.shape
    seq = t_ // batch
    n_t, sw = w.shape[0], lam.shape[2]
    tw = S5_GT * SSM_GROUP
    return pl.pallas_call(
        functools.partial(_s5_body, rows=seq // S5_BLK),
        grid=(n_t, batch),
        in_specs=[
            pl.BlockSpec((seq, tw), lambda c, b: (b, c)),
            pl.BlockSpec((None,) + w.shape[1:], lambda c, b: (c, 0, 0)),
            pl.BlockSpec((None,) + mm.shape[1:], lambda c, b: (c, 0, 0)),
            pl.BlockSpec((None, SUBLANES, sw), lambda c, b: (c, 0, 0)),
            pl.BlockSpec((1, tw), lambda c, b: (0, c)),
        ],
        out_specs=pl.BlockSpec((seq, tw), lambda c, b: (b, c)),
        out_shape=jax.ShapeDtypeStruct((t_, width), F32),
        compiler_params=_cparams(("parallel", "arbitrary")),
        name="s5_blocked",
    )(u, w, mm, lam, d)


def _merge_body(ya_ref, yb_ref, yc_ref, w0_ref, w1_ref, w2_ref, wg_ref,
                g0_ref, g1_ref, g2_ref, gb_ref, o_ref):
    gate = lambda g_ref, i: _sigmoid(g_ref[...].astype(F32) + gb_ref[i:i + 1, :])
    g_a, g_b, g_c = gate(g0_ref, 0), gate(g1_ref, 1), gate(g2_ref, 2)
    dot = lambda a, w: jnp.dot(a, w[...], preferred_element_type=F32)
    yc = yc_ref[...].astype(BF16)
    br_a = dot(ya_ref[...], w0_ref)
    br_b = dot(yb_ref[...], w1_ref)
    br_c = dot(yc, w2_ref)
    glu = dot(yc, wg_ref)
    merged = g_a * br_a + g_b * br_b + g_c * (br_c * _sigmoid(glu))
    o_ref[...] = merged.astype(o_ref.dtype)


def branch_merge(ya, yb, yc, w_branch, w_glu, layer, proj, gate_col0, gate_bias, *, tm, tn):
    m, kdim = ya.shape
    n = w_glu.shape[2]
    gc0 = gate_col0 // tn
    nj = n // tn
    y_spec = pl.BlockSpec((tm, kdim), lambda i, j: (i, 0))
    wb_spec = lambda br: pl.BlockSpec((None, None, kdim, tn), lambda i, j: (layer, br, 0, j))
    g_spec = lambda br: pl.BlockSpec((tm, tn), lambda i, j: (i, gc0 + br * nj + j))
    return pl.pallas_call(
        _merge_body,
        grid=(m // tm, nj),
        in_specs=[y_spec, y_spec, y_spec, wb_spec(0), wb_spec(1), wb_spec(2),
                  pl.BlockSpec((None, kdim, tn), lambda i, j: (layer, 0, j)),
                  g_spec(0), g_spec(1), g_spec(2),
                  pl.BlockSpec((N_BRANCH, tn), lambda i, j: (0, j))],
        out_specs=pl.BlockSpec((tm, tn), lambda i, j: (i, j)),
        out_shape=jax.ShapeDtypeStruct((m, n), BF16),
        compiler_params=_cparams(("parallel", "arbitrary")),
        name="branch_merge",
    )(ya, yb, yc, w_branch, w_branch, w_branch, w_glu, proj, proj, proj, gate_bias.astype(F32))


def _matmul_res_body(a_ref, w_ref, r_ref, o_ref):
    o_ref[...] = r_ref[...] + jnp.dot(a_ref[...], w_ref[...], preferred_element_type=F32)


def matmul_residual(a, w, layer, res, *, tm, tn):
    m, k = a.shape
    n = w.shape[2]
    return pl.pallas_call(
        _matmul_res_body,
        grid=(m // tm, n // tn),
        in_specs=[pl.BlockSpec((tm, k), lambda i, j: (i, 0)),
                  pl.BlockSpec((None, k, tn), lambda i, j: (layer, 0, j)),
                  pl.BlockSpec((tm, tn), lambda i, j: (i, j))],
        out_specs=pl.BlockSpec((tm, tn), lambda i, j: (i, j)),
        out_shape=jax.ShapeDtypeStruct((m, n), F32),
        compiler_params=_cparams(("parallel", "arbitrary")),
        name="matmul_residual",
    )(a, w, res)


def _ffn_body(x_ref, g_ref, wg_ref, wu_ref, wd_ref, og_ref, o_ref, h_ref, *, norm_output):
    f = pl.program_id(1)

    @pl.when(f == 0)
    def _():
        x = x_ref[...]
        h_ref[...] = _rmsnorm_rows(x, g_ref[...]).astype(BF16)
        o_ref[...] = x

    h = h_ref[...]
    gate = jnp.dot(h, wg_ref[...], preferred_element_type=F32)
    up = jnp.dot(h, wu_ref[...], preferred_element_type=F32)
    act = (gate * _sigmoid(gate) * up).astype(BF16)
    o_ref[...] += jnp.dot(act, wd_ref[...], preferred_element_type=F32)

    if norm_output:
        @pl.when(f == pl.num_programs(1) - 1)
        def _():
            o_ref[...] = _rmsnorm_rows(o_ref[...], og_ref[...])


def ffn(x, g, w_gate, w_up, w_down, layer, out_g, *, norm_output, tm, tf):
    m, d = x.shape
    hidden = w_gate.shape[2]
    return pl.pallas_call(
        functools.partial(_ffn_body, norm_output=norm_output),
        grid=(m // tm, hidden // tf),
        in_specs=[pl.BlockSpec((tm, d), lambda i, f: (i, 0)),
                  pl.BlockSpec((1, d), lambda i, f: (0, 0)),
                  pl.BlockSpec((None, d, tf), lambda i, f: (layer, 0, f)),
                  pl.BlockSpec((None, d, tf), lambda i, f: (layer, 0, f)),
                  pl.BlockSpec((None, tf, d), lambda i, f: (layer, f, 0)),
                  pl.BlockSpec((1, d), lambda i, f: (0, 0))],
        out_specs=pl.BlockSpec((tm, d), lambda i, f: (i, 0)),
        out_shape=jax.ShapeDtypeStruct((m, d), F32),
        scratch_shapes=[pltpu.VMEM((tm, d), BF16)],
        compiler_params=_cparams(("parallel", "arbitrary")),
        name="swiglu_ffn",
    )(x, g.reshape(1, d), w_gate, w_up, w_down, out_g.reshape(1, d))


def _pick_tile(n, want):
    t = min(n, want)
    while n % t:
        t //= 2
    return t


def kernel(x, norm_mix_g, w_in, gate_bias, lru_conv_w, lru_conv_b, lru_wa, lru_ba, lru_wx, lru_bx,
           lru_lambda, attn_rel_bias, ssm_a_re, ssm_a_im, ssm_b_re, ssm_b_im, ssm_c_re, ssm_c_im,
           ssm_d, ssm_log_step, ssm_w_glu, w_branch, w_out, norm_ffn_g, w_ffn_gate, w_ffn_up,
           w_ffn_down, norm_final_g):
    b_, s_, d_model = x.shape
    depth = w_in.shape[0]
    mix_w = lru_conv_w.shape[2]
    n_heads = attn_rel_bias.shape[1]
    n_g = ssm_a_re.shape[1]
    t_ = b_ * s_
    assert s_ % ATT_QBLK == 0 and s_ % S5_BLK == 0 and n_g % S5_GT == 0
    c_lru_x, c_lru_g, c_q, c_k, c_v, c_u, c_gate = [i * mix_w for i in range(7)]
    in_w = w_in.shape[2]

    tm = _pick_tile(t_, 1024)
    ts_lru = _pick_tile(s_, 1024)
    bias = rel_bias_matrix(attn_rel_bias.reshape(depth * n_heads, N_REL))
    w_in, w_branch, ssm_w_glu, w_out, w_ffn_gate, w_ffn_up, w_ffn_down = [
        w.astype(BF16) for w in (w_in, w_branch, ssm_w_glu, w_out, w_ffn_gate, w_ffn_up, w_ffn_down)]

    xf = x.reshape(t_, d_model).astype(F32)
    for l in range(depth):
        proj, u32 = norm_matmul(xf, norm_mix_g[l], w_in, l, tm=tm, tn=mix_w, f32_col0=c_u)
        proj3 = proj.reshape(b_, s_, in_w)
        y_a = lru_branch(proj3, c_lru_x, c_lru_g, lru_conv_w[l], lru_conv_b[l], lru_wa[l], lru_ba[l],
                         lru_wx[l], lru_bx[l], lru_lambda[l], ts=ts_lru)
        y_b = attention_branch(proj3, c_q, c_k, c_v, bias, l, n_heads)

        w1, mm, lam = s5_prepare(ssm_a_re[l], ssm_a_im[l], ssm_b_re[l], ssm_b_im[l],
                                 ssm_c_re[l], ssm_c_im[l], ssm_log_step[l])
        y_c = s5_branch(u32, w1, mm, lam, ssm_d[l].astype(F32).reshape(1, mix_w), batch=b_)

        merged = branch_merge(y_a.reshape(t_, mix_w), y_b.reshape(t_, mix_w), y_c, w_branch, ssm_w_glu, l,
                              proj, c_gate, gate_bias[l], tm=tm, tn=512)
        xf = matmul_residual(merged, w_out, l, xf, tm=_pick_tile(t_, 512), tn=d_model)
        xf = ffn(xf, norm_ffn_g[l], w_ffn_gate, w_ffn_up, w_ffn_down, l, norm_final_g.astype(F32),
                 norm_output=(l == depth - 1), tm=_pick_tile(t_, 512), tf=512)
    return xf.reshape(b_, s_, d_model).astype(x.dtype)
```

```python
import functools
import math

import jax
import jax.numpy as jnp
from jax import lax
from jax.experimental import pallas as pl
from jax.experimental.pallas import tpu as pltpu

F32 = jnp.float32
BF16 = jnp.bfloat16

CHUNK = 64
LRU_BW = 64
CONV_W = 4
LRU_C = 8.0
ATT_HD = 128
ATT_LEFT_CHUNKS = 8
MAX_REL = 128
N_REL = 2 * MAX_REL + 1
SSM_GROUP = 16
SSM_P = 64
N_BRANCH = 3
NORM_EPS = 1e-6
MASK_VALUE = -1e30

LANES = 128
SUBLANES = 8
MXU_DIM = 256
VMEM_LIMIT_BYTES = 56 * 1024 * 1024

S5_BLK = 16
S5_GT = LANES // SSM_GROUP
LOG2_E = math.log2(math.e)
ATT_HEADS_PER_STEP = 8
ATT_QBLK = 4 * CHUNK
ATT_HALF = ATT_QBLK // 2
ATT_LEFT = ATT_LEFT_CHUNKS * CHUNK
ATT_HWIN = ATT_HALF + ATT_LEFT
LRU_CW = MXU_DIM


def _cparams(sem):
    return pltpu.CompilerParams(dimension_semantics=sem, vmem_limit_bytes=VMEM_LIMIT_BYTES)


def _rmsnorm_rows(x, g):
    ms = jnp.mean(x * x, axis=-1, keepdims=True)
    return x * lax.rsqrt(ms + NORM_EPS) * g


def _norm_matmul_body(x_ref, g_ref, w_ref, o_ref, o32_ref, h_ref, *, f32_block):
    j = pl.program_id(1)

    @pl.when(j == 0)
    def _():
        h_ref[...] = _rmsnorm_rows(x_ref[...], g_ref[...]).astype(BF16)

    acc = jnp.dot(h_ref[...], w_ref[...], preferred_element_type=F32)
    o_ref[...] = acc.astype(o_ref.dtype)

    @pl.when(j == f32_block)
    def _():
        o32_ref[...] = acc


def norm_matmul(x, g, w, layer, *, tm, tn, f32_col0):
    m, k = x.shape
    n = w.shape[2]
    return pl.pallas_call(
        functools.partial(_norm_matmul_body, f32_block=f32_col0 // tn),
        grid=(m // tm, n // tn),
        in_specs=[
            pl.BlockSpec((tm, k), lambda i, j: (i, 0)),
            pl.BlockSpec((1, k), lambda i, j: (0, 0)),
            pl.BlockSpec((None, k, tn), lambda i, j: (layer, 0, j)),
        ],
        out_specs=[pl.BlockSpec((tm, tn), lambda i, j: (i, j)),
                   pl.BlockSpec((tm, tn), lambda i, j: (i, 0))],
        out_shape=[jax.ShapeDtypeStruct((m, n), BF16), jax.ShapeDtypeStruct((m, tn), F32)],
        scratch_shapes=[pltpu.VMEM((tm, k), BF16)],
        compiler_params=_cparams(("parallel", "arbitrary")),
        name="norm_matmul",
    )(x, g.reshape(1, k), w)


def _softplus(x):
    return jnp.maximum(x, 0.0) + jnp.log1p(jnp.exp(-jnp.abs(x)))


def _sigmoid(x):
    return 0.5 * jnp.tanh(0.5 * x) + 0.5


def _lru_body(x_ref, gate_ref, cw_ref, cb_ref, wa_ref, wx_ref, ba_ref, bx_ref, lam_ref,
              o_ref, xbuf_ref, h_ref, *, ts):
    t = pl.program_id(2)

    @pl.when(t == 0)
    def _():
        xbuf_ref[0:SUBLANES, :] = jnp.zeros((SUBLANES, LRU_CW), F32)
        h_ref[...] = jnp.zeros_like(h_ref)

    x = x_ref[...].astype(F32)
    xbuf_ref[SUBLANES:SUBLANES + ts, :] = x
    xc = cb_ref[...] + x * cw_ref[CONV_W - 1:CONV_W, :]
    for k in range(CONV_W - 1):
        back = CONV_W - 1 - k
        xc = xc + xbuf_ref[pl.ds(SUBLANES - back, ts), :] * cw_ref[k:k + 1, :]
    xbuf_ref[0:SUBLANES, :] = x[ts - SUBLANES:ts, :]

    xcb = xc.astype(BF16)
    ta = jnp.tanh(jnp.dot(xcb, wa_ref[...], preferred_element_type=F32) + ba_ref[...])
    tx = jnp.tanh(jnp.dot(xcb, wx_ref[...], preferred_element_type=F32) + bx_ref[...])
    c1 = (-0.25 * LRU_C) * _softplus(-lam_ref[...])
    th = jnp.tanh(c1 * ta + c1)
    rinv = 1.0 / (1.0 - th)
    a = (1.0 + th) * rinv
    xh = 0.5 * xc
    b = (2.0 * rinv) * jnp.sqrt(-th) * (xh * tx + xh)

    nt = ts // SUBLANES
    a = a.reshape(nt, SUBLANES, LRU_CW)
    b = b.reshape(nt, SUBLANES, LRU_CW)
    sub = lax.broadcasted_iota(jnp.int32, (nt, SUBLANES, LRU_CW), 1)
    s = 1
    while s < SUBLANES:
        keep = sub >= s
        b_sh = pltpu.roll(b, s, 1)
        a_sh = pltpu.roll(a, s, 1)
        b = b + jnp.where(keep, a * b_sh, 0.0)
        a = jnp.where(keep, a * a_sh, a)
        s *= 2
    carry = h_ref[...]
    tiles = []
    for k in range(nt):
        hk = b[k] + a[k] * carry
        tiles.append(hk)
        carry = hk[SUBLANES - 1:SUBLANES, :]
    h_ref[...] = carry
    h = jnp.concatenate(tiles, axis=0)
    o_ref[...] = (h * jax.nn.gelu(gate_ref[...].astype(F32))).astype(o_ref.dtype)


def _block_diag_tiles(w):
    n_blk, bw, _ = w.shape
    per = LRU_CW // bw
    wt = w.reshape(n_blk // per, per, bw, bw)
    eye = jnp.eye(per, dtype=w.dtype)
    tiles = jnp.einsum("cpkj,pq->cpkqj", wt, eye)
    return tiles.reshape(n_blk // per, LRU_CW, LRU_CW)


def lru_branch(proj3, x_col0, gate_col0, conv_w, conv_b, wa, ba, wx, bx, lam, *, ts):
    b_, s_, _ = proj3.shape
    width = conv_w.shape[1]
    n_c = width // LRU_CW
    xc0 = x_col0 // LRU_CW
    gc0 = gate_col0 // LRU_CW
    row = lambda v: v.reshape(1, width).astype(F32)
    vec_spec = pl.BlockSpec((1, LRU_CW), lambda b, c, t: (0, c))
    return pl.pallas_call(
        functools.partial(_lru_body, ts=ts),
        grid=(b_, n_c, s_ // ts),
        in_specs=[
            pl.BlockSpec((None, ts, LRU_CW), lambda b, c, t: (b, t, xc0 + c)),
            pl.BlockSpec((None, ts, LRU_CW), lambda b, c, t: (b, t, gc0 + c)),
            pl.BlockSpec((CONV_W, LRU_CW), lambda b, c, t: (0, c)),
            vec_spec,
            pl.BlockSpec((None, LRU_CW, LRU_CW), lambda b, c, t: (c, 0, 0)),
            pl.BlockSpec((None, LRU_CW, LRU_CW), lambda b, c, t: (c, 0, 0)),
            vec_spec, vec_spec, vec_spec,
        ],
        out_specs=pl.BlockSpec((None, ts, LRU_CW), lambda b, c, t: (b, t, c)),
        out_shape=jax.ShapeDtypeStruct((b_, s_, width), BF16),
        scratch_shapes=[pltpu.VMEM((ts + SUBLANES, LRU_CW), F32), pltpu.VMEM((1, LRU_CW), F32)],
        compiler_params=_cparams(("parallel", "parallel", "arbitrary")),
        name="rg_lru",
    )(proj3, proj3, conv_w.astype(F32), row(conv_b), _block_diag_tiles(0.5 * wa).astype(BF16),
      _block_diag_tiles(0.5 * wx).astype(BF16), row(0.5 * ba), row(0.5 * bx), row(lam))


def _rel_bias_body(tab_ref, o_ref):
    h = pl.program_id(0)
    width = 1 << (ATT_HWIN + ATT_HALF - 1).bit_length()
    m = lax.broadcasted_iota(jnp.int32, (1, width), 1)
    delta = jnp.where(m < ATT_HWIN, m, m - width)
    idx = jnp.clip(ATT_LEFT - delta, -MAX_REL, MAX_REL) + MAX_REL

    def pick(d, acc):
        return jnp.where(idx == d, tab_ref[h, d], acc)

    gvec = lax.fori_loop(0, N_REL, pick, jnp.zeros((1, width), F32), unroll=N_REL)
    full = pltpu.roll(jnp.broadcast_to(gvec, (ATT_HALF, width)), 0, 1, stride=1, stride_axis=0)
    chunk_shift = int(math.log2(CHUNK))
    qc = lax.shift_right_logical(lax.broadcasted_iota(jnp.int32, (ATT_HALF, ATT_HWIN), 0), chunk_shift)
    kc = lax.shift_right_logical(lax.broadcasted_iota(jnp.int32, (ATT_HALF, ATT_HWIN), 1), chunk_shift)
    in_band = (kc >= qc) & (kc <= qc + ATT_LEFT_CHUNKS)
    o_ref[...] = jnp.where(in_band, full[:, :ATT_HWIN] * LOG2_E, MASK_VALUE)


def rel_bias_matrix(rel_bias):
    n = rel_bias.shape[0]
    return pl.pallas_call(
        _rel_bias_body,
        grid=(n,),
        in_specs=[pl.BlockSpec(memory_space=pltpu.SMEM)],
        out_specs=pl.BlockSpec((None, ATT_HALF, ATT_HWIN), lambda h: (h, 0, 0)),
        out_shape=jax.ShapeDtypeStruct((n, ATT_HALF, ATT_HWIN), F32),
        compiler_params=_cparams(("arbitrary",)),
        name="rel_bias",
    )(rel_bias.astype(F32))


def _attn_body(q_ref, k0_ref, k1_ref, k2_ref, v0_ref, v1_ref, v2_ref, bias_ref, o_ref):
    qb = pl.program_id(2)
    nt = (((1,), (1,)), ((), ()))
    chunk_shift = int(math.log2(CHUNK))
    kc = lax.shift_right_logical(lax.broadcasted_iota(jnp.int32, (1, ATT_HWIN), 1), chunk_shift)
    k_refs = (k0_ref, k1_ref, k2_ref)
    v_refs = (v0_ref, v1_ref, v2_ref)
    units = [(r0, h) for r0 in range(0, ATT_QBLK, ATT_HALF) for h in range(ATT_HEADS_PER_STEP)]
    parts_of = lambda r0: ((0, r0, ATT_QBLK), (1, 0, ATT_QBLK), (2, 0, r0 + ATT_HALF))
    def score(r0, h):
        cols = slice(h * ATT_HD, (h + 1) * ATT_HD)
        first_chunk = (qb - 2) * (ATT_QBLK // CHUNK) + r0 // CHUNK
        before_start = jnp.where(kc + first_chunk >= 0, 0.0, MASK_VALUE)
        q = q_ref[r0:r0 + ATT_HALF, cols]
        s = jnp.concatenate(
            [lax.dot_general(q, k_refs[blk][a:b, cols], nt, preferred_element_type=F32)
             for blk, a, b in parts_of(r0)], axis=1)
        return s * (ATT_HD ** -0.5 * LOG2_E) + bias_ref[h] + before_start

    def softmax_terms(s):
        p = jnp.exp2(s - jnp.max(s, axis=-1, keepdims=True))
        return p.astype(BF16), jnp.sum(p, axis=-1, keepdims=True)

    def weighted_values(r0, h, pb, l):
        cols = slice(h * ATT_HD, (h + 1) * ATT_HD)
        o = jnp.zeros((ATT_HALF, ATT_HD), F32)
        c0 = 0
        for blk, a, b in parts_of(r0):
            o = o + jnp.dot(pb[:, c0:c0 + b - a], v_refs[blk][a:b, cols], preferred_element_type=F32)
            c0 += b - a
        o_ref[r0:r0 + ATT_HALF, cols] = (o / l).astype(o_ref.dtype)

    scores = [score(*u) for u in units]
    probs = [softmax_terms(s) for s in scores]
    for u, (pb, l) in zip(units, probs):
        weighted_values(*u, pb, l)


def attention_branch(proj3, q_col0, k_col0, v_col0, bias, layer, n_heads):
    b_, s_, _ = proj3.shape
    hp = ATT_HEADS_PER_STEP
    wblk = hp * ATT_HD
    qc0, kc0, vc0 = q_col0 // wblk, k_col0 // wblk, v_col0 // wblk

    def kv_spec(col0, back):
        return pl.BlockSpec((None, ATT_QBLK, wblk),
                            lambda h, b, i: (b, jnp.maximum(i - back, 0), col0 + h))

    return pl.pallas_call(
        _attn_body,
        grid=(n_heads // hp, b_, s_ // ATT_QBLK),
        in_specs=[
            pl.BlockSpec((None, ATT_QBLK, wblk), lambda h, b, i: (b, i, qc0 + h)),
            kv_spec(kc0, 2), kv_spec(kc0, 1), kv_spec(kc0, 0),
            kv_spec(vc0, 2), kv_spec(vc0, 1), kv_spec(vc0, 0),
            pl.BlockSpec((hp, ATT_HALF, ATT_HWIN), lambda h, b, i: (layer * (n_heads // hp) + h, 0, 0)),
        ],
        out_specs=pl.BlockSpec((None, ATT_QBLK, wblk), lambda h, b, i: (b, i, h)),
        out_shape=jax.ShapeDtypeStruct((b_, s_, n_heads * ATT_HD), BF16),
        compiler_params=_cparams(("parallel", "parallel", "parallel")),
        name="chunk_attention",
    )(proj3, proj3, proj3, proj3, proj3, proj3, proj3, bias)


def _s5_prep_body(are_c, aim_c, ls_c, are_r, aim_r, ls_r, ctre, ctim, bta, btb, w_ref, m_ref, lam_ref):
    blk = S5_BLK
    tw = S5_GT * SSM_GROUP
    sw = S5_GT * 2 * SSM_P
    lg_grp, lg_p = int(math.log2(SSM_GROUP)), int(math.log2(SSM_P))
    group_of_state = lambda q: lax.shift_right_logical(q, lg_p) & (S5_GT - 1)

    def lam_bar(are, aim, ls):
        step = jnp.exp(ls)
        mag = jnp.exp(are * step)
        return mag * jnp.cos(aim * step), mag * jnp.sin(aim * step)

    hw = sw // 2
    lr_c, li_c = lam_bar(are_c[...], aim_c[...], ls_c[...])
    rowc = lax.broadcasted_iota(jnp.int32, (hw, tw), 0)
    lanec = lax.broadcasted_iota(jnp.int32, (hw, tw), 1)
    same_c = group_of_state(rowc) == lax.shift_right_logical(lanec, lg_grp)
    cre = jnp.where(same_c, ctre[...], 0.0)
    cim = jnp.where(same_c, ctim[...], 0.0)
    pr = jnp.ones((hw, tw), F32)
    pi = jnp.zeros((hw, tw), F32)
    cl_re, cl_im = [], []
    for _ in range(blk + 1):
        cl_re.append(cre * pr - cim * pi)
        cl_im.append(-(cre * pi + cim * pr))
        pr, pi = pr * lr_c - pi * li_c, pr * li_c + pi * lr_c
    m_ref[0:hw, :] = jnp.concatenate(cl_re[1:], axis=1).astype(BF16)
    m_ref[hw:sw, :] = jnp.concatenate(cl_im[1:], axis=1).astype(BF16)
    cl0 = jnp.concatenate([jnp.concatenate(cl_re[:blk], axis=1),
                           jnp.concatenate(cl_im[:blk], axis=1)], axis=0)

    ar, ai = are_r[...], aim_r[...]
    lr, li = lam_bar(ar, ai, ls_r[...])
    den = ar * ar + ai * ai
    nr = lr - 1.0
    cr = (nr * ar + li * ai) / den
    ci = (li * ar - nr * ai) / den
    rowr = lax.broadcasted_iota(jnp.int32, (tw, sw), 0)
    laner = lax.broadcasted_iota(jnp.int32, (tw, sw), 1)
    same_r = lax.shift_right_logical(rowr, lg_grp) == group_of_state(laner)
    ba = jnp.where(same_r, bta[...], 0.0)
    bb = jnp.where(same_r, btb[...], 0.0)
    bb2 = cr * ba + ci * bb
    bb2s = cr * bb - ci * ba

    kt = jnp.dot(bb2, cl0, preferred_element_type=F32, precision=lax.Precision.HIGHEST).astype(BF16)
    pr = jnp.ones((1, sw), F32)
    pi = jnp.zeros((1, sw), F32)
    for i in range(blk - 1, -1, -1):
        rows = slice(i * tw, (i + 1) * tw)
        if i:
            w_ref[rows, 0:i * tw] = jnp.zeros((tw, i * tw), BF16)
        w_ref[rows, i * tw:blk * tw] = kt[:, 0:(blk - i) * tw]
        w_ref[rows, blk * tw:] = (pr * bb2 + pi * bb2s).astype(BF16)
        pr, pi = pr * lr - pi * li, pr * li + pi * lr
    sign = jnp.where(lax.broadcasted_iota(jnp.int32, (1, sw), 1) >= sw // 2, 1.0, -1.0)
    lam_ref[...] = jnp.zeros_like(lam_ref)
    lam_ref[0:1, :] = pr
    lam_ref[1:2, :] = sign * pi


def s5_prepare(a_re, a_im, b_re, b_im, c_re, c_im, log_step):
    n_g, npp = a_re.shape
    grp = b_re.shape[2]
    blk, gt = S5_BLK, S5_GT
    n_t = n_g // gt
    tw, sw = gt * grp, gt * 2 * npp
    f = lambda v: v.astype(F32)
    pair = lambda v: jnp.concatenate([f(v).reshape(n_t, gt * npp)] * 2, axis=-1)
    hw = sw // 2
    as_col = lambda v: jnp.broadcast_to(f(v).reshape(n_t, hw, 1), (n_t, hw, tw))
    as_row = lambda v: v.reshape(n_t, 1, sw)
    ls = jnp.broadcast_to(f(log_step)[:, None], (n_g, npp))
    ct = lambda v: jnp.tile(f(v).transpose(0, 2, 1).reshape(n_t, hw, grp), (1, 1, gt))
    b_t = lambda v: jnp.broadcast_to(f(v).transpose(0, 2, 1).reshape(n_t, gt, grp, 1, npp),
                                     (n_t, gt, grp, gt, npp)).reshape(n_t, tw, gt * npp)
    btr, bti = b_t(b_re), b_t(b_im)
    bt = lambda lo, hi: jnp.concatenate([lo, hi], axis=-1)
    t3 = lambda shp: pl.BlockSpec((None,) + shp, lambda t: (t, 0, 0))
    return pl.pallas_call(
        _s5_prep_body,
        grid=(n_t,),
        in_specs=[t3((hw, tw)), t3((hw, tw)), t3((hw, tw)), t3((1, sw)), t3((1, sw)), t3((1, sw)),
                  t3((hw, tw)), t3((hw, tw)), t3((tw, sw)), t3((tw, sw))],
        out_specs=[t3((blk * tw, blk * tw + sw)), t3((sw, blk * tw)), t3((SUBLANES, sw))],
        out_shape=[jax.ShapeDtypeStruct((n_t, blk * tw, blk * tw + sw), BF16),
                   jax.ShapeDtypeStruct((n_t, sw, blk * tw), BF16),
                   jax.ShapeDtypeStruct((n_t, SUBLANES, sw), F32)],
        compiler_params=_cparams(("parallel",)),
        name="s5_prepare",
    )(as_col(a_re), as_col(a_im), as_col(ls), as_row(pair(a_re)), as_row(pair(a_im)),
      as_row(pair(ls)), ct(c_re), ct(c_im), bt(btr, bti), bt(-bti, btr))


def _s5_body(u_ref, w_ref, m_ref, lam_ref, d_ref, o_ref, *, rows):
    blk = S5_BLK
    tw = S5_GT * SSM_GROUP
    sw = lam_ref.shape[1]
    u_steps = [u_ref[pl.ds(i, rows, stride=blk), :] for i in range(blk)]
    lhs = jnp.concatenate([u.astype(BF16) for u in u_steps], axis=1)
    z = jnp.dot(lhs, w_ref[:, blk * tw:], preferred_element_type=F32)
    r = jnp.dot(lhs, w_ref[:, 0:blk * tw], preferred_element_type=F32)
    row = lax.broadcasted_iota(jnp.int32, (rows, sw), 0)
    ca = lam_ref[0:1, :]
    cb = lam_ref[1:2, :]
    s = 1
    while s < rows:
        zs = jnp.where(row >= s, pltpu.roll(z, s, 0), 0.0)
        z = z + ca * zs + cb * pltpu.roll(zs, sw // 2, 1)
        ca, cb = ca * ca - cb * cb, 2.0 * ca * cb
        s *= 2
    zprev = jnp.where(row >= 1, pltpu.roll(z, 1, 0), 0.0)
    ys = jnp.dot(zprev.astype(BF16), m_ref[...], preferred_element_type=F32)
    d = d_ref[...]
    for j in range(blk):
        cols = slice(j * tw, (j + 1) * tw)
        y = r[:, cols] + ys[:, cols] + d * u_steps[j]
        o_ref[pl.ds(j, rows, stride=blk), :] = jax.nn.gelu(y)


def s5_branch(u, w, mm, lam, d, *, batch):
    t_, width = u.shape
    seq = t_ // batch
    n_t, sw = w.shape[0], lam.shape[2]
    tw = S5_GT * SSM_GROUP
    return pl.pallas_call(
        functools.partial(_s5_body, rows=seq // S5_BLK),
        grid=(n_t, batch),
        in_specs=[
            pl.BlockSpec((seq, tw), lambda c, b: (b, c)),
            pl.BlockSpec((None,) + w.shape[1:], lambda c, b: (c, 0, 0)),
            pl.BlockSpec((None,) + mm.shape[1:], lambda c, b: (c, 0, 0)),
            pl.BlockSpec((None, SUBLANES, sw), lambda c, b: (c, 0, 0)),
            pl.BlockSpec((1, tw), lambda c, b: (0, c)),
        ],
        out_specs=pl.BlockSpec((seq, tw), lambda c, b: (b, c)),
        out_shape=jax.ShapeDtypeStruct((t_, width), F32),
        compiler_params=_cparams(("parallel", "arbitrary")),
        name="s5_blocked",
    )(u, w, mm, lam, d)


def _merge_body(ya_ref, yb_ref, yc_ref, w0_ref, w1_ref, w2_ref, wg_ref,
                g0_ref, g1_ref, g2_ref, gb_ref, o_ref):
    ya, yb, yc = ya_ref[...], yb_ref[...], yc_ref[...].astype(BF16)
    half = o_ref.shape[1] // 2
    halves = (slice(0, half), slice(half, 2 * half))
    dot = lambda a, w, c: jnp.dot(a, w[:, c], preferred_element_type=F32)
    prods = [(dot(ya, w0_ref, c), dot(yb, w1_ref, c), dot(yc, w2_ref, c), dot(yc, wg_ref, c)) for c in halves]
    gate = lambda g_ref, i, c: _sigmoid(g_ref[:, c].astype(F32) + gb_ref[i:i + 1, c])
    for c, (br_a, br_b, br_c, glu) in zip(halves, prods):
        merged = (gate(g0_ref, 0, c) * br_a + gate(g1_ref, 1, c) * br_b
                  + gate(g2_ref, 2, c) * (br_c * _sigmoid(glu)))
        o_ref[:, c] = merged.astype(o_ref.dtype)


def branch_merge(ya, yb, yc, w_branch, w_glu, layer, proj, gate_col0, gate_bias, *, tm, tn):
    m, kdim = ya.shape
    n = w_glu.shape[2]
    gc0 = gate_col0 // tn
    nj = n // tn
    y_spec = pl.BlockSpec((tm, kdim), lambda i, j: (i, 0))
    wb_spec = lambda br: pl.BlockSpec((None, None, kdim, tn), lambda i, j: (layer, br, 0, j))
    g_spec = lambda br: pl.BlockSpec((tm, tn), lambda i, j: (i, gc0 + br * nj + j))
    return pl.pallas_call(
        _merge_body,
        grid=(m // tm, nj),
        in_specs=[y_spec, y_spec, y_spec, wb_spec(0), wb_spec(1), wb_spec(2),
                  pl.BlockSpec((None, kdim, tn), lambda i, j: (layer, 0, j)),
                  g_spec(0), g_spec(1), g_spec(2),
                  pl.BlockSpec((N_BRANCH, tn), lambda i, j: (0, j))],
        out_specs=pl.BlockSpec((tm, tn), lambda i, j: (i, j)),
        out_shape=jax.ShapeDtypeStruct((m, n), BF16),
        compiler_params=_cparams(("parallel", "arbitrary")),
        name="branch_merge",
    )(ya, yb, yc, w_branch, w_branch, w_branch, w_glu, proj, proj, proj, gate_bias.astype(F32))


def _matmul_res_body(a_ref, w_ref, r_ref, o_ref):
    o_ref[...] = r_ref[...] + jnp.dot(a_ref[...], w_ref[...], preferred_element_type=F32)


def matmul_residual(a, w, layer, res, *, tm, tn):
    m, k = a.shape
    n = w.shape[2]
    return pl.pallas_call(
        _matmul_res_body,
        grid=(m // tm, n // tn),
        in_specs=[pl.BlockSpec((tm, k), lambda i, j: (i, 0)),
                  pl.BlockSpec((None, k, tn), lambda i, j: (layer, 0, j)),
                  pl.BlockSpec((tm, tn), lambda i, j: (i, j))],
        out_specs=pl.BlockSpec((tm, tn), lambda i, j: (i, j)),
        out_shape=jax.ShapeDtypeStruct((m, n), F32),
        compiler_params=_cparams(("parallel", "arbitrary")),
        name="matmul_residual",
    )(a, w, res)


def _ffn_body(x_ref, g_ref, wg_ref, wu_ref, wd_ref, og_ref, o_ref, h_ref, *, norm_output):
    f = pl.program_id(1)

    @pl.when(f == 0)
    def _():
        x = x_ref[...]
        h_ref[...] = _rmsnorm_rows(x, g_ref[...]).astype(BF16)
        o_ref[...] = x

    h = h_ref[...]
    gate = jnp.dot(h, wg_ref[...], preferred_element_type=F32)
    up = jnp.dot(h, wu_ref[...], preferred_element_type=F32)
    act = (gate * _sigmoid(gate) * up).astype(BF16)
    o_ref[...] += jnp.dot(act, wd_ref[...], preferred_element_type=F32)

    if norm_output:
        @pl.when(f == pl.num_programs(1) - 1)
        def _():
            o_ref[...] = _rmsnorm_rows(o_ref[...], og_ref[...])


def ffn(x, g, w_gate, w_up, w_down, layer, out_g, *, norm_output, tm, tf):
    m, d = x.shape
    hidden = w_gate.shape[2]
    return pl.pallas_call(
        functools.partial(_ffn_body, norm_output=norm_output),
        grid=(m // tm, hidden // tf),
        in_specs=[pl.BlockSpec((tm, d), lambda i, f: (i, 0)),
                  pl.BlockSpec((1, d), lambda i, f: (0, 0)),
                  pl.BlockSpec((None, d, tf), lambda i, f: (layer, 0, f)),
                  pl.BlockSpec((None, d, tf), lambda i, f: (layer, 0, f)),
                  pl.BlockSpec((None, tf, d), lambda i, f: (layer, f, 0)),
                  pl.BlockSpec((1, d), lambda i, f: (0, 0))],
        out_specs=pl.BlockSpec((tm, d), lambda i, f: (i, 0)),
        out_shape=jax.ShapeDtypeStruct((m, d), F32),
        scratch_shapes=[pltpu.VMEM((tm, d), BF16)],
        compiler_params=_cparams(("parallel", "arbitrary")),
        name="swiglu_ffn",
    )(x, g.reshape(1, d), w_gate, w_up, w_down, out_g.reshape(1, d))


def _pick_tile(n, want):
    t = min(n, want)
    while n % t:
        t //= 2
    return t


def kernel(x, norm_mix_g, w_in, gate_bias, lru_conv_w, lru_conv_b, lru_wa, lru_ba, lru_wx, lru_bx,
           lru_lambda, attn_rel_bias, ssm_a_re, ssm_a_im, ssm_b_re, ssm_b_im, ssm_c_re, ssm_c_im,
           ssm_d, ssm_log_step, ssm_w_glu, w_branch, w_out, norm_ffn_g, w_ffn_gate, w_ffn_up,
           w_ffn_down, norm_final_g):
    b_, s_, d_model = x.shape
    depth = w_in.shape[0]
    mix_w = lru_conv_w.shape[2]
    n_heads = attn_rel_bias.shape[1]
    n_g = ssm_a_re.shape[1]
    t_ = b_ * s_
    assert s_ % ATT_QBLK == 0 and s_ % S5_BLK == 0 and n_g % S5_GT == 0
    c_lru_x, c_lru_g, c_q, c_k, c_v, c_u, c_gate = [i * mix_w for i in range(7)]
    in_w = w_in.shape[2]

    tm = _pick_tile(t_, 1024)
    ts_lru = _pick_tile(s_, 1024)
    bias = rel_bias_matrix(attn_rel_bias.reshape(depth * n_heads, N_REL))
    w_in, w_branch, ssm_w_glu, w_out, w_ffn_gate, w_ffn_up, w_ffn_down = [
        w.astype(BF16) for w in (w_in, w_branch, ssm_w_glu, w_out, w_ffn_gate, w_ffn_up, w_ffn_down)]

    xf = x.reshape(t_, d_model).astype(F32)
    for l in range(depth):
        proj, u32 = norm_matmul(xf, norm_mix_g[l], w_in, l, tm=tm, tn=mix_w, f32_col0=c_u)
        proj3 = proj.reshape(b_, s_, in_w)
        y_a = lru_branch(proj3, c_lru_x, c_lru_g, lru_conv_w[l], lru_conv_b[l], lru_wa[l], lru_ba[l],
                         lru_wx[l], lru_bx[l], lru_lambda[l], ts=ts_lru)
        y_b = attention_branch(proj3, c_q, c_k, c_v, bias, l, n_heads)

        w1, mm, lam = s5_prepare(ssm_a_re[l], ssm_a_im[l], ssm_b_re[l], ssm_b_im[l],
                                 ssm_c_re[l], ssm_c_im[l], ssm_log_step[l])
        y_c = s5_branch(u32, w1, mm, lam, ssm_d[l].astype(F32).reshape(1, mix_w), batch=b_)

        merged = branch_merge(y_a.reshape(t_, mix_w), y_b.reshape(t_, mix_w), y_c, w_branch, ssm_w_glu, l,
                              proj, c_gate, gate_bias[l], tm=tm, tn=512)
        xf = matmul_residual(merged, w_out, l, xf, tm=_pick_tile(t_, 512), tn=d_model)
        xf = ffn(xf, norm_ffn_g[l], w_ffn_gate, w_ffn_up, w_ffn_down, l, norm_final_g.astype(F32),
                 norm_output=(l == depth - 1), tm=_pick_tile(t_, 512), tf=512)
    return xf.reshape(b_, s_, d_model).astype(x.dtype)
```

```python
import functools
import math

import jax
import jax.numpy as jnp
from jax import lax
from jax.experimental import pallas as pl
from jax.experimental.pallas import tpu as pltpu

F32 = jnp.float32
BF16 = jnp.bfloat16

CHUNK = 64
LRU_BW = 64
CONV_W = 4
LRU_C = 8.0
ATT_HD = 128
ATT_LEFT_CHUNKS = 8
MAX_REL = 128
N_REL = 2 * MAX_REL + 1
SSM_GROUP = 16
SSM_P = 64
N_BRANCH = 3
NORM_EPS = 1e-6
MASK_VALUE = -1e30

LANES = 128
SUBLANES = 8
MXU_DIM = 256
VMEM_LIMIT_BYTES = 56 * 1024 * 1024

S5_BLK = 16
S5_GT = LANES // SSM_GROUP
LOG2_E = math.log2(math.e)
ATT_HEADS_PER_STEP = 8
ATT_QBLK = 4 * CHUNK
ATT_HALF = ATT_QBLK // 2
ATT_LEFT = ATT_LEFT_CHUNKS * CHUNK
ATT_HWIN = ATT_HALF + ATT_LEFT
LRU_CW = MXU_DIM
FFN_ACC_COLS = 2 * MXU_DIM
BIG_TILE_VMEM_LIMIT_BYTES = 60 * 1024 * 1024


def _cparams(sem):
    return pltpu.CompilerParams(dimension_semantics=sem, vmem_limit_bytes=VMEM_LIMIT_BYTES)


def _rmsnorm_rows(x, g):
    ms = jnp.mean(x * x, axis=-1, keepdims=True)
    return x * lax.rsqrt(ms + NORM_EPS) * g


def _norm_matmul_body(x_ref, g_ref, w_ref, o_ref, o32_ref, h_ref, *, f32_block, f32_off):
    j = pl.program_id(1)

    @pl.when(j == 0)
    def _():
        h_ref[...] = _rmsnorm_rows(x_ref[...], g_ref[...]).astype(BF16)

    h = h_ref[...]
    piece = o32_ref.shape[1]
    pieces = [slice(c, c + piece) for c in range(0, w_ref.shape[1], piece)]
    accs = [jnp.dot(h, w_ref[:, c], preferred_element_type=F32) for c in pieces]
    for c, acc in zip(pieces, accs):
        o_ref[:, c] = acc.astype(o_ref.dtype)

    @pl.when(j == f32_block)
    def _():
        o32_ref[...] = accs[f32_off // piece]


def norm_matmul(x, g, w, layer, *, tm, tn, f32_col0, f32_width):
    m, k = x.shape
    n = w.shape[2]
    assert tn % f32_width == 0 and f32_col0 % f32_width == 0
    return pl.pallas_call(
        functools.partial(_norm_matmul_body, f32_block=f32_col0 // tn, f32_off=f32_col0 % tn),
        grid=(m // tm, n // tn),
        in_specs=[
            pl.BlockSpec((tm, k), lambda i, j: (i, 0)),
            pl.BlockSpec((1, k), lambda i, j: (0, 0)),
            pl.BlockSpec((None, k, tn), lambda i, j: (layer, 0, j)),
        ],
        out_specs=[pl.BlockSpec((tm, tn), lambda i, j: (i, j)),
                   pl.BlockSpec((tm, f32_width), lambda i, j: (i, 0))],
        out_shape=[jax.ShapeDtypeStruct((m, n), BF16), jax.ShapeDtypeStruct((m, f32_width), F32)],
        scratch_shapes=[pltpu.VMEM((tm, k), BF16)],
        compiler_params=pltpu.CompilerParams(dimension_semantics=("parallel", "arbitrary"),
                                             vmem_limit_bytes=BIG_TILE_VMEM_LIMIT_BYTES),
        name="norm_matmul",
    )(x, g.reshape(1, k), w)


def _softplus(x):
    return jnp.maximum(x, 0.0) + jnp.log1p(jnp.exp(-jnp.abs(x)))


def _sigmoid(x):
    return 0.5 * jnp.tanh(0.5 * x) + 0.5


def _lru_body(x_ref, gate_ref, cw_ref, cb_ref, wa_ref, wx_ref, ba_ref, bx_ref, lam_ref,
              o_ref, xbuf_ref, h_ref, *, ts):
    t = pl.program_id(2)

    @pl.when(t == 0)
    def _():
        xbuf_ref[0:SUBLANES, :] = jnp.zeros((SUBLANES, LRU_CW), F32)
        h_ref[...] = jnp.zeros_like(h_ref)

    x = x_ref[...].astype(F32)
    xbuf_ref[SUBLANES:SUBLANES + ts, :] = x
    xc = cb_ref[...] + x * cw_ref[CONV_W - 1:CONV_W, :]
    for k in range(CONV_W - 1):
        back = CONV_W - 1 - k
        xc = xc + xbuf_ref[pl.ds(SUBLANES - back, ts), :] * cw_ref[k:k + 1, :]
    xbuf_ref[0:SUBLANES, :] = x[ts - SUBLANES:ts, :]

    xcb = xc.astype(BF16)
    ta = jnp.tanh(jnp.dot(xcb, wa_ref[...], preferred_element_type=F32) + ba_ref[...])
    tx = jnp.tanh(jnp.dot(xcb, wx_ref[...], preferred_element_type=F32) + bx_ref[...])
    c1 = (-0.25 * LRU_C) * _softplus(-lam_ref[...])
    th = jnp.tanh(c1 * ta + c1)
    rinv = 1.0 / (1.0 - th)
    a = (1.0 + th) * rinv
    xh = 0.5 * xc
    b = (2.0 * rinv) * jnp.sqrt(-th) * (xh * tx + xh)

    nt = ts // SUBLANES
    a = a.reshape(nt, SUBLANES, LRU_CW)
    b = b.reshape(nt, SUBLANES, LRU_CW)
    sub = lax.broadcasted_iota(jnp.int32, (nt, SUBLANES, LRU_CW), 1)
    s = 1
    while s < SUBLANES:
        keep = sub >= s
        b_sh = pltpu.roll(b, s, 1)
        a_sh = pltpu.roll(a, s, 1)
        b = b + jnp.where(keep, a * b_sh, 0.0)
        a = jnp.where(keep, a * a_sh, a)
        s *= 2
    carry = h_ref[...]
    tiles = []
    for k in range(nt):
        hk = b[k] + a[k] * carry
        tiles.append(hk)
        carry = hk[SUBLANES - 1:SUBLANES, :]
    h_ref[...] = carry
    h = jnp.concatenate(tiles, axis=0)
    o_ref[...] = (h * jax.nn.gelu(gate_ref[...].astype(F32))).astype(o_ref.dtype)


def _block_diag_tiles(w):
    n_blk, bw, _ = w.shape
    per = LRU_CW // bw
    wt = w.reshape(n_blk // per, per, bw, bw)
    eye = jnp.eye(per, dtype=w.dtype)
    tiles = jnp.einsum("cpkj,pq->cpkqj", wt, eye)
    return tiles.reshape(n_blk // per, LRU_CW, LRU_CW)


def lru_branch(proj3, x_col0, gate_col0, conv_w, conv_b, wa, ba, wx, bx, lam, *, ts):
    b_, s_, _ = proj3.shape
    width = conv_w.shape[1]
    n_c = width // LRU_CW
    xc0 = x_col0 // LRU_CW
    gc0 = gate_col0 // LRU_CW
    row = lambda v: v.reshape(1, width).astype(F32)
    vec_spec = pl.BlockSpec((1, LRU_CW), lambda b, c, t: (0, c))
    return pl.pallas_call(
        functools.partial(_lru_body, ts=ts),
        grid=(b_, n_c, s_ // ts),
        in_specs=[
            pl.BlockSpec((None, ts, LRU_CW), lambda b, c, t: (b, t, xc0 + c)),
            pl.BlockSpec((None, ts, LRU_CW), lambda b, c, t: (b, t, gc0 + c)),
            pl.BlockSpec((CONV_W, LRU_CW), lambda b, c, t: (0, c)),
            vec_spec,
            pl.BlockSpec((None, LRU_CW, LRU_CW), lambda b, c, t: (c, 0, 0)),
            pl.BlockSpec((None, LRU_CW, LRU_CW), lambda b, c, t: (c, 0, 0)),
            vec_spec, vec_spec, vec_spec,
        ],
        out_specs=pl.BlockSpec((None, ts, LRU_CW), lambda b, c, t: (b, t, c)),
        out_shape=jax.ShapeDtypeStruct((b_, s_, width), BF16),
        scratch_shapes=[pltpu.VMEM((ts + SUBLANES, LRU_CW), F32), pltpu.VMEM((1, LRU_CW), F32)],
        compiler_params=_cparams(("parallel", "parallel", "arbitrary")),
        name="rg_lru",
    )(proj3, proj3, conv_w.astype(F32), row(conv_b), _block_diag_tiles(0.5 * wa).astype(BF16),
      _block_diag_tiles(0.5 * wx).astype(BF16), row(0.5 * ba), row(0.5 * bx), row(lam))


def _rel_bias_body(tab_ref, o_ref):
    h = pl.program_id(0)
    width = 1 << (ATT_HWIN + ATT_HALF - 1).bit_length()
    m = lax.broadcasted_iota(jnp.int32, (1, width), 1)
    delta = jnp.where(m < ATT_HWIN, m, m - width)
    idx = jnp.clip(ATT_LEFT - delta, -MAX_REL, MAX_REL) + MAX_REL

    def pick(d, acc):
        return jnp.where(idx == d, tab_ref[h, d], acc)

    gvec = lax.fori_loop(0, N_REL, pick, jnp.zeros((1, width), F32), unroll=N_REL)
    full = pltpu.roll(jnp.broadcast_to(gvec, (ATT_HALF, width)), 0, 1, stride=1, stride_axis=0)
    chunk_shift = int(math.log2(CHUNK))
    qc = lax.shift_right_logical(lax.broadcasted_iota(jnp.int32, (ATT_HALF, ATT_HWIN), 0), chunk_shift)
    kc = lax.shift_right_logical(lax.broadcasted_iota(jnp.int32, (ATT_HALF, ATT_HWIN), 1), chunk_shift)
    in_band = (kc >= qc) & (kc <= qc + ATT_LEFT_CHUNKS)
    o_ref[...] = jnp.where(in_band, full[:, :ATT_HWIN] * LOG2_E, MASK_VALUE)


def rel_bias_matrix(rel_bias):
    n = rel_bias.shape[0]
    return pl.pallas_call(
        _rel_bias_body,
        grid=(n,),
        in_specs=[pl.BlockSpec(memory_space=pltpu.SMEM)],
        out_specs=pl.BlockSpec((None, ATT_HALF, ATT_HWIN), lambda h: (h, 0, 0)),
        out_shape=jax.ShapeDtypeStruct((n, ATT_HALF, ATT_HWIN), F32),
        compiler_params=_cparams(("arbitrary",)),
        name="rel_bias",
    )(rel_bias.astype(F32))


def _attn_body(q_ref, k0_ref, k1_ref, k2_ref, v0_ref, v1_ref, v2_ref, bias_ref, o_ref):
    qb = pl.program_id(2)
    nt = (((1,), (1,)), ((), ()))
    chunk_shift = int(math.log2(CHUNK))
    kc = lax.shift_right_logical(lax.broadcasted_iota(jnp.int32, (1, ATT_HWIN), 1), chunk_shift)
    k_refs = (k0_ref, k1_ref, k2_ref)
    v_refs = (v0_ref, v1_ref, v2_ref)
    units = [(r0, h) for r0 in range(0, ATT_QBLK, ATT_HALF) for h in range(ATT_HEADS_PER_STEP)]
    parts_of = lambda r0: ((0, r0, ATT_QBLK), (1, 0, ATT_QBLK), (2, 0, r0 + ATT_HALF))
    def score(r0, h):
        cols = slice(h * ATT_HD, (h + 1) * ATT_HD)
        first_chunk = (qb - 2) * (ATT_QBLK // CHUNK) + r0 // CHUNK
        before_start = jnp.where(kc + first_chunk >= 0, 0.0, MASK_VALUE)
        q = q_ref[r0:r0 + ATT_HALF, cols]
        s = jnp.concatenate(
            [lax.dot_general(q, k_refs[blk][a:b, cols], nt, preferred_element_type=F32)
             for blk, a, b in parts_of(r0)], axis=1)
        return s * (ATT_HD ** -0.5 * LOG2_E) + bias_ref[h] + before_start

    def softmax_terms(s):
        p = jnp.exp2(s - jnp.max(s, axis=-1, keepdims=True))
        return p.astype(BF16), jnp.sum(p, axis=-1, keepdims=True)

    def weighted_values(r0, h, pb, l):
        cols = slice(h * ATT_HD, (h + 1) * ATT_HD)
        o = jnp.zeros((ATT_HALF, ATT_HD), F32)
        c0 = 0
        for blk, a, b in parts_of(r0):
            o = o + jnp.dot(pb[:, c0:c0 + b - a], v_refs[blk][a:b, cols], preferred_element_type=F32)
            c0 += b - a
        o_ref[r0:r0 + ATT_HALF, cols] = (o / l).astype(o_ref.dtype)

    scores = [score(*u) for u in units]
    probs = [softmax_terms(s) for s in scores]
    for u, (pb, l) in zip(units, probs):
        weighted_values(*u, pb, l)


def attention_branch(proj3, q_col0, k_col0, v_col0, bias, layer, n_heads):
    b_, s_, _ = proj3.shape
    hp = ATT_HEADS_PER_STEP
    wblk = hp * ATT_HD
    qc0, kc0, vc0 = q_col0 // wblk, k_col0 // wblk, v_col0 // wblk

    def kv_spec(col0, back):
        return pl.BlockSpec((None, ATT_QBLK, wblk),
                            lambda h, b, i: (b, jnp.maximum(i - back, 0), col0 + h))

    return pl.pallas_call(
        _attn_body,
        grid=(n_heads // hp, b_, s_ // ATT_QBLK),
        in_specs=[
            pl.BlockSpec((None, ATT_QBLK, wblk), lambda h, b, i: (b, i, qc0 + h)),
            kv_spec(kc0, 2), kv_spec(kc0, 1), kv_spec(kc0, 0),
            kv_spec(vc0, 2), kv_spec(vc0, 1), kv_spec(vc0, 0),
            pl.BlockSpec((hp, ATT_HALF, ATT_HWIN), lambda h, b, i: (layer * (n_heads // hp) + h, 0, 0)),
        ],
        out_specs=pl.BlockSpec((None, ATT_QBLK, wblk), lambda h, b, i: (b, i, h)),
        out_shape=jax.ShapeDtypeStruct((b_, s_, n_heads * ATT_HD), BF16),
        compiler_params=_cparams(("parallel", "parallel", "parallel")),
        name="chunk_attention",
    )(proj3, proj3, proj3, proj3, proj3, proj3, proj3, bias)


def _s5_prep_body(are_c, aim_c, ls_c, are_r, aim_r, ls_r, ctre, ctim, bta, btb, w_ref, m_ref, lam_ref):
    blk = S5_BLK
    tw = S5_GT * SSM_GROUP
    sw = S5_GT * 2 * SSM_P
    lg_grp, lg_p = int(math.log2(SSM_GROUP)), int(math.log2(SSM_P))
    group_of_state = lambda q: lax.shift_right_logical(q, lg_p) & (S5_GT - 1)

    def lam_bar(are, aim, ls):
        step = jnp.exp(ls)
        mag = jnp.exp(are * step)
        return mag * jnp.cos(aim * step), mag * jnp.sin(aim * step)

    hw = sw // 2
    lr_c, li_c = lam_bar(are_c[...], aim_c[...], ls_c[...])
    rowc = lax.broadcasted_iota(jnp.int32, (hw, tw), 0)
    lanec = lax.broadcasted_iota(jnp.int32, (hw, tw), 1)
    same_c = group_of_state(rowc) == lax.shift_right_logical(lanec, lg_grp)
    cre = jnp.where(same_c, ctre[...], 0.0)
    cim = jnp.where(same_c, ctim[...], 0.0)
    pr = jnp.ones((hw, tw), F32)
    pi = jnp.zeros((hw, tw), F32)
    cl_re, cl_im = [], []
    for _ in range(blk + 1):
        cl_re.append(cre * pr - cim * pi)
        cl_im.append(-(cre * pi + cim * pr))
        pr, pi = pr * lr_c - pi * li_c, pr * li_c + pi * lr_c
    m_ref[0:hw, :] = jnp.concatenate(cl_re[1:], axis=1).astype(BF16)
    m_ref[hw:sw, :] = jnp.concatenate(cl_im[1:], axis=1).astype(BF16)
    cl0 = jnp.concatenate([jnp.concatenate(cl_re[:blk], axis=1),
                           jnp.concatenate(cl_im[:blk], axis=1)], axis=0)

    ar, ai = are_r[...], aim_r[...]
    lr, li = lam_bar(ar, ai, ls_r[...])
    den = ar * ar + ai * ai
    nr = lr - 1.0
    cr = (nr * ar + li * ai) / den
    ci = (li * ar - nr * ai) / den
    rowr = lax.broadcasted_iota(jnp.int32, (tw, sw), 0)
    laner = lax.broadcasted_iota(jnp.int32, (tw, sw), 1)
    same_r = lax.shift_right_logical(rowr, lg_grp) == group_of_state(laner)
    ba = jnp.where(same_r, bta[...], 0.0)
    bb = jnp.where(same_r, btb[...], 0.0)
    bb2 = cr * ba + ci * bb
    bb2s = cr * bb - ci * ba

    kt = jnp.dot(bb2, cl0, preferred_element_type=F32, precision=lax.Precision.HIGHEST).astype(BF16)
    pr = jnp.ones((1, sw), F32)
    pi = jnp.zeros((1, sw), F32)
    for i in range(blk - 1, -1, -1):
        rows = slice(i * tw, (i + 1) * tw)
        if i:
            w_ref[rows, 0:i * tw] = jnp.zeros((tw, i * tw), BF16)
        w_ref[rows, i * tw:blk * tw] = kt[:, 0:(blk - i) * tw]
        w_ref[rows, blk * tw:] = (pr * bb2 + pi * bb2s).astype(BF16)
        pr, pi = pr * lr - pi * li, pr * li + pi * lr
    sign = jnp.where(lax.broadcasted_iota(jnp.int32, (1, sw), 1) >= sw // 2, 1.0, -1.0)
    lam_ref[...] = jnp.zeros_like(lam_ref)
    lam_ref[0:1, :] = pr
    lam_ref[1:2, :] = sign * pi


def s5_prepare(a_re, a_im, b_re, b_im, c_re, c_im, log_step):
    n_g, npp = a_re.shape
    grp = b_re.shape[2]
    blk, gt = S5_BLK, S5_GT
    n_t = n_g // gt
    tw, sw = gt * grp, gt * 2 * npp
    f = lambda v: v.astype(F32)
    pair = lambda v: jnp.concatenate([f(v).reshape(n_t, gt * npp)] * 2, axis=-1)
    hw = sw // 2
    as_col = lambda v: jnp.broadcast_to(f(v).reshape(n_t, hw, 1), (n_t, hw, tw))
    as_row = lambda v: v.reshape(n_t, 1, sw)
    ls = jnp.broadcast_to(f(log_step)[:, None], (n_g, npp))
    ct = lambda v: jnp.tile(f(v).transpose(0, 2, 1).reshape(n_t, hw, grp), (1, 1, gt))
    b_t = lambda v: jnp.broadcast_to(f(v).transpose(0, 2, 1).reshape(n_t, gt, grp, 1, npp),
                                     (n_t, gt, grp, gt, npp)).reshape(n_t, tw, gt * npp)
    btr, bti = b_t(b_re), b_t(b_im)
    bt = lambda lo, hi: jnp.concatenate([lo, hi], axis=-1)
    t3 = lambda shp: pl.BlockSpec((None,) + shp, lambda t: (t, 0, 0))
    return pl.pallas_call(
        _s5_prep_body,
        grid=(n_t,),
        in_specs=[t3((hw, tw)), t3((hw, tw)), t3((hw, tw)), t3((1, sw)), t3((1, sw)), t3((1, sw)),
                  t3((hw, tw)), t3((hw, tw)), t3((tw, sw)), t3((tw, sw))],
        out_specs=[t3((blk * tw, blk * tw + sw)), t3((sw, blk * tw)), t3((SUBLANES, sw))],
        out_shape=[jax.ShapeDtypeStruct((n_t, blk * tw, blk * tw + sw), BF16),
                   jax.ShapeDtypeStruct((n_t, sw, blk * tw), BF16),
                   jax.ShapeDtypeStruct((n_t, SUBLANES, sw), F32)],
        compiler_params=_cparams(("parallel",)),
        name="s5_prepare",
    )(as_col(a_re), as_col(a_im), as_col(ls), as_row(pair(a_re)), as_row(pair(a_im)),
      as_row(pair(ls)), ct(c_re), ct(c_im), bt(btr, bti), bt(-bti, btr))


def _s5_body(u_ref, w_ref, m_ref, lam_ref, d_ref, o_ref, *, rows):
    blk = S5_BLK
    tw = S5_GT * SSM_GROUP
    sw = lam_ref.shape[1]
    u_steps = [u_ref[pl.ds(i, rows, stride=blk), :] for i in range(blk)]
    lhs = jnp.concatenate([u.astype(BF16) for u in u_steps], axis=1)
    z = jnp.dot(lhs, w_ref[:, blk * tw:], preferred_element_type=F32)
    r = jnp.dot(lhs, w_ref[:, 0:blk * tw], preferred_element_type=F32)
    row = lax.broadcasted_iota(jnp.int32, (rows, sw), 0)
    ca = lam_ref[0:1, :]
    cb = lam_ref[1:2, :]
    s = 1
    while s < rows:
        zs = jnp.where(row >= s, pltpu.roll(z, s, 0), 0.0)
        z = z + ca * zs + cb * pltpu.roll(zs, sw // 2, 1)
        ca, cb = ca * ca - cb * cb, 2.0 * ca * cb
        s *= 2
    zprev = jnp.where(row >= 1, pltpu.roll(z, 1, 0), 0.0)
    ys = jnp.dot(zprev.astype(BF16), m_ref[...], preferred_element_type=F32)
    d = d_ref[...]
    for j in range(blk):
        cols = slice(j * tw, (j + 1) * tw)
        y = r[:, cols] + ys[:, cols] + d * u_steps[j]
        o_ref[pl.ds(j, rows, stride=blk), :] = jax.nn.gelu(y)


def s5_branch(u, w, mm, lam, d, *, batch):
    t_, width = u.shape
    seq = t_ // batch
    n_t, sw = w.shape[0], lam.shape[2]
    tw = S5_GT * SSM_GROUP
    return pl.pallas_call(
        functools.partial(_s5_body, rows=seq // S5_BLK),
        grid=(n_t, batch),
        in_specs=[
            pl.BlockSpec((seq, tw), lambda c, b: (b, c)),
            pl.BlockSpec((None,) + w.shape[1:], lambda c, b: (c, 0, 0)),
            pl.BlockSpec((None,) + mm.shape[1:], lambda c, b: (c, 0, 0)),
            pl.BlockSpec((None, SUBLANES, sw), lambda c, b: (c, 0, 0)),
            pl.BlockSpec((1, tw), lambda c, b: (0, c)),
        ],
        out_specs=pl.BlockSpec((seq, tw), lambda c, b: (b, c)),
        out_shape=jax.ShapeDtypeStruct((t_, width), F32),
        compiler_params=_cparams(("parallel", "arbitrary")),
        name="s5_blocked",
    )(u, w, mm, lam, d)


def _merge_body(ya_ref, yb_ref, yc_ref, w0_ref, w1_ref, w2_ref, wg_ref,
                g0_ref, g1_ref, g2_ref, gb_ref, o_ref):
    ya, yb, yc = ya_ref[...], yb_ref[...], yc_ref[...].astype(BF16)
    half = o_ref.shape[1] // 2
    halves = (slice(0, half), slice(half, 2 * half))
    dot = lambda a, w, c: jnp.dot(a, w[:, c], preferred_element_type=F32)
    prods = [(dot(ya, w0_ref, c), dot(yb, w1_ref, c), dot(yc, w2_ref, c), dot(yc, wg_ref, c)) for c in halves]
    gate = lambda g_ref, i, c: _sigmoid(g_ref[:, c].astype(F32) + gb_ref[i:i + 1, c])
    for c, (br_a, br_b, br_c, glu) in zip(halves, prods):
        merged = (gate(g0_ref, 0, c) * br_a + gate(g1_ref, 1, c) * br_b
                  + gate(g2_ref, 2, c) * (br_c * _sigmoid(glu)))
        o_ref[:, c] = merged.astype(o_ref.dtype)


def branch_merge(ya, yb, yc, w_branch, w_glu, layer, proj, gate_col0, gate_bias, *, tm, tn):
    m, kdim = ya.shape
    n = w_glu.shape[2]
    gc0 = gate_col0 // tn
    nj = n // tn
    y_spec = pl.BlockSpec((tm, kdim), lambda i, j: (i, 0))
    wb_spec = lambda br: pl.BlockSpec((None, None, kdim, tn), lambda i, j: (layer, br, 0, j))
    g_spec = lambda br: pl.BlockSpec((tm, tn), lambda i, j: (i, gc0 + br * nj + j))
    return pl.pallas_call(
        _merge_body,
        grid=(m // tm, nj),
        in_specs=[y_spec, y_spec, y_spec, wb_spec(0), wb_spec(1), wb_spec(2),
                  pl.BlockSpec((None, kdim, tn), lambda i, j: (layer, 0, j)),
                  g_spec(0), g_spec(1), g_spec(2),
                  pl.BlockSpec((N_BRANCH, tn), lambda i, j: (0, j))],
        out_specs=pl.BlockSpec((tm, tn), lambda i, j: (i, j)),
        out_shape=jax.ShapeDtypeStruct((m, n), BF16),
        compiler_params=_cparams(("parallel", "arbitrary")),
        name="branch_merge",
    )(ya, yb, yc, w_branch, w_branch, w_branch, w_glu, proj, proj, proj, gate_bias.astype(F32))


def _matmul_res_body(a_ref, w_ref, r_ref, o_ref):
    o_ref[...] = r_ref[...] + jnp.dot(a_ref[...], w_ref[...], preferred_element_type=F32)


def matmul_residual(a, w, layer, res, *, tm, tn):
    m, k = a.shape
    n = w.shape[2]
    return pl.pallas_call(
        _matmul_res_body,
        grid=(m // tm, n // tn),
        in_specs=[pl.BlockSpec((tm, k), lambda i, j: (i, 0)),
                  pl.BlockSpec((None, k, tn), lambda i, j: (layer, 0, j)),
                  pl.BlockSpec((tm, tn), lambda i, j: (i, j))],
        out_specs=pl.BlockSpec((tm, tn), lambda i, j: (i, j)),
        out_shape=jax.ShapeDtypeStruct((m, n), F32),
        compiler_params=_cparams(("parallel", "arbitrary")),
        name="matmul_residual",
    )(a, w, res)


def _ffn_body(x_ref, g_ref, wg_ref, wu_ref, wd_ref, og_ref, o_ref, h_ref, *, norm_output):
    f = pl.program_id(1)

    @pl.when(f == 0)
    def _():
        x = x_ref[...]
        h_ref[...] = _rmsnorm_rows(x, g_ref[...]).astype(BF16)
        o_ref[...] = x

    h = h_ref[...]
    gate = jnp.dot(h, wg_ref[...], preferred_element_type=F32)
    up = jnp.dot(h, wu_ref[...], preferred_element_type=F32)
    act = (gate * _sigmoid(gate) * up).astype(BF16)
    for c in range(0, o_ref.shape[1], FFN_ACC_COLS):
        cols = slice(c, c + FFN_ACC_COLS)
        o_ref[:, cols] += jnp.dot(act, wd_ref[:, cols], preferred_element_type=F32)

    if norm_output:
        @pl.when(f == pl.num_programs(1) - 1)
        def _():
            o_ref[...] = _rmsnorm_rows(o_ref[...], og_ref[...])


def ffn(x, g, w_gate, w_up, w_down, layer, out_g, *, norm_output, tm, tf):
    m, d = x.shape
    hidden = w_gate.shape[2]
    return pl.pallas_call(
        functools.partial(_ffn_body, norm_output=norm_output),
        grid=(m // tm, hidden // tf),
        in_specs=[pl.BlockSpec((tm, d), lambda i, f: (i, 0)),
                  pl.BlockSpec((1, d), lambda i, f: (0, 0)),
                  pl.BlockSpec((None, d, tf), lambda i, f: (layer, 0, f)),
                  pl.BlockSpec((None, d, tf), lambda i, f: (layer, 0, f)),
                  pl.BlockSpec((None, tf, d), lambda i, f: (layer, f, 0)),
                  pl.BlockSpec((1, d), lambda i, f: (0, 0))],
        out_specs=pl.BlockSpec((tm, d), lambda i, f: (i, 0)),
        out_shape=jax.ShapeDtypeStruct((m, d), F32),
        scratch_shapes=[pltpu.VMEM((tm, d), BF16)],
        compiler_params=pltpu.CompilerParams(dimension_semantics=("parallel", "arbitrary"),
                                             vmem_limit_bytes=BIG_TILE_VMEM_LIMIT_BYTES),
        name="swiglu_ffn",
    )(x, g.reshape(1, d), w_gate, w_up, w_down, out_g.reshape(1, d))


def _pick_tile(n, want):
    t = min(n, want)
    while n % t:
        t //= 2
    return t


def kernel(x, norm_mix_g, w_in, gate_bias, lru_conv_w, lru_conv_b, lru_wa, lru_ba, lru_wx, lru_bx,
           lru_lambda, attn_rel_bias, ssm_a_re, ssm_a_im, ssm_b_re, ssm_b_im, ssm_c_re, ssm_c_im,
           ssm_d, ssm_log_step, ssm_w_glu, w_branch, w_out, norm_ffn_g, w_ffn_gate, w_ffn_up,
           w_ffn_down, norm_final_g):
    b_, s_, d_model = x.shape
    depth = w_in.shape[0]
    mix_w = lru_conv_w.shape[2]
    n_heads = attn_rel_bias.shape[1]
    n_g = ssm_a_re.shape[1]
    t_ = b_ * s_
    assert s_ % ATT_QBLK == 0 and s_ % S5_BLK == 0 and n_g % S5_GT == 0
    c_lru_x, c_lru_g, c_q, c_k, c_v, c_u, c_gate = [i * mix_w for i in range(7)]
    in_w = w_in.shape[2]

    tm = _pick_tile(t_, 1024)
    ts_lru = _pick_tile(s_, 1024)
    bias = rel_bias_matrix(attn_rel_bias.reshape(depth * n_heads, N_REL))
    w_in, w_branch, ssm_w_glu, w_out, w_ffn_gate, w_ffn_up, w_ffn_down = [
        w.astype(BF16) for w in (w_in, w_branch, ssm_w_glu, w_out, w_ffn_gate, w_ffn_up, w_ffn_down)]

    xf = x.reshape(t_, d_model).astype(F32)
    for l in range(depth):
        proj, u32 = norm_matmul(xf, norm_mix_g[l], w_in, l, tm=tm, tn=2 * mix_w, f32_col0=c_u, f32_width=mix_w)
        proj3 = proj.reshape(b_, s_, in_w)
        y_a = lru_branch(proj3, c_lru_x, c_lru_g, lru_conv_w[l], lru_conv_b[l], lru_wa[l], lru_ba[l],
                         lru_wx[l], lru_bx[l], lru_lambda[l], ts=ts_lru)
        y_b = attention_branch(proj3, c_q, c_k, c_v, bias, l, n_heads)

        w1, mm, lam = s5_prepare(ssm_a_re[l], ssm_a_im[l], ssm_b_re[l], ssm_b_im[l],
                                 ssm_c_re[l], ssm_c_im[l], ssm_log_step[l])
        y_c = s5_branch(u32, w1, mm, lam, ssm_d[l].astype(F32).reshape(1, mix_w), batch=b_)

        merged = branch_merge(y_a.reshape(t_, mix_w), y_b.reshape(t_, mix_w), y_c, w_branch, ssm_w_glu, l,
                              proj, c_gate, gate_bias[l], tm=tm, tn=512)
        xf = matmul_residual(merged, w_out, l, xf, tm=_pick_tile(t_, 512), tn=d_model)
        xf = ffn(xf, norm_ffn_g[l], w_ffn_gate, w_ffn_up, w_ffn_down, l, norm_final_g.astype(F32),
                 norm_output=(l == depth - 1), tm=tm, tf=512)
    return xf.reshape(b_, s_, d_model).astype(x.dtype)
```

```python
import functools
import math

import jax
import jax.numpy as jnp
from jax import lax
from jax.experimental import pallas as pl
from jax.experimental.pallas import tpu as pltpu

F32 = jnp.float32
BF16 = jnp.bfloat16

CHUNK = 64
LRU_BW = 64
CONV_W = 4
LRU_C = 8.0
ATT_HD = 128
ATT_LEFT_CHUNKS = 8
MAX_REL = 128
N_REL = 2 * MAX_REL + 1
SSM_GROUP = 16
SSM_P = 64
N_BRANCH = 3
NORM_EPS = 1e-6
MASK_VALUE = -1e30

LANES = 128
SUBLANES = 8
MXU_DIM = 256
VMEM_BYTES = 64 * 1024 * 1024
VMEM_LIMIT_BYTES = VMEM_BYTES - 4 * 1024 * 1024

S5_BLK = 16
S5_GT = LANES // SSM_GROUP
LOG2_E = math.log2(math.e)
ATT_HEADS_PER_STEP = 8
ATT_QBLK = 4 * CHUNK
ATT_HALF = ATT_QBLK // 2
ATT_LEFT = ATT_LEFT_CHUNKS * CHUNK
ATT_HWIN = ATT_HALF + ATT_LEFT
LRU_CW = MXU_DIM
FFN_ACC_COLS = 2 * MXU_DIM


def _cparams(sem):
    return pltpu.CompilerParams(dimension_semantics=sem, vmem_limit_bytes=VMEM_LIMIT_BYTES)


def _rmsnorm_rows(x, g):
    ms = jnp.mean(x * x, axis=-1, keepdims=True)
    return x * lax.rsqrt(ms + NORM_EPS) * g


def _norm_matmul_body(x_ref, g_ref, w_ref, o_ref, o32_ref, h_ref, *, f32_block, f32_off):
    j = pl.program_id(1)

    @pl.when(j == 0)
    def _():
        h_ref[...] = _rmsnorm_rows(x_ref[...], g_ref[...]).astype(BF16)

    h = h_ref[...]
    piece = o32_ref.shape[1]
    pieces = [slice(c, c + piece) for c in range(0, w_ref.shape[1], piece)]
    accs = [jnp.dot(h, w_ref[:, c], preferred_element_type=F32) for c in pieces]
    for c, acc in zip(pieces, accs):
        o_ref[:, c] = acc.astype(o_ref.dtype)

    @pl.when(j == f32_block)
    def _():
        o32_ref[...] = accs[f32_off // piece]


def norm_matmul(x, g, w, layer, *, tm, tn, f32_col0, f32_width):
    m, k = x.shape
    n = w.shape[2]
    assert tn % f32_width == 0 and f32_col0 % f32_width == 0
    return pl.pallas_call(
        functools.partial(_norm_matmul_body, f32_block=f32_col0 // tn, f32_off=f32_col0 % tn),
        grid=(m // tm, n // tn),
        in_specs=[
            pl.BlockSpec((tm, k), lambda i, j: (i, 0)),
            pl.BlockSpec((1, k), lambda i, j: (0, 0)),
            pl.BlockSpec((None, k, tn), lambda i, j: (layer, 0, j)),
        ],
        out_specs=[pl.BlockSpec((tm, tn), lambda i, j: (i, j)),
                   pl.BlockSpec((tm, f32_width), lambda i, j: (i, 0))],
        out_shape=[jax.ShapeDtypeStruct((m, n), BF16), jax.ShapeDtypeStruct((m, f32_width), F32)],
        scratch_shapes=[pltpu.VMEM((tm, k), BF16)],
        compiler_params=_cparams(("parallel", "arbitrary")),
        name="norm_matmul",
    )(x, g.reshape(1, k), w)


def _softplus(x):
    return jnp.maximum(x, 0.0) + jnp.log1p(jnp.exp(-jnp.abs(x)))


def _sigmoid(x):
    return 0.5 * jnp.tanh(0.5 * x) + 0.5


def _lru_body(x_ref, gate_ref, cw_ref, cb_ref, wa_ref, wx_ref, ba_ref, bx_ref, lam_ref,
              o_ref, xbuf_ref, h_ref, *, ts):
    t = pl.program_id(2)

    @pl.when(t == 0)
    def _():
        xbuf_ref[0:SUBLANES, :] = jnp.zeros((SUBLANES, LRU_CW), F32)
        h_ref[...] = jnp.zeros_like(h_ref)

    x = x_ref[...].astype(F32)
    xbuf_ref[SUBLANES:SUBLANES + ts, :] = x
    xc = cb_ref[...] + x * cw_ref[CONV_W - 1:CONV_W, :]
    for k in range(CONV_W - 1):
        back = CONV_W - 1 - k
        xc = xc + xbuf_ref[pl.ds(SUBLANES - back, ts), :] * cw_ref[k:k + 1, :]
    xbuf_ref[0:SUBLANES, :] = x[ts - SUBLANES:ts, :]

    xcb = xc.astype(BF16)
    ta = jnp.tanh(jnp.dot(xcb, wa_ref[...], preferred_element_type=F32) + ba_ref[...])
    tx = jnp.tanh(jnp.dot(xcb, wx_ref[...], preferred_element_type=F32) + bx_ref[...])
    c1 = (-0.25 * LRU_C) * _softplus(-lam_ref[...])
    th = jnp.tanh(c1 * ta + c1)
    rinv = 1.0 / (1.0 - th)
    a = (1.0 + th) * rinv
    xh = 0.5 * xc
    b = (2.0 * rinv) * jnp.sqrt(-th) * (xh * tx + xh)

    nt = ts // SUBLANES
    a = a.reshape(nt, SUBLANES, LRU_CW)
    b = b.reshape(nt, SUBLANES, LRU_CW)
    sub = lax.broadcasted_iota(jnp.int32, (nt, SUBLANES, LRU_CW), 1)
    s = 1
    while s < SUBLANES:
        keep = sub >= s
        b_sh = pltpu.roll(b, s, 1)
        a_sh = pltpu.roll(a, s, 1)
        b = b + jnp.where(keep, a * b_sh, 0.0)
        a = jnp.where(keep, a * a_sh, a)
        s *= 2
    carry = h_ref[...]
    tiles = []
    for k in range(nt):
        hk = b[k] + a[k] * carry
        tiles.append(hk)
        carry = hk[SUBLANES - 1:SUBLANES, :]
    h_ref[...] = carry
    h = jnp.concatenate(tiles, axis=0)
    o_ref[...] = (h * jax.nn.gelu(gate_ref[...].astype(F32))).astype(o_ref.dtype)


def _block_diag_tiles(w):
    n_blk, bw, _ = w.shape
    per = LRU_CW // bw
    wt = w.reshape(n_blk // per, per, bw, bw)
    eye = jnp.eye(per, dtype=w.dtype)
    tiles = jnp.einsum("cpkj,pq->cpkqj", wt, eye)
    return tiles.reshape(n_blk // per, LRU_CW, LRU_CW)


def lru_branch(proj3, x_col0, gate_col0, conv_w, conv_b, wa, ba, wx, bx, lam, *, ts):
    b_, s_, _ = proj3.shape
    width = conv_w.shape[1]
    n_c = width // LRU_CW
    xc0 = x_col0 // LRU_CW
    gc0 = gate_col0 // LRU_CW
    row = lambda v: v.reshape(1, width).astype(F32)
    vec_spec = pl.BlockSpec((1, LRU_CW), lambda b, c, t: (0, c))
    return pl.pallas_call(
        functools.partial(_lru_body, ts=ts),
        grid=(b_, n_c, s_ // ts),
        in_specs=[
            pl.BlockSpec((None, ts, LRU_CW), lambda b, c, t: (b, t, xc0 + c)),
            pl.BlockSpec((None, ts, LRU_CW), lambda b, c, t: (b, t, gc0 + c)),
            pl.BlockSpec((CONV_W, LRU_CW), lambda b, c, t: (0, c)),
            vec_spec,
            pl.BlockSpec((None, LRU_CW, LRU_CW), lambda b, c, t: (c, 0, 0)),
            pl.BlockSpec((None, LRU_CW, LRU_CW), lambda b, c, t: (c, 0, 0)),
            vec_spec, vec_spec, vec_spec,
        ],
        out_specs=pl.BlockSpec((None, ts, LRU_CW), lambda b, c, t: (b, t, c)),
        out_shape=jax.ShapeDtypeStruct((b_, s_, width), BF16),
        scratch_shapes=[pltpu.VMEM((ts + SUBLANES, LRU_CW), F32), pltpu.VMEM((1, LRU_CW), F32)],
        compiler_params=_cparams(("parallel", "parallel", "arbitrary")),
        name="rg_lru",
    )(proj3, proj3, conv_w.astype(F32), row(conv_b), _block_diag_tiles(0.5 * wa).astype(BF16),
      _block_diag_tiles(0.5 * wx).astype(BF16), row(0.5 * ba), row(0.5 * bx), row(lam))


def _rel_bias_body(tab_ref, o_ref):
    h = pl.program_id(0)
    width = 1 << (ATT_HWIN + ATT_HALF - 1).bit_length()
    m = lax.broadcasted_iota(jnp.int32, (1, width), 1)
    delta = jnp.where(m < ATT_HWIN, m, m - width)
    idx = jnp.clip(ATT_LEFT - delta, -MAX_REL, MAX_REL) + MAX_REL

    def pick(d, acc):
        return jnp.where(idx == d, tab_ref[h, d], acc)

    gvec = lax.fori_loop(0, N_REL, pick, jnp.zeros((1, width), F32), unroll=N_REL)
    full = pltpu.roll(jnp.broadcast_to(gvec, (ATT_HALF, width)), 0, 1, stride=1, stride_axis=0)
    chunk_shift = int(math.log2(CHUNK))
    qc = lax.shift_right_logical(lax.broadcasted_iota(jnp.int32, (ATT_HALF, ATT_HWIN), 0), chunk_shift)
    kc = lax.shift_right_logical(lax.broadcasted_iota(jnp.int32, (ATT_HALF, ATT_HWIN), 1), chunk_shift)
    in_band = (kc >= qc) & (kc <= qc + ATT_LEFT_CHUNKS)
    o_ref[...] = jnp.where(in_band, full[:, :ATT_HWIN] * LOG2_E, MASK_VALUE)


def rel_bias_matrix(rel_bias):
    n = rel_bias.shape[0]
    return pl.pallas_call(
        _rel_bias_body,
        grid=(n,),
        in_specs=[pl.BlockSpec(memory_space=pltpu.SMEM)],
        out_specs=pl.BlockSpec((None, ATT_HALF, ATT_HWIN), lambda h: (h, 0, 0)),
        out_shape=jax.ShapeDtypeStruct((n, ATT_HALF, ATT_HWIN), F32),
        compiler_params=_cparams(("arbitrary",)),
        name="rel_bias",
    )(rel_bias.astype(F32))


def _attn_body(q_ref, k0_ref, k1_ref, k2_ref, v0_ref, v1_ref, v2_ref, bias_ref, o_ref):
    qb = pl.program_id(2)
    nt = (((1,), (1,)), ((), ()))
    chunk_shift = int(math.log2(CHUNK))
    kc = lax.shift_right_logical(lax.broadcasted_iota(jnp.int32, (1, ATT_HWIN), 1), chunk_shift)
    k_refs = (k0_ref, k1_ref, k2_ref)
    v_refs = (v0_ref, v1_ref, v2_ref)
    units = [(r0, h) for r0 in range(0, ATT_QBLK, ATT_HALF) for h in range(ATT_HEADS_PER_STEP)]
    parts_of = lambda r0: ((0, r0, ATT_QBLK), (1, 0, ATT_QBLK), (2, 0, r0 + ATT_HALF))
    def score(r0, h):
        cols = slice(h * ATT_HD, (h + 1) * ATT_HD)
        first_chunk = (qb - 2) * (ATT_QBLK // CHUNK) + r0 // CHUNK
        before_start = jnp.where(kc + first_chunk >= 0, 0.0, MASK_VALUE)
        q = q_ref[r0:r0 + ATT_HALF, cols]
        s = jnp.concatenate(
            [lax.dot_general(q, k_refs[blk][a:b, cols], nt, preferred_element_type=F32)
             for blk, a, b in parts_of(r0)], axis=1)
        return s * (ATT_HD ** -0.5 * LOG2_E) + bias_ref[h] + before_start

    def softmax_terms(s):
        p = jnp.exp2(s - jnp.max(s, axis=-1, keepdims=True))
        return p.astype(BF16), jnp.sum(p, axis=-1, keepdims=True)

    def weighted_values(r0, h, pb, l):
        cols = slice(h * ATT_HD, (h + 1) * ATT_HD)
        o = jnp.zeros((ATT_HALF, ATT_HD), F32)
        c0 = 0
        for blk, a, b in parts_of(r0):
            o = o + jnp.dot(pb[:, c0:c0 + b - a], v_refs[blk][a:b, cols], preferred_element_type=F32)
            c0 += b - a
        o_ref[r0:r0 + ATT_HALF, cols] = (o / l).astype(o_ref.dtype)

    scores = [score(*u) for u in units]
    probs = [softmax_terms(s) for s in scores]
    for u, (pb, l) in zip(units, probs):
        weighted_values(*u, pb, l)


def attention_branch(proj3, q_col0, k_col0, v_col0, bias, layer, n_heads):
    b_, s_, _ = proj3.shape
    hp = ATT_HEADS_PER_STEP
    wblk = hp * ATT_HD
    qc0, kc0, vc0 = q_col0 // wblk, k_col0 // wblk, v_col0 // wblk

    def kv_spec(col0, back):
        return pl.BlockSpec((None, ATT_QBLK, wblk),
                            lambda h, b, i: (b, jnp.maximum(i - back, 0), col0 + h))

    return pl.pallas_call(
        _attn_body,
        grid=(n_heads // hp, b_, s_ // ATT_QBLK),
        in_specs=[
            pl.BlockSpec((None, ATT_QBLK, wblk), lambda h, b, i: (b, i, qc0 + h)),
            kv_spec(kc0, 2), kv_spec(kc0, 1), kv_spec(kc0, 0),
            kv_spec(vc0, 2), kv_spec(vc0, 1), kv_spec(vc0, 0),
            pl.BlockSpec((hp, ATT_HALF, ATT_HWIN), lambda h, b, i: (layer * (n_heads // hp) + h, 0, 0)),
        ],
        out_specs=pl.BlockSpec((None, ATT_QBLK, wblk), lambda h, b, i: (b, i, h)),
        out_shape=jax.ShapeDtypeStruct((b_, s_, n_heads * ATT_HD), BF16),
        compiler_params=_cparams(("parallel", "parallel", "parallel")),
        name="chunk_attention",
    )(proj3, proj3, proj3, proj3, proj3, proj3, proj3, bias)


def _s5_prep_body(are_c, aim_c, ls_c, are_r, aim_r, ls_r, ctre, ctim, bta, btb, w_ref, m_ref, lam_ref):
    blk = S5_BLK
    tw = S5_GT * SSM_GROUP
    sw = S5_GT * 2 * SSM_P
    lg_grp, lg_p = int(math.log2(SSM_GROUP)), int(math.log2(SSM_P))
    group_of_state = lambda q: lax.shift_right_logical(q, lg_p) & (S5_GT - 1)

    def lam_bar(are, aim, ls):
        step = jnp.exp(ls)
        mag = jnp.exp(are * step)
        return mag * jnp.cos(aim * step), mag * jnp.sin(aim * step)

    hw = sw // 2
    lr_c, li_c = lam_bar(are_c[...], aim_c[...], ls_c[...])
    rowc = lax.broadcasted_iota(jnp.int32, (hw, tw), 0)
    lanec = lax.broadcasted_iota(jnp.int32, (hw, tw), 1)
    same_c = group_of_state(rowc) == lax.shift_right_logical(lanec, lg_grp)
    cre = jnp.where(same_c, ctre[...], 0.0)
    cim = jnp.where(same_c, ctim[...], 0.0)
    pr = jnp.ones((hw, tw), F32)
    pi = jnp.zeros((hw, tw), F32)
    cl_re, cl_im = [], []
    for _ in range(blk + 1):
        cl_re.append(cre * pr - cim * pi)
        cl_im.append(-(cre * pi + cim * pr))
        pr, pi = pr * lr_c - pi * li_c, pr * li_c + pi * lr_c
    m_ref[0:hw, :] = jnp.concatenate(cl_re[1:], axis=1).astype(BF16)
    m_ref[hw:sw, :] = jnp.concatenate(cl_im[1:], axis=1).astype(BF16)
    cl0 = jnp.concatenate([jnp.concatenate(cl_re[:blk], axis=1),
                           jnp.concatenate(cl_im[:blk], axis=1)], axis=0)

    ar, ai = are_r[...], aim_r[...]
    lr, li = lam_bar(ar, ai, ls_r[...])
    den = ar * ar + ai * ai
    nr = lr - 1.0
    cr = (nr * ar + li * ai) / den
    ci = (li * ar - nr * ai) / den
    rowr = lax.broadcasted_iota(jnp.int32, (tw, sw), 0)
    laner = lax.broadcasted_iota(jnp.int32, (tw, sw), 1)
    same_r = lax.shift_right_logical(rowr, lg_grp) == group_of_state(laner)
    ba = jnp.where(same_r, bta[...], 0.0)
    bb = jnp.where(same_r, btb[...], 0.0)
    bb2 = cr * ba + ci * bb
    bb2s = cr * bb - ci * ba

    kt = jnp.dot(bb2, cl0, preferred_element_type=F32, precision=lax.Precision.HIGHEST).astype(BF16)
    pr = jnp.ones((1, sw), F32)
    pi = jnp.zeros((1, sw), F32)
    for i in range(blk - 1, -1, -1):
        rows = slice(i * tw, (i + 1) * tw)
        if i:
            w_ref[rows, 0:i * tw] = jnp.zeros((tw, i * tw), BF16)
        w_ref[rows, i * tw:blk * tw] = kt[:, 0:(blk - i) * tw]
        w_ref[rows, blk * tw:] = (pr * bb2 + pi * bb2s).astype(BF16)
        pr, pi = pr * lr - pi * li, pr * li + pi * lr
    sign = jnp.where(lax.broadcasted_iota(jnp.int32, (1, sw), 1) >= sw // 2, 1.0, -1.0)
    lam_ref[...] = jnp.zeros_like(lam_ref)
    lam_ref[0:1, :] = pr
    lam_ref[1:2, :] = sign * pi


def s5_prepare(a_re, a_im, b_re, b_im, c_re, c_im, log_step):
    n_g, npp = a_re.shape
    grp = b_re.shape[2]
    blk, gt = S5_BLK, S5_GT
    n_t = n_g // gt
    tw, sw = gt * grp, gt * 2 * npp
    f = lambda v: v.astype(F32)
    pair = lambda v: jnp.concatenate([f(v).reshape(n_t, gt * npp)] * 2, axis=-1)
    hw = sw // 2
    as_col = lambda v: jnp.broadcast_to(f(v).reshape(n_t, hw, 1), (n_t, hw, tw))
    as_row = lambda v: v.reshape(n_t, 1, sw)
    ls = jnp.broadcast_to(f(log_step)[:, None], (n_g, npp))
    ct = lambda v: jnp.tile(f(v).transpose(0, 2, 1).reshape(n_t, hw, grp), (1, 1, gt))
    b_t = lambda v: jnp.broadcast_to(f(v).transpose(0, 2, 1).reshape(n_t, gt, grp, 1, npp),
                                     (n_t, gt, grp, gt, npp)).reshape(n_t, tw, gt * npp)
    btr, bti = b_t(b_re), b_t(b_im)
    bt = lambda lo, hi: jnp.concatenate([lo, hi], axis=-1)
    t3 = lambda shp: pl.BlockSpec((None,) + shp, lambda t: (t, 0, 0))
    return pl.pallas_call(
        _s5_prep_body,
        grid=(n_t,),
        in_specs=[t3((hw, tw)), t3((hw, tw)), t3((hw, tw)), t3((1, sw)), t3((1, sw)), t3((1, sw)),
                  t3((hw, tw)), t3((hw, tw)), t3((tw, sw)), t3((tw, sw))],
        out_specs=[t3((blk * tw, blk * tw + sw)), t3((sw, blk * tw)), t3((SUBLANES, sw))],
        out_shape=[jax.ShapeDtypeStruct((n_t, blk * tw, blk * tw + sw), BF16),
                   jax.ShapeDtypeStruct((n_t, sw, blk * tw), BF16),
                   jax.ShapeDtypeStruct((n_t, SUBLANES, sw), F32)],
        compiler_params=_cparams(("parallel",)),
        name="s5_prepare",
    )(as_col(a_re), as_col(a_im), as_col(ls), as_row(pair(a_re)), as_row(pair(a_im)),
      as_row(pair(ls)), ct(c_re), ct(c_im), bt(btr, bti), bt(-bti, btr))


def _s5_body(u_ref, w_ref, m_ref, lam_ref, d_ref, o_ref, *, rows):
    blk = S5_BLK
    tw = S5_GT * SSM_GROUP
    sw = lam_ref.shape[1]
    u_steps = [u_ref[pl.ds(i, rows, stride=blk), :] for i in range(blk)]
    lhs = jnp.concatenate([u.astype(BF16) for u in u_steps], axis=1)
    z = jnp.dot(lhs, w_ref[:, blk * tw:], preferred_element_type=F32)
    qw = blk * tw // 4
    r_q = [jnp.dot(lhs[:, 0:(q + 1) * qw], w_ref[0:(q + 1) * qw, q * qw:(q + 1) * qw],
                   preferred_element_type=F32) for q in range(4)]
    row = lax.broadcasted_iota(jnp.int32, (rows, sw), 0)
    ca = lam_ref[0:1, :]
    cb = lam_ref[1:2, :]
    s = 1
    while s < rows:
        zs = jnp.where(row >= s, pltpu.roll(z, s, 0), 0.0)
        z = z + ca * zs + cb * pltpu.roll(zs, sw // 2, 1)
        ca, cb = ca * ca - cb * cb, 2.0 * ca * cb
        s *= 2
    zprev = jnp.where(row >= 1, pltpu.roll(z, 1, 0), 0.0)
    ys = jnp.dot(zprev.astype(BF16), m_ref[...], preferred_element_type=F32)
    d = d_ref[...]
    for j in range(blk):
        q, off = divmod(j * tw, qw)
        y = r_q[q][:, off:off + tw] + ys[:, j * tw:(j + 1) * tw] + d * u_steps[j]
        o_ref[pl.ds(j, rows, stride=blk), :] = jax.nn.gelu(y)


def s5_branch(u, w, mm, lam, d, *, batch):
    t_, width = u.shape
    seq = t_ // batch
    n_t, sw = w.shape[0], lam.shape[2]
    tw = S5_GT * SSM_GROUP
    return pl.pallas_call(
        functools.partial(_s5_body, rows=seq // S5_BLK),
        grid=(n_t, batch),
        in_specs=[
            pl.BlockSpec((seq, tw), lambda c, b: (b, c)),
            pl.BlockSpec((None,) + w.shape[1:], lambda c, b: (c, 0, 0)),
            pl.BlockSpec((None,) + mm.shape[1:], lambda c, b: (c, 0, 0)),
            pl.BlockSpec((None, SUBLANES, sw), lambda c, b: (c, 0, 0)),
            pl.BlockSpec((1, tw), lambda c, b: (0, c)),
        ],
        out_specs=pl.BlockSpec((seq, tw), lambda c, b: (b, c)),
        out_shape=jax.ShapeDtypeStruct((t_, width), F32),
        compiler_params=_cparams(("parallel", "arbitrary")),
        name="s5_blocked",
    )(u, w, mm, lam, d)


def _merge_body(ya_ref, yb_ref, yc_ref, w0_ref, w1_ref, w2_ref, wg_ref,
                g0_ref, g1_ref, g2_ref, gb_ref, o_ref):
    ya, yb, yc = ya_ref[...], yb_ref[...], yc_ref[...].astype(BF16)
    half = o_ref.shape[1] // 2
    halves = (slice(0, half), slice(half, 2 * half))
    dot = lambda a, w, c: jnp.dot(a, w[:, c], preferred_element_type=F32)
    prods = [(dot(ya, w0_ref, c), dot(yb, w1_ref, c), dot(yc, w2_ref, c), dot(yc, wg_ref, c)) for c in halves]
    gate = lambda g_ref, i, c: _sigmoid(g_ref[:, c].astype(F32) + gb_ref[i:i + 1, c])
    for c, (br_a, br_b, br_c, glu) in zip(halves, prods):
        merged = (gate(g0_ref, 0, c) * br_a + gate(g1_ref, 1, c) * br_b
                  + gate(g2_ref, 2, c) * (br_c * _sigmoid(glu)))
        o_ref[:, c] = merged.astype(o_ref.dtype)


def branch_merge(ya, yb, yc, w_branch, w_glu, layer, proj, gate_col0, gate_bias, *, tm, tn):
    m, kdim = ya.shape
    n = w_glu.shape[2]
    gc0 = gate_col0 // tn
    nj = n // tn
    y_spec = pl.BlockSpec((tm, kdim), lambda i, j: (i, 0))
    wb_spec = lambda br: pl.BlockSpec((None, None, kdim, tn), lambda i, j: (layer, br, 0, j))
    g_spec = lambda br: pl.BlockSpec((tm, tn), lambda i, j: (i, gc0 + br * nj + j))
    return pl.pallas_call(
        _merge_body,
        grid=(m // tm, nj),
        in_specs=[y_spec, y_spec, y_spec, wb_spec(0), wb_spec(1), wb_spec(2),
                  pl.BlockSpec((None, kdim, tn), lambda i, j: (layer, 0, j)),
                  g_spec(0), g_spec(1), g_spec(2),
                  pl.BlockSpec((N_BRANCH, tn), lambda i, j: (0, j))],
        out_specs=pl.BlockSpec((tm, tn), lambda i, j: (i, j)),
        out_shape=jax.ShapeDtypeStruct((m, n), BF16),
        compiler_params=_cparams(("parallel", "arbitrary")),
        name="branch_merge",
    )(ya, yb, yc, w_branch, w_branch, w_branch, w_glu, proj, proj, proj, gate_bias.astype(F32))


def _matmul_res_body(a_ref, w_ref, r_ref, o_ref):
    o_ref[...] = r_ref[...] + jnp.dot(a_ref[...], w_ref[...], preferred_element_type=F32)


def matmul_residual(a, w, layer, res, *, tm, tn):
    m, k = a.shape
    n = w.shape[2]
    return pl.pallas_call(
        _matmul_res_body,
        grid=(m // tm, n // tn),
        in_specs=[pl.BlockSpec((tm, k), lambda i, j: (i, 0)),
                  pl.BlockSpec((None, k, tn), lambda i, j: (layer, 0, j)),
                  pl.BlockSpec((tm, tn), lambda i, j: (i, j))],
        out_specs=pl.BlockSpec((tm, tn), lambda i, j: (i, j)),
        out_shape=jax.ShapeDtypeStruct((m, n), F32),
        compiler_params=_cparams(("parallel", "arbitrary")),
        name="matmul_residual",
    )(a, w, res)


def _ffn_body(x_ref, g_ref, wg_ref, wu_ref, wd_ref, og_ref, o_ref, h_ref, *, norm_output):
    f = pl.program_id(1)

    @pl.when(f == 0)
    def _():
        x = x_ref[...]
        h_ref[...] = _rmsnorm_rows(x, g_ref[...]).astype(BF16)
        o_ref[...] = x

    h = h_ref[...]
    gate = jnp.dot(h, wg_ref[...], preferred_element_type=F32)
    up = jnp.dot(h, wu_ref[...], preferred_element_type=F32)
    act = (gate * _sigmoid(gate) * up).astype(BF16)
    for c in range(0, o_ref.shape[1], FFN_ACC_COLS):
        cols = slice(c, c + FFN_ACC_COLS)
        o_ref[:, cols] += jnp.dot(act, wd_ref[:, cols], preferred_element_type=F32)

    if norm_output:
        @pl.when(f == pl.num_programs(1) - 1)
        def _():
            o_ref[...] = _rmsnorm_rows(o_ref[...], og_ref[...])


def ffn(x, g, w_gate, w_up, w_down, layer, out_g, *, norm_output, tm, tf):
    m, d = x.shape
    hidden = w_gate.shape[2]
    return pl.pallas_call(
        functools.partial(_ffn_body, norm_output=norm_output),
        grid=(m // tm, hidden // tf),
        in_specs=[pl.BlockSpec((tm, d), lambda i, f: (i, 0)),
                  pl.BlockSpec((1, d), lambda i, f: (0, 0)),
                  pl.BlockSpec((None, d, tf), lambda i, f: (layer, 0, f)),
                  pl.BlockSpec((None, d, tf), lambda i, f: (layer, 0, f)),
                  pl.BlockSpec((None, tf, d), lambda i, f: (layer, f, 0)),
                  pl.BlockSpec((1, d), lambda i, f: (0, 0))],
        out_specs=pl.BlockSpec((tm, d), lambda i, f: (i, 0)),
        out_shape=jax.ShapeDtypeStruct((m, d), F32),
        scratch_shapes=[pltpu.VMEM((tm, d), BF16)],
        compiler_params=_cparams(("parallel", "arbitrary")),
        name="swiglu_ffn",
    )(x, g.reshape(1, d), w_gate, w_up, w_down, out_g.reshape(1, d))


def _pick_tile(n, want):
    t = min(n, want)
    while n % t:
        t //= 2
    return t


def kernel(x, norm_mix_g, w_in, gate_bias, lru_conv_w, lru_conv_b, lru_wa, lru_ba, lru_wx, lru_bx,
           lru_lambda, attn_rel_bias, ssm_a_re, ssm_a_im, ssm_b_re, ssm_b_im, ssm_c_re, ssm_c_im,
           ssm_d, ssm_log_step, ssm_w_glu, w_branch, w_out, norm_ffn_g, w_ffn_gate, w_ffn_up,
           w_ffn_down, norm_final_g):
    b_, s_, d_model = x.shape
    depth = w_in.shape[0]
    mix_w = lru_conv_w.shape[2]
    n_heads = attn_rel_bias.shape[1]
    n_g = ssm_a_re.shape[1]
    t_ = b_ * s_
    assert s_ % ATT_QBLK == 0 and s_ % S5_BLK == 0 and n_g % S5_GT == 0
    c_lru_x, c_lru_g, c_q, c_k, c_v, c_u, c_gate = [i * mix_w for i in range(7)]
    in_w = w_in.shape[2]

    tm = _pick_tile(t_, 1024)
    ts_lru = _pick_tile(s_, 2048)
    bias = rel_bias_matrix(attn_rel_bias.reshape(depth * n_heads, N_REL))
    w_in, w_branch, ssm_w_glu, w_out, w_ffn_gate, w_ffn_up, w_ffn_down = [
        w.astype(BF16) for w in (w_in, w_branch, ssm_w_glu, w_out, w_ffn_gate, w_ffn_up, w_ffn_down)]

    xf = x.reshape(t_, d_model).astype(F32)
    for l in range(depth):
        proj, u32 = norm_matmul(xf, norm_mix_g[l], w_in, l, tm=tm, tn=2 * mix_w, f32_col0=c_u, f32_width=mix_w)
        proj3 = proj.reshape(b_, s_, in_w)
        y_a = lru_branch(proj3, c_lru_x, c_lru_g, lru_conv_w[l], lru_conv_b[l], lru_wa[l], lru_ba[l],
                         lru_wx[l], lru_bx[l], lru_lambda[l], ts=ts_lru)
        y_b = attention_branch(proj3, c_q, c_k, c_v, bias, l, n_heads)

        w1, mm, lam = s5_prepare(ssm_a_re[l], ssm_a_im[l], ssm_b_re[l], ssm_b_im[l],
                                 ssm_c_re[l], ssm_c_im[l], ssm_log_step[l])
        y_c = s5_branch(u32, w1, mm, lam, ssm_d[l].astype(F32).reshape(1, mix_w), batch=b_)

        merged = branch_merge(y_a.reshape(t_, mix_w), y_b.reshape(t_, mix_w), y_c, w_branch, ssm_w_glu, l,
                              proj, c_gate, gate_bias[l], tm=tm, tn=512)
        xf = matmul_residual(merged, w_out, l, xf, tm=_pick_tile(t_, 512), tn=d_model)
        xf = ffn(xf, norm_ffn_g[l], w_ffn_gate, w_ffn_up, w_ffn_down, l, norm_final_g.astype(F32),
                 norm_output=(l == depth - 1), tm=tm, tf=512)
    return xf.reshape(b_, s_, d_model).astype(x.dtype)
```

```python
import functools
import math

import jax
import jax.numpy as jnp
from jax import lax
from jax.experimental import pallas as pl
from jax.experimental.pallas import tpu as pltpu

F32 = jnp.float32
BF16 = jnp.bfloat16

CHUNK = 64
LRU_BW = 64
CONV_W = 4
LRU_C = 8.0
ATT_HD = 128
ATT_LEFT_CHUNKS = 8
MAX_REL = 128
N_REL = 2 * MAX_REL + 1
SSM_GROUP = 16
SSM_P = 64
N_BRANCH = 3
NORM_EPS = 1e-6
MASK_VALUE = -1e30

LANES = 128
SUBLANES = 8
MXU_DIM = 256
VMEM_BYTES = 64 * 1024 * 1024
VMEM_LIMIT_BYTES = VMEM_BYTES - 4 * 1024 * 1024

S5_BLK = 16
S5_GT = LANES // SSM_GROUP
LOG2_E = math.log2(math.e)
ATT_HEADS_PER_STEP = 8
ATT_QBLK = 4 * CHUNK
ATT_HALF = ATT_QBLK // 2
ATT_LEFT = ATT_LEFT_CHUNKS * CHUNK
ATT_HWIN = ATT_HALF + ATT_LEFT
LRU_CW = MXU_DIM
FFN_ACC_COLS = 2 * MXU_DIM


def _cparams(sem):
    return pltpu.CompilerParams(dimension_semantics=sem, vmem_limit_bytes=VMEM_LIMIT_BYTES)


def _rmsnorm_rows(x, g):
    ms = jnp.mean(x * x, axis=-1, keepdims=True)
    return x * lax.rsqrt(ms + NORM_EPS) * g


def _norm_matmul_body(x_ref, g_ref, w_ref, o_ref, o32_ref, h_ref, *, f32_block, f32_off):
    j = pl.program_id(1)

    @pl.when(j == 0)
    def _():
        h_ref[...] = _rmsnorm_rows(x_ref[...], g_ref[...]).astype(BF16)

    h = h_ref[...]
    piece = o32_ref.shape[1]
    pieces = [slice(c, c + piece) for c in range(0, w_ref.shape[1], piece)]
    accs = [jnp.dot(h, w_ref[:, c], preferred_element_type=F32) for c in pieces]
    for c, acc in zip(pieces, accs):
        o_ref[:, c] = acc.astype(o_ref.dtype)

    @pl.when(j == f32_block)
    def _():
        o32_ref[...] = accs[f32_off // piece]


def norm_matmul(x, g, w, layer, *, tm, tn, f32_col0, f32_width):
    m, k = x.shape
    n = w.shape[2]
    assert tn % f32_width == 0 and f32_col0 % f32_width == 0
    return pl.pallas_call(
        functools.partial(_norm_matmul_body, f32_block=f32_col0 // tn, f32_off=f32_col0 % tn),
        grid=(m // tm, n // tn),
        in_specs=[
            pl.BlockSpec((tm, k), lambda i, j: (i, 0)),
            pl.BlockSpec((1, k), lambda i, j: (0, 0)),
            pl.BlockSpec((None, k, tn), lambda i, j: (layer, 0, j)),
        ],
        out_specs=[pl.BlockSpec((tm, tn), lambda i, j: (i, j)),
                   pl.BlockSpec((tm, f32_width), lambda i, j: (i, 0))],
        out_shape=[jax.ShapeDtypeStruct((m, n), BF16), jax.ShapeDtypeStruct((m, f32_width), F32)],
        scratch_shapes=[pltpu.VMEM((tm, k), BF16)],
        compiler_params=_cparams(("parallel", "arbitrary")),
        name="norm_matmul",
    )(x, g.reshape(1, k), w)


def _softplus(x):
    return jnp.maximum(x, 0.0) + jnp.log1p(jnp.exp(-jnp.abs(x)))


def _sigmoid(x):
    return 0.5 * jnp.tanh(0.5 * x) + 0.5


def _lru_body(x_ref, gate_ref, cw_ref, cb_ref, wa_ref, wx_ref, ba_ref, bx_ref, lam_ref,
              o_ref, xbuf_ref, h_ref, *, ts):
    t = pl.program_id(2)

    @pl.when(t == 0)
    def _():
        xbuf_ref[0:SUBLANES, :] = jnp.zeros((SUBLANES, LRU_CW), F32)
        h_ref[...] = jnp.zeros_like(h_ref)

    x = x_ref[...].astype(F32)
    xbuf_ref[SUBLANES:SUBLANES + ts, :] = x
    xc = cb_ref[...] + x * cw_ref[CONV_W - 1:CONV_W, :]
    for k in range(CONV_W - 1):
        back = CONV_W - 1 - k
        xc = xc + xbuf_ref[pl.ds(SUBLANES - back, ts), :] * cw_ref[k:k + 1, :]
    xbuf_ref[0:SUBLANES, :] = x[ts - SUBLANES:ts, :]

    xcb = xc.astype(BF16)
    ta = jnp.tanh(jnp.dot(xcb, wa_ref[...], preferred_element_type=F32) + ba_ref[...])
    tx = jnp.tanh(jnp.dot(xcb, wx_ref[...], preferred_element_type=F32) + bx_ref[...])
    c1 = (-0.25 * LRU_C) * _softplus(-lam_ref[...])
    th = jnp.tanh(c1 * ta + c1)
    rinv = 1.0 / (1.0 - th)
    a = (1.0 + th) * rinv
    xh = 0.5 * xc
    b = (2.0 * rinv) * jnp.sqrt(-th) * (xh * tx + xh)

    nt = ts // SUBLANES
    a = a.reshape(nt, SUBLANES, LRU_CW)
    b = b.reshape(nt, SUBLANES, LRU_CW)
    sub = lax.broadcasted_iota(jnp.int32, (nt, SUBLANES, LRU_CW), 1)
    s = 1
    while s < SUBLANES:
        keep = sub >= s
        b_sh = pltpu.roll(b, s, 1)
        a_sh = pltpu.roll(a, s, 1)
        b = b + jnp.where(keep, a * b_sh, 0.0)
        a = jnp.where(keep, a * a_sh, a)
        s *= 2
    carry = h_ref[...]
    tiles = []
    for k in range(nt):
        hk = b[k] + a[k] * carry
        tiles.append(hk)
        carry = hk[SUBLANES - 1:SUBLANES, :]
    h_ref[...] = carry
    h = jnp.concatenate(tiles, axis=0)
    o_ref[...] = (h * jax.nn.gelu(gate_ref[...].astype(F32))).astype(o_ref.dtype)


def _block_diag_tiles(w):
    n_blk, bw, _ = w.shape
    per = LRU_CW // bw
    wt = w.reshape(n_blk // per, per, bw, bw)
    eye = jnp.eye(per, dtype=w.dtype)
    tiles = jnp.einsum("cpkj,pq->cpkqj", wt, eye)
    return tiles.reshape(n_blk // per, LRU_CW, LRU_CW)


def lru_branch(proj3, x_col0, gate_col0, conv_w, conv_b, wa, ba, wx, bx, lam, *, ts):
    b_, s_, _ = proj3.shape
    width = conv_w.shape[1]
    n_c = width // LRU_CW
    xc0 = x_col0 // LRU_CW
    gc0 = gate_col0 // LRU_CW
    row = lambda v: v.reshape(1, width).astype(F32)
    vec_spec = pl.BlockSpec((1, LRU_CW), lambda b, c, t: (0, c))
    return pl.pallas_call(
        functools.partial(_lru_body, ts=ts),
        grid=(b_, n_c, s_ // ts),
        in_specs=[
            pl.BlockSpec((None, ts, LRU_CW), lambda b, c, t: (b, t, xc0 + c)),
            pl.BlockSpec((None, ts, LRU_CW), lambda b, c, t: (b, t, gc0 + c)),
            pl.BlockSpec((CONV_W, LRU_CW), lambda b, c, t: (0, c)),
            vec_spec,
            pl.BlockSpec((None, LRU_CW, LRU_CW), lambda b, c, t: (c, 0, 0)),
            pl.BlockSpec((None, LRU_CW, LRU_CW), lambda b, c, t: (c, 0, 0)),
            vec_spec, vec_spec, vec_spec,
        ],
        out_specs=pl.BlockSpec((None, ts, LRU_CW), lambda b, c, t: (b, t, c)),
        out_shape=jax.ShapeDtypeStruct((b_, s_, width), BF16),
        scratch_shapes=[pltpu.VMEM((ts + SUBLANES, LRU_CW), F32), pltpu.VMEM((1, LRU_CW), F32)],
        compiler_params=_cparams(("parallel", "parallel", "arbitrary")),
        name="rg_lru",
    )(proj3, proj3, conv_w.astype(F32), row(conv_b), _block_diag_tiles(0.5 * wa).astype(BF16),
      _block_diag_tiles(0.5 * wx).astype(BF16), row(0.5 * ba), row(0.5 * bx), row(lam))


def _rel_bias_body(tab_ref, o_ref):
    h = pl.program_id(0)
    width = 1 << (ATT_HWIN + ATT_HALF - 1).bit_length()
    m = lax.broadcasted_iota(jnp.int32, (1, width), 1)
    delta = jnp.where(m < ATT_HWIN, m, m - width)
    idx = jnp.clip(ATT_LEFT - delta, -MAX_REL, MAX_REL) + MAX_REL

    def pick(d, acc):
        return jnp.where(idx == d, tab_ref[h, d], acc)

    gvec = lax.fori_loop(0, N_REL, pick, jnp.zeros((1, width), F32), unroll=N_REL)
    full = pltpu.roll(jnp.broadcast_to(gvec, (ATT_HALF, width)), 0, 1, stride=1, stride_axis=0)
    chunk_shift = int(math.log2(CHUNK))
    qc = lax.shift_right_logical(lax.broadcasted_iota(jnp.int32, (ATT_HALF, ATT_HWIN), 0), chunk_shift)
    kc = lax.shift_right_logical(lax.broadcasted_iota(jnp.int32, (ATT_HALF, ATT_HWIN), 1), chunk_shift)
    in_band = (kc >= qc) & (kc <= qc + ATT_LEFT_CHUNKS)
    o_ref[...] = jnp.where(in_band, full[:, :ATT_HWIN] * LOG2_E, MASK_VALUE)


def rel_bias_matrix(rel_bias):
    n = rel_bias.shape[0]
    return pl.pallas_call(
        _rel_bias_body,
        grid=(n,),
        in_specs=[pl.BlockSpec(memory_space=pltpu.SMEM)],
        out_specs=pl.BlockSpec((None, ATT_HALF, ATT_HWIN), lambda h: (h, 0, 0)),
        out_shape=jax.ShapeDtypeStruct((n, ATT_HALF, ATT_HWIN), F32),
        compiler_params=_cparams(("arbitrary",)),
        name="rel_bias",
    )(rel_bias.astype(F32))


def _attn_body(q_ref, k0_ref, k1_ref, k2_ref, v0_ref, v1_ref, v2_ref, bias_ref, o_ref):
    qb = pl.program_id(2)
    nt = (((1,), (1,)), ((), ()))
    chunk_shift = int(math.log2(CHUNK))
    kc = lax.shift_right_logical(lax.broadcasted_iota(jnp.int32, (1, ATT_HWIN), 1), chunk_shift)
    k_refs = (k0_ref, k1_ref, k2_ref)
    v_refs = (v0_ref, v1_ref, v2_ref)
    units = [(r0, h) for r0 in range(0, ATT_QBLK, ATT_HALF) for h in range(ATT_HEADS_PER_STEP)]
    parts_of = lambda r0: ((0, r0, ATT_QBLK), (1, 0, ATT_QBLK), (2, 0, r0 + ATT_HALF))
    def score(r0, h):
        cols = slice(h * ATT_HD, (h + 1) * ATT_HD)
        first_chunk = (qb - 2) * (ATT_QBLK // CHUNK) + r0 // CHUNK
        before_start = jnp.where(kc + first_chunk >= 0, 0.0, MASK_VALUE)
        q = q_ref[r0:r0 + ATT_HALF, cols]
        s = jnp.concatenate(
            [lax.dot_general(q, k_refs[blk][a:b, cols], nt, preferred_element_type=F32)
             for blk, a, b in parts_of(r0)], axis=1)
        return s * (ATT_HD ** -0.5 * LOG2_E) + bias_ref[h] + before_start

    def softmax_terms(s):
        p = jnp.exp2(s - jnp.max(s, axis=-1, keepdims=True))
        return p.astype(BF16), jnp.sum(p, axis=-1, keepdims=True)

    def weighted_values(r0, h, pb, l):
        cols = slice(h * ATT_HD, (h + 1) * ATT_HD)
        o = jnp.zeros((ATT_HALF, ATT_HD), F32)
        c0 = 0
        for blk, a, b in parts_of(r0):
            o = o + jnp.dot(pb[:, c0:c0 + b - a], v_refs[blk][a:b, cols], preferred_element_type=F32)
            c0 += b - a
        o_ref[r0:r0 + ATT_HALF, cols] = (o / l).astype(o_ref.dtype)

    scores = [score(*u) for u in units]
    probs = [softmax_terms(s) for s in scores]
    for u, (pb, l) in zip(units, probs):
        weighted_values(*u, pb, l)


def attention_branch(proj3, q_col0, k_col0, v_col0, bias, layer, n_heads):
    b_, s_, _ = proj3.shape
    hp = ATT_HEADS_PER_STEP
    wblk = hp * ATT_HD
    qc0, kc0, vc0 = q_col0 // wblk, k_col0 // wblk, v_col0 // wblk

    def kv_spec(col0, back):
        return pl.BlockSpec((None, ATT_QBLK, wblk),
                            lambda h, b, i: (b, jnp.maximum(i - back, 0), col0 + h))

    return pl.pallas_call(
        _attn_body,
        grid=(n_heads // hp, b_, s_ // ATT_QBLK),
        in_specs=[
            pl.BlockSpec((None, ATT_QBLK, wblk), lambda h, b, i: (b, i, qc0 + h)),
            kv_spec(kc0, 2), kv_spec(kc0, 1), kv_spec(kc0, 0),
            kv_spec(vc0, 2), kv_spec(vc0, 1), kv_spec(vc0, 0),
            pl.BlockSpec((hp, ATT_HALF, ATT_HWIN), lambda h, b, i: (layer * (n_heads // hp) + h, 0, 0)),
        ],
        out_specs=pl.BlockSpec((None, ATT_QBLK, wblk), lambda h, b, i: (b, i, h)),
        out_shape=jax.ShapeDtypeStruct((b_, s_, n_heads * ATT_HD), BF16),
        compiler_params=_cparams(("parallel", "parallel", "parallel")),
        name="chunk_attention",
    )(proj3, proj3, proj3, proj3, proj3, proj3, proj3, bias)


def _s5_prep_body(are_c, aim_c, ls_c, are_r, aim_r, ls_r, ctre, ctim, bta, btb, w_ref, m_ref, lam_ref):
    blk = S5_BLK
    tw = S5_GT * SSM_GROUP
    sw = S5_GT * 2 * SSM_P
    lg_grp, lg_p = int(math.log2(SSM_GROUP)), int(math.log2(SSM_P))
    group_of_state = lambda q: lax.shift_right_logical(q, lg_p) & (S5_GT - 1)

    def lam_bar(are, aim, ls):
        step = jnp.exp(ls)
        mag = jnp.exp(are * step)
        return mag * jnp.cos(aim * step), mag * jnp.sin(aim * step)

    hw = sw // 2
    lr_c, li_c = lam_bar(are_c[...], aim_c[...], ls_c[...])
    rowc = lax.broadcasted_iota(jnp.int32, (hw, tw), 0)
    lanec = lax.broadcasted_iota(jnp.int32, (hw, tw), 1)
    same_c = group_of_state(rowc) == lax.shift_right_logical(lanec, lg_grp)
    cre = jnp.where(same_c, ctre[...], 0.0)
    cim = jnp.where(same_c, ctim[...], 0.0)
    pr = jnp.ones((hw, tw), F32)
    pi = jnp.zeros((hw, tw), F32)
    cl_re, cl_im = [], []
    for _ in range(blk + 1):
        cl_re.append(cre * pr - cim * pi)
        cl_im.append(-(cre * pi + cim * pr))
        pr, pi = pr * lr_c - pi * li_c, pr * li_c + pi * lr_c
    m_ref[0:hw, :] = jnp.concatenate(cl_re[1:], axis=1).astype(BF16)
    m_ref[hw:sw, :] = jnp.concatenate(cl_im[1:], axis=1).astype(BF16)
    cl0 = jnp.concatenate([jnp.concatenate(cl_re[:blk], axis=1),
                           jnp.concatenate(cl_im[:blk], axis=1)], axis=0)

    ar, ai = are_r[...], aim_r[...]
    lr, li = lam_bar(ar, ai, ls_r[...])
    den = ar * ar + ai * ai
    nr = lr - 1.0
    cr = (nr * ar + li * ai) / den
    ci = (li * ar - nr * ai) / den
    rowr = lax.broadcasted_iota(jnp.int32, (tw, sw), 0)
    laner = lax.broadcasted_iota(jnp.int32, (tw, sw), 1)
    same_r = lax.shift_right_logical(rowr, lg_grp) == group_of_state(laner)
    ba = jnp.where(same_r, bta[...], 0.0)
    bb = jnp.where(same_r, btb[...], 0.0)
    bb2 = cr * ba + ci * bb
    bb2s = cr * bb - ci * ba

    kt = jnp.dot(bb2, cl0, preferred_element_type=F32, precision=lax.Precision.HIGHEST).astype(BF16)
    pr = jnp.ones((1, sw), F32)
    pi = jnp.zeros((1, sw), F32)
    for i in range(blk - 1, -1, -1):
        rows = slice(i * tw, (i + 1) * tw)
        if i:
            w_ref[rows, 0:i * tw] = jnp.zeros((tw, i * tw), BF16)
        w_ref[rows, i * tw:blk * tw] = kt[:, 0:(blk - i) * tw]
        w_ref[rows, blk * tw:] = (pr * bb2 + pi * bb2s).astype(BF16)
        pr, pi = pr * lr - pi * li, pr * li + pi * lr
    sign = jnp.where(lax.broadcasted_iota(jnp.int32, (1, sw), 1) >= sw // 2, 1.0, -1.0)
    lam_ref[...] = jnp.zeros_like(lam_ref)
    lam_ref[0:1, :] = pr
    lam_ref[1:2, :] = sign * pi


def s5_prepare(a_re, a_im, b_re, b_im, c_re, c_im, log_step):
    n_g, npp = a_re.shape
    grp = b_re.shape[2]
    blk, gt = S5_BLK, S5_GT
    n_t = n_g // gt
    tw, sw = gt * grp, gt * 2 * npp
    f = lambda v: v.astype(F32)
    pair = lambda v: jnp.concatenate([f(v).reshape(n_t, gt * npp)] * 2, axis=-1)
    hw = sw // 2
    as_col = lambda v: jnp.broadcast_to(f(v).reshape(n_t, hw, 1), (n_t, hw, tw))
    as_row = lambda v: v.reshape(n_t, 1, sw)
    ls = jnp.broadcast_to(f(log_step)[:, None], (n_g, npp))
    ct = lambda v: jnp.tile(f(v).transpose(0, 2, 1).reshape(n_t, hw, grp), (1, 1, gt))
    b_t = lambda v: jnp.broadcast_to(f(v).transpose(0, 2, 1).reshape(n_t, gt, grp, 1, npp),
                                     (n_t, gt, grp, gt, npp)).reshape(n_t, tw, gt * npp)
    btr, bti = b_t(b_re), b_t(b_im)
    bt = lambda lo, hi: jnp.concatenate([lo, hi], axis=-1)
    t3 = lambda shp: pl.BlockSpec((None,) + shp, lambda t: (t, 0, 0))
    return pl.pallas_call(
        _s5_prep_body,
        grid=(n_t,),
        in_specs=[t3((hw, tw)), t3((hw, tw)), t3((hw, tw)), t3((1, sw)), t3((1, sw)), t3((1, sw)),
                  t3((hw, tw)), t3((hw, tw)), t3((tw, sw)), t3((tw, sw))],
        out_specs=[t3((blk * tw, blk * tw + sw)), t3((sw, blk * tw)), t3((SUBLANES, sw))],
        out_shape=[jax.ShapeDtypeStruct((n_t, blk * tw, blk * tw + sw), BF16),
                   jax.ShapeDtypeStruct((n_t, sw, blk * tw), BF16),
                   jax.ShapeDtypeStruct((n_t, SUBLANES, sw), F32)],
        compiler_params=_cparams(("parallel",)),
        name="s5_prepare",
    )(as_col(a_re), as_col(a_im), as_col(ls), as_row(pair(a_re)), as_row(pair(a_im)),
      as_row(pair(ls)), ct(c_re), ct(c_im), bt(btr, bti), bt(-bti, btr))


def _s5_body(u_ref, w_ref, m_ref, lam_ref, d_ref, o_ref, *, rows):
    blk = S5_BLK
    tw = S5_GT * SSM_GROUP
    sw = lam_ref.shape[1]
    u_steps = [u_ref[pl.ds(i, rows, stride=blk), :] for i in range(blk)]
    lhs = jnp.concatenate([u.astype(BF16) for u in u_steps], axis=1)
    z = jnp.dot(lhs, w_ref[:, blk * tw:], preferred_element_type=F32)
    qw = blk * tw // 4
    r_q = [jnp.dot(lhs[:, 0:(q + 1) * qw], w_ref[0:(q + 1) * qw, q * qw:(q + 1) * qw],
                   preferred_element_type=F32) for q in range(4)]
    row = lax.broadcasted_iota(jnp.int32, (rows, sw), 0)
    ca = lam_ref[0:1, :]
    cb = lam_ref[1:2, :]
    s = 1
    while s < rows:
        zs = jnp.where(row >= s, pltpu.roll(z, s, 0), 0.0)
        z = z + ca * zs + cb * pltpu.roll(zs, sw // 2, 1)
        ca, cb = ca * ca - cb * cb, 2.0 * ca * cb
        s *= 2
    zprev = jnp.where(row >= 1, pltpu.roll(z, 1, 0), 0.0)
    ys = jnp.dot(zprev.astype(BF16), m_ref[...], preferred_element_type=F32)
    d = d_ref[...]
    for j in range(blk):
        q, off = divmod(j * tw, qw)
        y = r_q[q][:, off:off + tw] + ys[:, j * tw:(j + 1) * tw] + d * u_steps[j]
        o_ref[pl.ds(j, rows, stride=blk), :] = jax.nn.gelu(y)


def s5_branch(u, w, mm, lam, d, *, batch):
    t_, width = u.shape
    seq = t_ // batch
    n_t, sw = w.shape[0], lam.shape[2]
    tw = S5_GT * SSM_GROUP
    return pl.pallas_call(
        functools.partial(_s5_body, rows=seq // S5_BLK),
        grid=(n_t, batch),
        in_specs=[
            pl.BlockSpec((seq, tw), lambda c, b: (b, c)),
            pl.BlockSpec((None,) + w.shape[1:], lambda c, b: (c, 0, 0)),
            pl.BlockSpec((None,) + mm.shape[1:], lambda c, b: (c, 0, 0)),
            pl.BlockSpec((None, SUBLANES, sw), lambda c, b: (c, 0, 0)),
            pl.BlockSpec((1, tw), lambda c, b: (0, c)),
        ],
        out_specs=pl.BlockSpec((seq, tw), lambda c, b: (b, c)),
        out_shape=jax.ShapeDtypeStruct((t_, width), F32),
        compiler_params=_cparams(("parallel", "arbitrary")),
        name="s5_blocked",
    )(u, w, mm, lam, d)


def _merge_body(ya_ref, yb_ref, yc_ref, w0_ref, w1_ref, w2_ref, wg_ref,
                g0_ref, g1_ref, g2_ref, gb_ref, o_ref):
    hm, hn = o_ref.shape[0] // 2, o_ref.shape[1] // 2
    pieces = [(slice(r, r + hm), slice(c, c + hn)) for r in (0, hm) for c in (0, hn)]
    dot = lambda a_ref, w, r, c: jnp.dot(a_ref[r, :].astype(BF16), w[:, c], preferred_element_type=F32)
    prods = [(dot(ya_ref, w0_ref, r, c), dot(yb_ref, w1_ref, r, c), dot(yc_ref, w2_ref, r, c),
              dot(yc_ref, wg_ref, r, c)) for r, c in pieces]
    gate = lambda g_ref, i, r, c: _sigmoid(g_ref[r, c].astype(F32) + gb_ref[i:i + 1, c])
    for (r, c), (br_a, br_b, br_c, glu) in zip(pieces, prods):
        merged = (gate(g0_ref, 0, r, c) * br_a + gate(g1_ref, 1, r, c) * br_b
                  + gate(g2_ref, 2, r, c) * (br_c * _sigmoid(glu)))
        o_ref[r, c] = merged.astype(o_ref.dtype)


def branch_merge(ya, yb, yc, w_branch, w_glu, layer, proj, gate_col0, gate_bias, *, tm, tn):
    m, kdim = ya.shape
    n = w_glu.shape[2]
    gc0 = gate_col0 // tn
    nj = n // tn
    y_spec = pl.BlockSpec((tm, kdim), lambda i, j: (i, 0))
    wb_spec = lambda br: pl.BlockSpec((None, None, kdim, tn), lambda i, j: (layer, br, 0, j))
    g_spec = lambda br: pl.BlockSpec((tm, tn), lambda i, j: (i, gc0 + br * nj + j))
    return pl.pallas_call(
        _merge_body,
        grid=(m // tm, nj),
        in_specs=[y_spec, y_spec, y_spec, wb_spec(0), wb_spec(1), wb_spec(2),
                  pl.BlockSpec((None, kdim, tn), lambda i, j: (layer, 0, j)),
                  g_spec(0), g_spec(1), g_spec(2),
                  pl.BlockSpec((N_BRANCH, tn), lambda i, j: (0, j))],
        out_specs=pl.BlockSpec((tm, tn), lambda i, j: (i, j)),
        out_shape=jax.ShapeDtypeStruct((m, n), BF16),
        compiler_params=_cparams(("parallel", "arbitrary")),
        name="branch_merge",
    )(ya, yb, yc, w_branch, w_branch, w_branch, w_glu, proj, proj, proj, gate_bias.astype(F32))


def _matmul_res_body(a_ref, w_ref, r_ref, o_ref):
    o_ref[...] = r_ref[...] + jnp.dot(a_ref[...], w_ref[...], preferred_element_type=F32)


def matmul_residual(a, w, layer, res, *, tm, tn):
    m, k = a.shape
    n = w.shape[2]
    return pl.pallas_call(
        _matmul_res_body,
        grid=(m // tm, n // tn),
        in_specs=[pl.BlockSpec((tm, k), lambda i, j: (i, 0)),
                  pl.BlockSpec((None, k, tn), lambda i, j: (layer, 0, j)),
                  pl.BlockSpec((tm, tn), lambda i, j: (i, j))],
        out_specs=pl.BlockSpec((tm, tn), lambda i, j: (i, j)),
        out_shape=jax.ShapeDtypeStruct((m, n), F32),
        compiler_params=_cparams(("parallel", "arbitrary")),
        name="matmul_residual",
    )(a, w, res)


def _ffn_body(x_ref, g_ref, wg_ref, wu_ref, wd_ref, og_ref, o_ref, h_ref, *, norm_output):
    f = pl.program_id(1)

    @pl.when(f == 0)
    def _():
        x = x_ref[...]
        h_ref[...] = _rmsnorm_rows(x, g_ref[...]).astype(BF16)
        o_ref[...] = x

    h = h_ref[...]
    gate = jnp.dot(h, wg_ref[...], preferred_element_type=F32)
    up = jnp.dot(h, wu_ref[...], preferred_element_type=F32)
    act = (gate * _sigmoid(gate) * up).astype(BF16)
    for c in range(0, o_ref.shape[1], FFN_ACC_COLS):
        cols = slice(c, c + FFN_ACC_COLS)
        o_ref[:, cols] += jnp.dot(act, wd_ref[:, cols], preferred_element_type=F32)

    if norm_output:
        @pl.when(f == pl.num_programs(1) - 1)
        def _():
            o_ref[...] = _rmsnorm_rows(o_ref[...], og_ref[...])


def ffn(x, g, w_gate, w_up, w_down, layer, out_g, *, norm_output, tm, tf):
    m, d = x.shape
    hidden = w_gate.shape[2]
    return pl.pallas_call(
        functools.partial(_ffn_body, norm_output=norm_output),
        grid=(m // tm, hidden // tf),
        in_specs=[pl.BlockSpec((tm, d), lambda i, f: (i, 0)),
                  pl.BlockSpec((1, d), lambda i, f: (0, 0)),
                  pl.BlockSpec((None, d, tf), lambda i, f: (layer, 0, f)),
                  pl.BlockSpec((None, d, tf), lambda i, f: (layer, 0, f)),
                  pl.BlockSpec((None, tf, d), lambda i, f: (layer, f, 0)),
                  pl.BlockSpec((1, d), lambda i, f: (0, 0))],
        out_specs=pl.BlockSpec((tm, d), lambda i, f: (i, 0)),
        out_shape=jax.ShapeDtypeStruct((m, d), F32),
        scratch_shapes=[pltpu.VMEM((tm, d), BF16)],
        compiler_params=_cparams(("parallel", "arbitrary")),
        name="swiglu_ffn",
    )(x, g.reshape(1, d), w_gate, w_up, w_down, out_g.reshape(1, d))


def _pick_tile(n, want):
    t = min(n, want)
    while n % t:
        t //= 2
    return t


def kernel(x, norm_mix_g, w_in, gate_bias, lru_conv_w, lru_conv_b, lru_wa, lru_ba, lru_wx, lru_bx,
           lru_lambda, attn_rel_bias, ssm_a_re, ssm_a_im, ssm_b_re, ssm_b_im, ssm_c_re, ssm_c_im,
           ssm_d, ssm_log_step, ssm_w_glu, w_branch, w_out, norm_ffn_g, w_ffn_gate, w_ffn_up,
           w_ffn_down, norm_final_g):
    b_, s_, d_model = x.shape
    depth = w_in.shape[0]
    mix_w = lru_conv_w.shape[2]
    n_heads = attn_rel_bias.shape[1]
    n_g = ssm_a_re.shape[1]
    t_ = b_ * s_
    assert s_ % ATT_QBLK == 0 and s_ % S5_BLK == 0 and n_g % S5_GT == 0
    c_lru_x, c_lru_g, c_q, c_k, c_v, c_u, c_gate = [i * mix_w for i in range(7)]
    in_w = w_in.shape[2]

    tm = _pick_tile(t_, 1024)
    ts_lru = _pick_tile(s_, 2048)
    bias = rel_bias_matrix(attn_rel_bias.reshape(depth * n_heads, N_REL))
    w_in, w_branch, ssm_w_glu, w_out, w_ffn_gate, w_ffn_up, w_ffn_down = [
        w.astype(BF16) for w in (w_in, w_branch, ssm_w_glu, w_out, w_ffn_gate, w_ffn_up, w_ffn_down)]

    xf = x.reshape(t_, d_model).astype(F32)
    for l in range(depth):
        proj, u32 = norm_matmul(xf, norm_mix_g[l], w_in, l, tm=tm, tn=2 * mix_w, f32_col0=c_u, f32_width=mix_w)
        proj3 = proj.reshape(b_, s_, in_w)
        y_a = lru_branch(proj3, c_lru_x, c_lru_g, lru_conv_w[l], lru_conv_b[l], lru_wa[l], lru_ba[l],
                         lru_wx[l], lru_bx[l], lru_lambda[l], ts=ts_lru)
        y_b = attention_branch(proj3, c_q, c_k, c_v, bias, l, n_heads)

        w1, mm, lam = s5_prepare(ssm_a_re[l], ssm_a_im[l], ssm_b_re[l], ssm_b_im[l],
                                 ssm_c_re[l], ssm_c_im[l], ssm_log_step[l])
        y_c = s5_branch(u32, w1, mm, lam, ssm_d[l].astype(F32).reshape(1, mix_w), batch=b_)

        merged = branch_merge(y_a.reshape(t_, mix_w), y_b.reshape(t_, mix_w), y_c, w_branch, ssm_w_glu, l,
                              proj, c_gate, gate_bias[l], tm=tm, tn=512)
        xf = matmul_residual(merged, w_out, l, xf, tm=_pick_tile(t_, 512), tn=d_model)
        xf = ffn(xf, norm_ffn_g[l], w_ffn_gate, w_ffn_up, w_ffn_down, l, norm_final_g.astype(F32),
                 norm_output=(l == depth - 1), tm=tm, tf=512)
    return xf.reshape(b_, s_, d_model).astype(x.dtype)
```

```python
import functools
import math

import jax
import jax.numpy as jnp
from jax import lax
from jax.experimental import pallas as pl
from jax.experimental.pallas import tpu as pltpu

F32 = jnp.float32
BF16 = jnp.bfloat16

CHUNK = 64
LRU_BW = 64
CONV_W = 4
LRU_C = 8.0
ATT_HD = 128
ATT_LEFT_CHUNKS = 8
MAX_REL = 128
N_REL = 2 * MAX_REL + 1
SSM_GROUP = 16
SSM_P = 64
N_BRANCH = 3
NORM_EPS = 1e-6
MASK_VALUE = -1e30

LANES = 128
SUBLANES = 8
MXU_DIM = 256
VMEM_BYTES = 64 * 1024 * 1024
VMEM_LIMIT_BYTES = VMEM_BYTES - 4 * 1024 * 1024

S5_BLK = 16
S5_GT = LANES // SSM_GROUP
LOG2_E = math.log2(math.e)
ATT_HEADS_PER_STEP = 8
ATT_QBLK = 4 * CHUNK
ATT_HALF = ATT_QBLK // 2
ATT_LEFT = ATT_LEFT_CHUNKS * CHUNK
ATT_HWIN = ATT_HALF + ATT_LEFT
LRU_CW = MXU_DIM
FFN_ACC_COLS = 2 * MXU_DIM


def _cparams(sem):
    return pltpu.CompilerParams(dimension_semantics=sem, vmem_limit_bytes=VMEM_LIMIT_BYTES)


def _rmsnorm_rows(x, g):
    ms = jnp.mean(x * x, axis=-1, keepdims=True)
    return x * lax.rsqrt(ms + NORM_EPS) * g


def _norm_matmul_body(x_ref, g_ref, w_ref, o_ref, o32_ref, h_ref, *, f32_block, f32_off):
    j = pl.program_id(1)

    @pl.when(j == 0)
    def _():
        h_ref[...] = _rmsnorm_rows(x_ref[...], g_ref[...]).astype(BF16)

    h = h_ref[...]
    piece = o32_ref.shape[1]
    pieces = [slice(c, c + piece) for c in range(0, w_ref.shape[1], piece)]
    accs = [jnp.dot(h, w_ref[:, c], preferred_element_type=F32) for c in pieces]
    for c, acc in zip(pieces, accs):
        o_ref[:, c] = acc.astype(o_ref.dtype)

    @pl.when(j == f32_block)
    def _():
        o32_ref[...] = accs[f32_off // piece]


def norm_matmul(x, g, w, layer, *, tm, tn, f32_col0, f32_width):
    m, k = x.shape
    n = w.shape[2]
    assert tn % f32_width == 0 and f32_col0 % f32_width == 0
    return pl.pallas_call(
        functools.partial(_norm_matmul_body, f32_block=f32_col0 // tn, f32_off=f32_col0 % tn),
        grid=(m // tm, n // tn),
        in_specs=[
            pl.BlockSpec((tm, k), lambda i, j: (i, 0)),
            pl.BlockSpec((1, k), lambda i, j: (0, 0)),
            pl.BlockSpec((None, k, tn), lambda i, j: (layer, 0, j)),
        ],
        out_specs=[pl.BlockSpec((tm, tn), lambda i, j: (i, j)),
                   pl.BlockSpec((tm, f32_width), lambda i, j: (i, 0))],
        out_shape=[jax.ShapeDtypeStruct((m, n), BF16), jax.ShapeDtypeStruct((m, f32_width), F32)],
        scratch_shapes=[pltpu.VMEM((tm, k), BF16)],
        compiler_params=_cparams(("parallel", "arbitrary")),
        name="norm_matmul",
    )(x, g.reshape(1, k), w)


def _softplus(x):
    return jnp.maximum(x, 0.0) + jnp.log1p(jnp.exp(-jnp.abs(x)))


def _sigmoid(x):
    return 0.5 * jnp.tanh(0.5 * x) + 0.5


def _lru_body(x_ref, gate_ref, cw_ref, cb_ref, wa_ref, wx_ref, ba_ref, bx_ref, lam_ref,
              o_ref, xbuf_ref, h_ref, *, ts):
    t = pl.program_id(2)

    @pl.when(t == 0)
    def _():
        xbuf_ref[0:SUBLANES, :] = jnp.zeros((SUBLANES, LRU_CW), F32)
        h_ref[...] = jnp.zeros_like(h_ref)

    x = x_ref[...].astype(F32)
    xbuf_ref[SUBLANES:SUBLANES + ts, :] = x
    xc = cb_ref[...] + x * cw_ref[CONV_W - 1:CONV_W, :]
    for k in range(CONV_W - 1):
        back = CONV_W - 1 - k
        xc = xc + xbuf_ref[pl.ds(SUBLANES - back, ts), :] * cw_ref[k:k + 1, :]
    xbuf_ref[0:SUBLANES, :] = x[ts - SUBLANES:ts, :]

    xcb = xc.astype(BF16)
    ta = jnp.tanh(jnp.dot(xcb, wa_ref[...], preferred_element_type=F32) + ba_ref[...])
    tx = jnp.tanh(jnp.dot(xcb, wx_ref[...], preferred_element_type=F32) + bx_ref[...])
    c1 = (-0.25 * LRU_C) * _softplus(-lam_ref[...])
    th = jnp.tanh(c1 * ta + c1)
    rinv = 1.0 / (1.0 - th)
    a = (1.0 + th) * rinv
    xh = 0.5 * xc
    b = (2.0 * rinv) * jnp.sqrt(-th) * (xh * tx + xh)

    nt = ts // SUBLANES
    a = a.reshape(nt, SUBLANES, LRU_CW)
    b = b.reshape(nt, SUBLANES, LRU_CW)
    sub = lax.broadcasted_iota(jnp.int32, (nt, SUBLANES, LRU_CW), 1)
    s = 1
    while s < SUBLANES:
        keep = sub >= s
        b_sh = pltpu.roll(b, s, 1)
        a_sh = pltpu.roll(a, s, 1)
        b = b + jnp.where(keep, a * b_sh, 0.0)
        a = jnp.where(keep, a * a_sh, a)
        s *= 2
    carry = h_ref[...]
    tiles = []
    for k in range(nt):
        hk = b[k] + a[k] * carry
        tiles.append(hk)
        carry = hk[SUBLANES - 1:SUBLANES, :]
    h_ref[...] = carry
    h = jnp.concatenate(tiles, axis=0)
    o_ref[...] = (h * jax.nn.gelu(gate_ref[...].astype(F32))).astype(o_ref.dtype)


def _block_diag_tiles(w):
    n_blk, bw, _ = w.shape
    per = LRU_CW // bw
    wt = w.reshape(n_blk // per, per, bw, bw)
    eye = jnp.eye(per, dtype=w.dtype)
    tiles = jnp.einsum("cpkj,pq->cpkqj", wt, eye)
    return tiles.reshape(n_blk // per, LRU_CW, LRU_CW)


def lru_branch(proj3, x_col0, gate_col0, conv_w, conv_b, wa, ba, wx, bx, lam, *, ts):
    b_, s_, _ = proj3.shape
    width = conv_w.shape[1]
    n_c = width // LRU_CW
    xc0 = x_col0 // LRU_CW
    gc0 = gate_col0 // LRU_CW
    row = lambda v: v.reshape(1, width).astype(F32)
    vec_spec = pl.BlockSpec((1, LRU_CW), lambda b, c, t: (0, c))
    return pl.pallas_call(
        functools.partial(_lru_body, ts=ts),
        grid=(b_, n_c, s_ // ts),
        in_specs=[
            pl.BlockSpec((None, ts, LRU_CW), lambda b, c, t: (b, t, xc0 + c)),
            pl.BlockSpec((None, ts, LRU_CW), lambda b, c, t: (b, t, gc0 + c)),
            pl.BlockSpec((CONV_W, LRU_CW), lambda b, c, t: (0, c)),
            vec_spec,
            pl.BlockSpec((None, LRU_CW, LRU_CW), lambda b, c, t: (c, 0, 0)),
            pl.BlockSpec((None, LRU_CW, LRU_CW), lambda b, c, t: (c, 0, 0)),
            vec_spec, vec_spec, vec_spec,
        ],
        out_specs=pl.BlockSpec((None, ts, LRU_CW), lambda b, c, t: (b, t, c)),
        out_shape=jax.ShapeDtypeStruct((b_, s_, width), BF16),
        scratch_shapes=[pltpu.VMEM((ts + SUBLANES, LRU_CW), F32), pltpu.VMEM((1, LRU_CW), F32)],
        compiler_params=_cparams(("parallel", "parallel", "arbitrary")),
        name="rg_lru",
    )(proj3, proj3, conv_w.astype(F32), row(conv_b), _block_diag_tiles(0.5 * wa).astype(BF16),
      _block_diag_tiles(0.5 * wx).astype(BF16), row(0.5 * ba), row(0.5 * bx), row(lam))


def _rel_bias_body(tab_ref, o_ref):
    h = pl.program_id(0)
    width = 1 << (ATT_HWIN + ATT_HALF - 1).bit_length()
    m = lax.broadcasted_iota(jnp.int32, (1, width), 1)
    delta = jnp.where(m < ATT_HWIN, m, m - width)
    idx = jnp.clip(ATT_LEFT - delta, -MAX_REL, MAX_REL) + MAX_REL

    def pick(d, acc):
        return jnp.where(idx == d, tab_ref[h, d], acc)

    gvec = lax.fori_loop(0, N_REL, pick, jnp.zeros((1, width), F32), unroll=N_REL)
    full = pltpu.roll(jnp.broadcast_to(gvec, (ATT_HALF, width)), 0, 1, stride=1, stride_axis=0)
    chunk_shift = int(math.log2(CHUNK))
    qc = lax.shift_right_logical(lax.broadcasted_iota(jnp.int32, (ATT_HALF, ATT_HWIN), 0), chunk_shift)
    kc = lax.shift_right_logical(lax.broadcasted_iota(jnp.int32, (ATT_HALF, ATT_HWIN), 1), chunk_shift)
    in_band = (kc >= qc) & (kc <= qc + ATT_LEFT_CHUNKS)
    o_ref[...] = jnp.where(in_band, full[:, :ATT_HWIN] * LOG2_E, MASK_VALUE)


def rel_bias_matrix(rel_bias):
    n = rel_bias.shape[0]
    return pl.pallas_call(
        _rel_bias_body,
        grid=(n,),
        in_specs=[pl.BlockSpec(memory_space=pltpu.SMEM)],
        out_specs=pl.BlockSpec((None, ATT_HALF, ATT_HWIN), lambda h: (h, 0, 0)),
        out_shape=jax.ShapeDtypeStruct((n, ATT_HALF, ATT_HWIN), F32),
        compiler_params=_cparams(("arbitrary",)),
        name="rel_bias",
    )(rel_bias.astype(F32))


def _attn_body(q_ref, k0_ref, k1_ref, k2_ref, v0_ref, v1_ref, v2_ref, bias_ref, o_ref):
    qb = pl.program_id(2)
    nt = (((1,), (1,)), ((), ()))
    chunk_shift = int(math.log2(CHUNK))
    kc = lax.shift_right_logical(lax.broadcasted_iota(jnp.int32, (1, ATT_HWIN), 1), chunk_shift)
    k_refs = (k0_ref, k1_ref, k2_ref)
    v_refs = (v0_ref, v1_ref, v2_ref)
    units = [(r0, h) for r0 in range(0, ATT_QBLK, ATT_HALF) for h in range(ATT_HEADS_PER_STEP)]
    parts_of = lambda r0: ((0, r0, ATT_QBLK), (1, 0, ATT_QBLK), (2, 0, r0 + ATT_HALF))
    def score(r0, h):
        cols = slice(h * ATT_HD, (h + 1) * ATT_HD)
        first_chunk = (qb - 2) * (ATT_QBLK // CHUNK) + r0 // CHUNK
        before_start = jnp.where(kc + first_chunk >= 0, 0.0, MASK_VALUE)
        q = q_ref[r0:r0 + ATT_HALF, cols]
        s = jnp.concatenate(
            [lax.dot_general(q, k_refs[blk][a:b, cols], nt, preferred_element_type=F32)
             for blk, a, b in parts_of(r0)], axis=1)
        return s * (ATT_HD ** -0.5 * LOG2_E) + bias_ref[h] + before_start

    def softmax_terms(s):
        p = jnp.exp2(s - jnp.max(s, axis=-1, keepdims=True))
        return p.astype(BF16), jnp.sum(p, axis=-1, keepdims=True)

    def weighted_values(r0, h, pb, l):
        cols = slice(h * ATT_HD, (h + 1) * ATT_HD)
        o = jnp.zeros((ATT_HALF, ATT_HD), F32)
        c0 = 0
        for blk, a, b in parts_of(r0):
            o = o + jnp.dot(pb[:, c0:c0 + b - a], v_refs[blk][a:b, cols], preferred_element_type=F32)
            c0 += b - a
        o_ref[r0:r0 + ATT_HALF, cols] = (o / l).astype(o_ref.dtype)

    scores = [score(*u) for u in units]
    probs = [softmax_terms(s) for s in scores]
    for u, (pb, l) in zip(units, probs):
        weighted_values(*u, pb, l)


def attention_branch(proj3, q_col0, k_col0, v_col0, bias, layer, n_heads):
    b_, s_, _ = proj3.shape
    hp = ATT_HEADS_PER_STEP
    wblk = hp * ATT_HD
    qc0, kc0, vc0 = q_col0 // wblk, k_col0 // wblk, v_col0 // wblk

    def kv_spec(col0, back):
        return pl.BlockSpec((None, ATT_QBLK, wblk),
                            lambda h, b, i: (b, jnp.maximum(i - back, 0), col0 + h))

    return pl.pallas_call(
        _attn_body,
        grid=(n_heads // hp, b_, s_ // ATT_QBLK),
        in_specs=[
            pl.BlockSpec((None, ATT_QBLK, wblk), lambda h, b, i: (b, i, qc0 + h)),
            kv_spec(kc0, 2), kv_spec(kc0, 1), kv_spec(kc0, 0),
            kv_spec(vc0, 2), kv_spec(vc0, 1), kv_spec(vc0, 0),
            pl.BlockSpec((hp, ATT_HALF, ATT_HWIN), lambda h, b, i: (layer * (n_heads // hp) + h, 0, 0)),
        ],
        out_specs=pl.BlockSpec((None, ATT_QBLK, wblk), lambda h, b, i: (b, i, h)),
        out_shape=jax.ShapeDtypeStruct((b_, s_, n_heads * ATT_HD), BF16),
        compiler_params=_cparams(("parallel", "parallel", "parallel")),
        name="chunk_attention",
    )(proj3, proj3, proj3, proj3, proj3, proj3, proj3, bias)


def _s5_prep_body(are_c, aim_c, ls_c, are_r, aim_r, ls_r, ctre, ctim, bta, btb, w_ref, m_ref, lam_ref):
    blk = S5_BLK
    tw = S5_GT * SSM_GROUP
    sw = S5_GT * 2 * SSM_P
    lg_grp, lg_p = int(math.log2(SSM_GROUP)), int(math.log2(SSM_P))
    group_of_state = lambda q: lax.shift_right_logical(q, lg_p) & (S5_GT - 1)

    def lam_bar(are, aim, ls):
        step = jnp.exp(ls)
        mag = jnp.exp(are * step)
        return mag * jnp.cos(aim * step), mag * jnp.sin(aim * step)

    hw = sw // 2
    lr_c, li_c = lam_bar(are_c[...], aim_c[...], ls_c[...])
    rowc = lax.broadcasted_iota(jnp.int32, (hw, tw), 0)
    lanec = lax.broadcasted_iota(jnp.int32, (hw, tw), 1)
    same_c = group_of_state(rowc) == lax.shift_right_logical(lanec, lg_grp)
    cre = jnp.where(same_c, ctre[...], 0.0)
    cim = jnp.where(same_c, ctim[...], 0.0)
    pr = jnp.ones((hw, tw), F32)
    pi = jnp.zeros((hw, tw), F32)
    cl_re, cl_im = [], []
    for _ in range(blk + 1):
        cl_re.append(cre * pr - cim * pi)
        cl_im.append(-(cre * pi + cim * pr))
        pr, pi = pr * lr_c - pi * li_c, pr * li_c + pi * lr_c
    m_ref[0:hw, :] = jnp.concatenate(cl_re[1:], axis=1).astype(BF16)
    m_ref[hw:sw, :] = jnp.concatenate(cl_im[1:], axis=1).astype(BF16)
    cl0 = jnp.concatenate([jnp.concatenate(cl_re[:blk], axis=1),
                           jnp.concatenate(cl_im[:blk], axis=1)], axis=0)

    ar, ai = are_r[...], aim_r[...]
    lr, li = lam_bar(ar, ai, ls_r[...])
    den = ar * ar + ai * ai
    nr = lr - 1.0
    cr = (nr * ar + li * ai) / den
    ci = (li * ar - nr * ai) / den
    rowr = lax.broadcasted_iota(jnp.int32, (tw, sw), 0)
    laner = lax.broadcasted_iota(jnp.int32, (tw, sw), 1)
    same_r = lax.shift_right_logical(rowr, lg_grp) == group_of_state(laner)
    ba = jnp.where(same_r, bta[...], 0.0)
    bb = jnp.where(same_r, btb[...], 0.0)
    bb2 = cr * ba + ci * bb
    bb2s = cr * bb - ci * ba

    kt = jnp.dot(bb2, cl0, preferred_element_type=F32, precision=lax.Precision.HIGHEST).astype(BF16)
    pr = jnp.ones((1, sw), F32)
    pi = jnp.zeros((1, sw), F32)
    for i in range(blk - 1, -1, -1):
        rows = slice(i * tw, (i + 1) * tw)
        if i:
            w_ref[rows, 0:i * tw] = jnp.zeros((tw, i * tw), BF16)
        w_ref[rows, i * tw:blk * tw] = kt[:, 0:(blk - i) * tw]
        w_ref[rows, blk * tw:] = (pr * bb2 + pi * bb2s).astype(BF16)
        pr, pi = pr * lr - pi * li, pr * li + pi * lr
    sign = jnp.where(lax.broadcasted_iota(jnp.int32, (1, sw), 1) >= sw // 2, 1.0, -1.0)
    lam_ref[...] = jnp.zeros_like(lam_ref)
    lam_ref[0:1, :] = pr
    lam_ref[1:2, :] = sign * pi


def s5_prepare(a_re, a_im, b_re, b_im, c_re, c_im, log_step):
    n_g, npp = a_re.shape
    grp = b_re.shape[2]
    blk, gt = S5_BLK, S5_GT
    n_t = n_g // gt
    tw, sw = gt * grp, gt * 2 * npp
    f = lambda v: v.astype(F32)
    pair = lambda v: jnp.concatenate([f(v).reshape(n_t, gt * npp)] * 2, axis=-1)
    hw = sw // 2
    as_col = lambda v: jnp.broadcast_to(f(v).reshape(n_t, hw, 1), (n_t, hw, tw))
    as_row = lambda v: v.reshape(n_t, 1, sw)
    ls = jnp.broadcast_to(f(log_step)[:, None], (n_g, npp))
    ct = lambda v: jnp.tile(f(v).transpose(0, 2, 1).reshape(n_t, hw, grp), (1, 1, gt))
    b_t = lambda v: jnp.broadcast_to(f(v).transpose(0, 2, 1).reshape(n_t, gt, grp, 1, npp),
                                     (n_t, gt, grp, gt, npp)).reshape(n_t, tw, gt * npp)
    btr, bti = b_t(b_re), b_t(b_im)
    bt = lambda lo, hi: jnp.concatenate([lo, hi], axis=-1)
    t3 = lambda shp: pl.BlockSpec((None,) + shp, lambda t: (t, 0, 0))
    return pl.pallas_call(
        _s5_prep_body,
        grid=(n_t,),
        in_specs=[t3((hw, tw)), t3((hw, tw)), t3((hw, tw)), t3((1, sw)), t3((1, sw)), t3((1, sw)),
                  t3((hw, tw)), t3((hw, tw)), t3((tw, sw)), t3((tw, sw))],
        out_specs=[t3((blk * tw, blk * tw + sw)), t3((sw, blk * tw)), t3((SUBLANES, sw))],
        out_shape=[jax.ShapeDtypeStruct((n_t, blk * tw, blk * tw + sw), BF16),
                   jax.ShapeDtypeStruct((n_t, sw, blk * tw), BF16),
                   jax.ShapeDtypeStruct((n_t, SUBLANES, sw), F32)],
        compiler_params=_cparams(("parallel",)),
        name="s5_prepare",
    )(as_col(a_re), as_col(a_im), as_col(ls), as_row(pair(a_re)), as_row(pair(a_im)),
      as_row(pair(ls)), ct(c_re), ct(c_im), bt(btr, bti), bt(-bti, btr))


def _s5_body(u_ref, w_ref, m_ref, lam_ref, d_ref, o_ref, *, rows):
    blk = S5_BLK
    tw = S5_GT * SSM_GROUP
    sw = lam_ref.shape[1]
    u_steps = [u_ref[pl.ds(i, rows, stride=blk), :] for i in range(blk)]
    lhs = jnp.concatenate([u.astype(BF16) for u in u_steps], axis=1)
    z = jnp.dot(lhs, w_ref[:, blk * tw:], preferred_element_type=F32)
    qw = blk * tw // 4
    r_q = [jnp.dot(lhs[:, 0:(q + 1) * qw], w_ref[0:(q + 1) * qw, q * qw:(q + 1) * qw],
                   preferred_element_type=F32) for q in range(4)]
    row = lax.broadcasted_iota(jnp.int32, (rows, sw), 0)
    ca = lam_ref[0:1, :]
    cb = lam_ref[1:2, :]
    s = 1
    while s < rows:
        zs = jnp.where(row >= s, pltpu.roll(z, s, 0), 0.0)
        z = z + ca * zs + cb * pltpu.roll(zs, sw // 2, 1)
        ca, cb = ca * ca - cb * cb, 2.0 * ca * cb
        s *= 2
    zprev = jnp.where(row >= 1, pltpu.roll(z, 1, 0), 0.0)
    ys = jnp.dot(zprev.astype(BF16), m_ref[...], preferred_element_type=F32)
    d = d_ref[...]
    for j in range(blk):
        q, off = divmod(j * tw, qw)
        y = r_q[q][:, off:off + tw] + ys[:, j * tw:(j + 1) * tw] + d * u_steps[j]
        o_ref[pl.ds(j, rows, stride=blk), :] = jax.nn.gelu(y)


def s5_branch(u, w, mm, lam, d, *, batch):
    t_, width = u.shape
    seq = t_ // batch
    n_t, sw = w.shape[0], lam.shape[2]
    tw = S5_GT * SSM_GROUP
    return pl.pallas_call(
        functools.partial(_s5_body, rows=seq // S5_BLK),
        grid=(n_t, batch),
        in_specs=[
            pl.BlockSpec((seq, tw), lambda c, b: (b, c)),
            pl.BlockSpec((None,) + w.shape[1:], lambda c, b: (c, 0, 0)),
            pl.BlockSpec((None,) + mm.shape[1:], lambda c, b: (c, 0, 0)),
            pl.BlockSpec((None, SUBLANES, sw), lambda c, b: (c, 0, 0)),
            pl.BlockSpec((1, tw), lambda c, b: (0, c)),
        ],
        out_specs=pl.BlockSpec((seq, tw), lambda c, b: (b, c)),
        out_shape=jax.ShapeDtypeStruct((t_, width), F32),
        compiler_params=_cparams(("parallel", "arbitrary")),
        name="s5_blocked",
    )(u, w, mm, lam, d)


def _merge_body(ya_ref, yb_ref, yc_ref, w0_ref, w1_ref, w2_ref, wg_ref,
                g0_ref, g1_ref, g2_ref, gb_ref, o_ref):
    hm, hn = o_ref.shape[0] // 2, o_ref.shape[1] // 2
    pieces = [(slice(r, r + hm), slice(c, c + hn)) for r in (0, hm) for c in (0, hn)]
    dot = lambda a_ref, w, r, c: jnp.dot(a_ref[r, :].astype(BF16), w[:, c], preferred_element_type=F32)
    prods = [(dot(ya_ref, w0_ref, r, c), dot(yb_ref, w1_ref, r, c), dot(yc_ref, w2_ref, r, c),
              dot(yc_ref, wg_ref, r, c)) for r, c in pieces]
    gate = lambda g_ref, i, r, c: _sigmoid(g_ref[r, c].astype(F32) + gb_ref[i:i + 1, c])
    for (r, c), (br_a, br_b, br_c, glu) in zip(pieces, prods):
        merged = (gate(g0_ref, 0, r, c) * br_a + gate(g1_ref, 1, r, c) * br_b
                  + gate(g2_ref, 2, r, c) * (br_c * _sigmoid(glu)))
        o_ref[r, c] = merged.astype(o_ref.dtype)


def branch_merge(ya, yb, yc, w_branch, w_glu, layer, proj, gate_col0, gate_bias, *, tm, tn):
    m, kdim = ya.shape
    n = w_glu.shape[2]
    gc0 = gate_col0 // tn
    nj = n // tn
    y_spec = pl.BlockSpec((tm, kdim), lambda i, j: (i, 0))
    wb_spec = lambda br: pl.BlockSpec((None, None, kdim, tn), lambda i, j: (layer, br, 0, j))
    g_spec = lambda br: pl.BlockSpec((tm, tn), lambda i, j: (i, gc0 + br * nj + j))
    return pl.pallas_call(
        _merge_body,
        grid=(m // tm, nj),
        in_specs=[y_spec, y_spec, y_spec, wb_spec(0), wb_spec(1), wb_spec(2),
                  pl.BlockSpec((None, kdim, tn), lambda i, j: (layer, 0, j)),
                  g_spec(0), g_spec(1), g_spec(2),
                  pl.BlockSpec((N_BRANCH, tn), lambda i, j: (0, j))],
        out_specs=pl.BlockSpec((tm, tn), lambda i, j: (i, j)),
        out_shape=jax.ShapeDtypeStruct((m, n), BF16),
        compiler_params=_cparams(("parallel", "arbitrary")),
        name="branch_merge",
    )(ya, yb, yc, w_branch, w_branch, w_branch, w_glu, proj, proj, proj, gate_bias.astype(F32))


def _matmul_res_body(a_ref, w_ref, r_ref, g_ref, o_ref, h_ref):
    x = r_ref[...] + jnp.dot(a_ref[...], w_ref[...], preferred_element_type=F32)
    o_ref[...] = x
    h_ref[...] = _rmsnorm_rows(x, g_ref[...]).astype(BF16)


def matmul_residual(a, w, layer, res, g, *, tm):
    m, k = a.shape
    n = w.shape[2]
    row_spec = pl.BlockSpec((tm, n), lambda i: (i, 0))
    return pl.pallas_call(
        _matmul_res_body,
        grid=(m // tm,),
        in_specs=[pl.BlockSpec((tm, k), lambda i: (i, 0)),
                  pl.BlockSpec((None, k, n), lambda i: (layer, 0, 0)),
                  row_spec,
                  pl.BlockSpec((1, n), lambda i: (0, 0))],
        out_specs=[row_spec, row_spec],
        out_shape=[jax.ShapeDtypeStruct((m, n), F32), jax.ShapeDtypeStruct((m, n), BF16)],
        compiler_params=_cparams(("parallel",)),
        name="matmul_residual",
    )(a, w, res, g.reshape(1, n))


def _ffn_body(x_ref, h_ref, wg_ref, wu_ref, wd_ref, og_ref, o_ref, *, norm_output):
    f = pl.program_id(1)

    @pl.when(f == 0)
    def _():
        o_ref[...] = x_ref[...]

    h = h_ref[...]
    gate = jnp.dot(h, wg_ref[...], preferred_element_type=F32)
    up = jnp.dot(h, wu_ref[...], preferred_element_type=F32)
    act = (gate * _sigmoid(gate) * up).astype(BF16)
    for c in range(0, o_ref.shape[1], FFN_ACC_COLS):
        cols = slice(c, c + FFN_ACC_COLS)
        o_ref[:, cols] += jnp.dot(act, wd_ref[:, cols], preferred_element_type=F32)

    if norm_output:
        @pl.when(f == pl.num_programs(1) - 1)
        def _():
            o_ref[...] = _rmsnorm_rows(o_ref[...], og_ref[...])


def ffn(x, h, w_gate, w_up, w_down, layer, out_g, *, norm_output, tm, tf):
    m, d = x.shape
    hidden = w_gate.shape[2]
    return pl.pallas_call(
        functools.partial(_ffn_body, norm_output=norm_output),
        grid=(m // tm, hidden // tf),
        in_specs=[pl.BlockSpec((tm, d), lambda i, f: (i, 0)),
                  pl.BlockSpec((tm, d), lambda i, f: (i, 0)),
                  pl.BlockSpec((None, d, tf), lambda i, f: (layer, 0, f)),
                  pl.BlockSpec((None, d, tf), lambda i, f: (layer, 0, f)),
                  pl.BlockSpec((None, tf, d), lambda i, f: (layer, f, 0)),
                  pl.BlockSpec((1, d), lambda i, f: (0, 0))],
        out_specs=pl.BlockSpec((tm, d), lambda i, f: (i, 0)),
        out_shape=jax.ShapeDtypeStruct((m, d), F32),
        compiler_params=_cparams(("parallel", "arbitrary")),
        name="swiglu_ffn",
    )(x, h, w_gate, w_up, w_down, out_g.reshape(1, d))


def _pick_tile(n, want):
    t = min(n, want)
    while n % t:
        t //= 2
    return t


def kernel(x, norm_mix_g, w_in, gate_bias, lru_conv_w, lru_conv_b, lru_wa, lru_ba, lru_wx, lru_bx,
           lru_lambda, attn_rel_bias, ssm_a_re, ssm_a_im, ssm_b_re, ssm_b_im, ssm_c_re, ssm_c_im,
           ssm_d, ssm_log_step, ssm_w_glu, w_branch, w_out, norm_ffn_g, w_ffn_gate, w_ffn_up,
           w_ffn_down, norm_final_g):
    b_, s_, d_model = x.shape
    depth = w_in.shape[0]
    mix_w = lru_conv_w.shape[2]
    n_heads = attn_rel_bias.shape[1]
    n_g = ssm_a_re.shape[1]
    t_ = b_ * s_
    assert s_ % ATT_QBLK == 0 and s_ % S5_BLK == 0 and n_g % S5_GT == 0
    c_lru_x, c_lru_g, c_q, c_k, c_v, c_u, c_gate = [i * mix_w for i in range(7)]
    in_w = w_in.shape[2]

    tm = _pick_tile(t_, 1024)
    ts_lru = _pick_tile(s_, 2048)
    bias = rel_bias_matrix(attn_rel_bias.reshape(depth * n_heads, N_REL))
    w_in, w_branch, ssm_w_glu, w_out, w_ffn_gate, w_ffn_up, w_ffn_down = [
        w.astype(BF16) for w in (w_in, w_branch, ssm_w_glu, w_out, w_ffn_gate, w_ffn_up, w_ffn_down)]

    xf = x.reshape(t_, d_model).astype(F32)
    for l in range(depth):
        proj, u32 = norm_matmul(xf, norm_mix_g[l], w_in, l, tm=tm, tn=2 * mix_w, f32_col0=c_u, f32_width=mix_w)
        proj3 = proj.reshape(b_, s_, in_w)
        y_a = lru_branch(proj3, c_lru_x, c_lru_g, lru_conv_w[l], lru_conv_b[l], lru_wa[l], lru_ba[l],
                         lru_wx[l], lru_bx[l], lru_lambda[l], ts=ts_lru)
        y_b = attention_branch(proj3, c_q, c_k, c_v, bias, l, n_heads)

        w1, mm, lam = s5_prepare(ssm_a_re[l], ssm_a_im[l], ssm_b_re[l], ssm_b_im[l],
                                 ssm_c_re[l], ssm_c_im[l], ssm_log_step[l])
        y_c = s5_branch(u32, w1, mm, lam, ssm_d[l].astype(F32).reshape(1, mix_w), batch=b_)

        merged = branch_merge(y_a.reshape(t_, mix_w), y_b.reshape(t_, mix_w), y_c, w_branch, ssm_w_glu, l,
                              proj, c_gate, gate_bias[l], tm=tm, tn=512)
        xf, h_ffn = matmul_residual(merged, w_out, l, xf, norm_ffn_g[l].astype(F32), tm=_pick_tile(t_, 512))
        xf = ffn(xf, h_ffn, w_ffn_gate, w_ffn_up, w_ffn_down, l, norm_final_g.astype(F32),
                 norm_output=(l == depth - 1), tm=tm, tf=512)
    return xf.reshape(b_, s_, d_model).astype(x.dtype)
```
